```python
import math
import jax, jax.numpy as jnp
from jax import lax
import numpy as np

D_MODEL = 2048
BATCH = 1
SEQ = 8192
DEPTH = 1
DEC_BATCH = 32
DEC_SEQ = 1
PAST_LEN = 8192
PAGE_SIZE = 128

H_HG = 8
HG_DK = 128
HG_DV = 128
HG_W = H_HG * HG_DK
HG_VW = H_HG * HG_DV
HG_CHUNK = 64
H_ATT = 8
ATT_DH = 128
ATT_W = H_ATT * ATT_DH
MOBA_BLOCK = 256
MOBA_TOPK = 3
Q_BLOCK = 64
N_BUCKETS = 32
MAX_EXACT = N_BUCKETS // 2
REL_MAX_DIST = 1024
D_FF = 5632
CONV_W = 3
EPS = 1e-6
N_IN = 2 * HG_W + 2 * HG_VW + 3 * ATT_W + 2 * D_MODEL

kernel_name = "hgrn2_moba_gated_hybrid_step"


def rms_norm(x, w):
    xf = x.astype(jnp.float32)
    y = xf * lax.rsqrt(jnp.mean(xf * xf, axis=-1, keepdims=True) + EPS)
    return (y * w.astype(jnp.float32)).astype(x.dtype)


def rel_bucket(n):
    n = jnp.maximum(n, 0)
    nf = jnp.maximum(n, 1).astype(jnp.float32)
    large = MAX_EXACT + (jnp.log(nf / MAX_EXACT) / math.log(REL_MAX_DIST / MAX_EXACT)
                         * (N_BUCKETS - MAX_EXACT)).astype(jnp.int32)
    large = jnp.minimum(large, N_BUCKETS - 1)
    return jnp.where(n < MAX_EXACT, n, large)


def hgrn2_recurrence(q, k, v, logf, s0):
    B, T, H, _ = q.shape
    c = min(HG_CHUNK, T)
    n = -(-T // c)
    pad = n * c - T

    def blocks(a):
        a = jnp.pad(a.astype(jnp.float32), ((0, 0), (0, pad), (0, 0), (0, 0)))
        return a.reshape(B, n, c, H, a.shape[-1]).transpose(1, 0, 2, 3, 4)

    causal = jnp.tril(jnp.ones((c, c), bool))[None, :, :, None, None]

    def step(S, inp):
        qc, kc, vc, gc = inp
        b = jnp.cumsum(gc, axis=1)
        o_inter = jnp.einsum('bthk,bhkv->bthv', qc * jnp.exp(b), S)
        decay = jnp.exp(jnp.where(causal, b[:, :, None] - b[:, None, :], -jnp.inf))
        scores = jnp.einsum('bthk,bshk,btshk->btsh', qc, kc, decay)
        o_intra = jnp.einsum('btsh,bshv->bthv', scores, vc)
        b_last = b[:, -1]
        S = jnp.exp(b_last)[..., None] * S + jnp.einsum(
            'bshk,bshv->bhkv', kc * jnp.exp(b_last[:, None] - b), vc)
        return S, o_inter + o_intra

    S, o = lax.scan(step, s0.astype(jnp.float32), (blocks(q), blocks(k), blocks(v), blocks(logf)))
    o = o.transpose(1, 0, 2, 3, 4).reshape(B, n * c, H, v.shape[-1])[:, :T]
    return o, S


def moba_attention(q, k, v, q_pos, rel_bias):
    B, Tq = q.shape[:2]
    L = k.shape[1]
    nb = -(-L // MOBA_BLOCK)
    kpad = nb * MOBA_BLOCK - L
    k = jnp.pad(k, ((0, 0), (0, kpad), (0, 0), (0, 0)))
    v = jnp.pad(v, ((0, 0), (0, kpad), (0, 0), (0, 0)))
    kb = k.reshape(B, nb, MOBA_BLOCK, H_ATT, ATT_DH)
    vb = v.reshape(B, nb, MOBA_BLOCK, H_ATT, ATT_DH)
    k_mean = jnp.mean(kb.astype(jnp.float32), axis=2)
    n_extra = max(0, MOBA_TOPK - nb)
    qb = min(Q_BLOCK, Tq)
    nq = -(-Tq // qb)
    qpad = nq * qb - Tq
    q = jnp.pad(q, ((0, 0), (0, qpad), (0, 0), (0, 0)))
    q_pos = jnp.pad(q_pos, (0, qpad), mode='edge')
    q_chunks = q.reshape(B, nq, qb, H_ATT, ATT_DH).transpose(1, 0, 2, 3, 4)
    pos_chunks = q_pos.reshape(nq, qb)
    b_idx = jnp.arange(B)[:, None, None, None]
    h_idx = jnp.arange(H_ATT)[None, :, None, None]
    offs = jnp.arange(MOBA_BLOCK, dtype=jnp.int32)
    bias_tab = rel_bias.astype(jnp.float32)
    scale = ATT_DH ** -0.5

    def attend(args):
        qc, pc = args
        qc = qc.astype(jnp.float32) * scale
        own = pc // MOBA_BLOCK
        s = jnp.einsum('bqhd,bnhd->bhqn', qc, k_mean)
        s = jnp.where(jnp.arange(nb)[None, :] < own[:, None], s, -jnp.inf)
        if n_extra:
            s = jnp.concatenate([s, jnp.full((B, H_ATT, qb, n_extra), -jnp.inf, jnp.float32)], axis=-1)
        _, sel = lax.top_k(s, MOBA_TOPK)
        sel_valid = jnp.arange(MOBA_TOPK)[None, :] < own[:, None]
        sel = jnp.where(sel_valid, sel, 0)
        blocks = jnp.concatenate(
            [sel, jnp.broadcast_to(own[None, None, :, None], (B, H_ATT, qb, 1))], axis=-1)
        blk_valid = jnp.concatenate([sel_valid, jnp.ones((qb, 1), bool)], axis=-1)
        kg = kb[b_idx, blocks, :, h_idx, :].astype(jnp.float32)
        vg = vb[b_idx, blocks, :, h_idx, :].astype(jnp.float32)
        logits = jnp.einsum('bqhd,bhqsnd->bhqsn', qc, kg)
        rel = pc[None, None, :, None, None] - (blocks[..., None] * MOBA_BLOCK + offs)
        mask = (rel >= 0) & blk_valid[None, None, :, :, None]
        bias = bias_tab[rel_bucket(rel), h_idx[..., None]]
        logits = jnp.where(mask, logits + bias, -jnp.inf)
        p = jax.nn.softmax(logits.reshape(B, H_ATT, qb, -1), axis=-1)
        return jnp.einsum('bhqm,bhqmd->bqhd', p, vg.reshape(B, H_ATT, qb, -1, ATT_DH))

    out = lax.map(attend, (q_chunks, pos_chunks))
    return out.transpose(1, 0, 2, 3, 4).reshape(B, nq * qb, H_ATT, ATT_DH)[:, :Tq]


def decoder_layer(x, k_past, v_past, hg_state, ffn_buf, lb, rel_bias,
                  n_mix_pre, n_mix_post, n_ffn_pre, n_ffn_post, w_in, hg_out_norm,
                  w_branch_hgrn, w_branch_moba, w_out, w_ffn_up, ffn_conv_w, ffn_conv_b, w_ffn_down):
    B, T, _ = x.shape
    P = k_past.shape[1]
    f32 = jnp.float32
    h = rms_norm(x, n_mix_pre)
    proj = h @ w_in
    sizes = [HG_W, HG_W, HG_VW, HG_VW, ATT_W, ATT_W, ATT_W, D_MODEL, D_MODEL]
    hq, hf, hi, hg_og, aq, ak, av, g_hg, g_att = jnp.split(proj, np.cumsum(sizes)[:-1].tolist(), axis=-1)

    qh = jax.nn.silu(hq.astype(f32)).reshape(B, T, H_HG, HG_DK)
    f = lb + (1.0 - lb) * jax.nn.sigmoid(hf.astype(f32))
    logf = jnp.log(f).reshape(B, T, H_HG, HG_DK)
    kh = (1.0 - f).reshape(B, T, H_HG, HG_DK)
    vh = hi.reshape(B, T, H_HG, HG_DV)
    o_hg, hg_new = hgrn2_recurrence(qh, kh, vh, logf, hg_state)
    o_hg = rms_norm(o_hg, hg_out_norm) * jax.nn.silu(hg_og.astype(f32).reshape(B, T, H_HG, HG_DV))
    br_hg = o_hg.reshape(B, T, HG_VW).astype(x.dtype) @ w_branch_hgrn

    aq = aq.reshape(B, T, H_ATT, ATT_DH)
    ak = ak.reshape(B, T, H_ATT, ATT_DH)
    av = av.reshape(B, T, H_ATT, ATT_DH)
    k_all = jnp.concatenate([k_past.astype(ak.dtype), ak], axis=1)
    v_all = jnp.concatenate([v_past.astype(av.dtype), av], axis=1)
    q_pos = P + jnp.arange(T, dtype=jnp.int32)
    o_att = moba_attention(aq, k_all, v_all, q_pos, rel_bias)
    br_att = o_att.reshape(B, T, ATT_W).astype(x.dtype) @ w_branch_moba

    merged = jax.nn.sigmoid(g_hg) * br_hg + jax.nn.sigmoid(g_att) * br_att
    x = x + rms_norm(merged @ w_out, n_mix_post)

    h2 = rms_norm(x, n_ffn_pre)
    gu = h2 @ w_ffn_up
    g, u = gu[..., :D_FF], gu[..., D_FF:]
    ext = jnp.concatenate([ffn_buf.astype(g.dtype), g], axis=1)
    gc = ffn_conv_b + sum(ffn_conv_w[w] * ext[:, w:w + T] for w in range(CONV_W))
    y = (jax.nn.gelu(gc, approximate=True) * u) @ w_ffn_down
    x = x + rms_norm(y, n_ffn_post)
    return x, ak, av, hg_new, ext[:, -(CONV_W - 1):]


def setup_inputs(seed: int = 0) -> dict:
    key = jax.random.key(seed)
    ks = jax.random.split(key, 24)
    n_pages = PAST_LEN // PAGE_SIZE
    n_pool = (DEC_BATCH * n_pages * 5) // 4
    nrm = jax.random.normal
    f32 = jnp.float32

    def gain(k, shape):
        return 1.0 + 0.1 * nrm(k, shape, f32)

    perm = jax.random.permutation(ks[0], n_pool)[:DEC_BATCH * n_pages]
    return {
        "x_prompt": nrm(ks[1], (BATCH, SEQ, D_MODEL), f32),
        "x_sample": nrm(ks[2], (DEC_BATCH, DEC_SEQ, D_MODEL), f32),
        "cache_k": nrm(ks[3], (DEPTH, n_pool, PAGE_SIZE, H_ATT, ATT_DH), f32),
        "cache_v": nrm(ks[4], (DEPTH, n_pool, PAGE_SIZE, H_ATT, ATT_DH), f32),
        "state_hgrn": 0.5 * nrm(ks[5], (DEPTH, DEC_BATCH, H_HG, HG_DK, HG_DV), f32),
        "state_ffn_conv": nrm(ks[6], (DEPTH, DEC_BATCH, CONV_W - 1, D_FF), f32),
        "page_table": perm.reshape(DEC_BATCH, n_pages).astype(jnp.int32),
        "rel_bias": 0.5 * nrm(ks[7], (N_BUCKETS, H_ATT), f32),
        "hg_lb": nrm(ks[8], (DEPTH + 1, HG_W), f32),
        "norm_mix_pre": gain(ks[9], (DEPTH, D_MODEL)),
        "norm_mix_post": gain(ks[10], (DEPTH, D_MODEL)),
        "norm_ffn_pre": gain(ks[11], (DEPTH, D_MODEL)),
        "norm_ffn_post": gain(ks[12], (DEPTH, D_MODEL)),
        "w_in": nrm(ks[13], (DEPTH, D_MODEL, N_IN), f32) * D_MODEL ** -0.5,
        "hg_out_norm": gain(ks[14], (DEPTH, HG_DV)),
        "w_branch_hgrn": nrm(ks[15], (DEPTH, HG_VW, D_MODEL), f32) * HG_VW ** -0.5,
        "w_branch_moba": nrm(ks[16], (DEPTH, ATT_W, D_MODEL), f32) * ATT_W ** -0.5,
        "w_out": nrm(ks[17], (DEPTH, D_MODEL, D_MODEL), f32) * D_MODEL ** -0.5,
        "w_ffn_up": nrm(ks[18], (DEPTH, D_MODEL, 2 * D_FF), f32) * D_MODEL ** -0.5,
        "ffn_conv_w": nrm(ks[19], (DEPTH, CONV_W, D_FF), f32) * CONV_W ** -0.5,
        "ffn_conv_b": 0.02 * nrm(ks[20], (DEPTH, D_FF), f32),
        "w_ffn_down": nrm(ks[21], (DEPTH, D_FF, D_MODEL), f32) * D_FF ** -0.5,
    }


def reference(x_prompt, x_sample, cache_k, cache_v, state_hgrn, state_ffn_conv, page_table,
              rel_bias, hg_lb, norm_mix_pre, norm_mix_post, norm_ffn_pre, norm_ffn_post,
              w_in, hg_out_norm, w_branch_hgrn, w_branch_moba, w_out,
              w_ffn_up, ffn_conv_w, ffn_conv_b, w_ffn_down):
    B = x_prompt.shape[0]
    DB = x_sample.shape[0]
    n_past = page_table.shape[1] * cache_k.shape[2]
    lower_bounds = jnp.cumsum(jax.nn.softmax(hg_lb.astype(jnp.float32), axis=0), axis=0)
    yp, ys = x_prompt, x_sample
    kp_l, vp_l, sp_l, cp_l, ks_l, vs_l, ss_l, cs_l = [], [], [], [], [], [], [], []
    for l in range(DEPTH):
        lw = (lower_bounds[l], rel_bias, norm_mix_pre[l], norm_mix_post[l], norm_ffn_pre[l],
              norm_ffn_post[l], w_in[l], hg_out_norm[l], w_branch_hgrn[l], w_branch_moba[l],
              w_out[l], w_ffn_up[l], ffn_conv_w[l], ffn_conv_b[l], w_ffn_down[l])
        empty = jnp.zeros((B, 0, H_ATT, ATT_DH), x_prompt.dtype)
        yp, kp, vp, sp, cp = decoder_layer(
            yp, empty, empty, jnp.zeros((B, H_HG, HG_DK, HG_DV), jnp.float32),
            jnp.zeros((B, CONV_W - 1, D_FF), x_prompt.dtype), *lw)
        k_past = cache_k[l][page_table].reshape(DB, n_past, H_ATT, ATT_DH)
        v_past = cache_v[l][page_table].reshape(DB, n_past, H_ATT, ATT_DH)
        ys, kss, vss, sss, css = decoder_layer(ys, k_past, v_past, state_hgrn[l], state_ffn_conv[l], *lw)
        kp_l.append(kp); vp_l.append(vp); sp_l.append(sp); cp_l.append(cp)
        ks_l.append(kss); vs_l.append(vss); ss_l.append(sss); cs_l.append(css)
    return (yp, ys, jnp.stack(kp_l), jnp.stack(vp_l), jnp.stack(sp_l), jnp.stack(cp_l),
            jnp.stack(ks_l), jnp.stack(vs_l), jnp.stack(ss_l), jnp.stack(cs_l))
```

```python
import functools
import math

import numpy as np
import jax
import jax.numpy as jnp
from jax import lax
from jax.experimental import pallas as pl
from jax.experimental.pallas import tpu as pltpu

F32 = jnp.float32
BF16 = jnp.bfloat16

D_MODEL = 2048
H_HG = 8
HG_DK = 128
HG_DV = 128
HG_W = H_HG * HG_DK
HG_VW = H_HG * HG_DV
H_ATT = 8
ATT_DH = 128
ATT_W = H_ATT * ATT_DH
MOBA_BLOCK = 256
MOBA_TOPK = 3
N_BUCKETS = 32
MAX_EXACT = N_BUCKETS // 2
REL_MAX_DIST = 1024
D_FF = 5632
CONV_W = 3
EPS = 1e-6
N_IN = 2 * HG_W + 2 * HG_VW + 3 * ATT_W + 2 * D_MODEL
ATT_SCALE = ATT_DH ** -0.5

COL_HQ, COL_HF, COL_HI, COL_OG = 0, 8, 16, 24
COL_AQ, COL_AK, COL_AV = 32, 40, 48
COLK_AK, COLK_GHG, COLK_GATT = 5, 7, 9

HG_CHUNK = 128
HG_SUB = 8
VMEM_LIMIT = 56 * 1024 * 1024
NEG_INF = float("-inf")


def _bucket_thresholds():
    n = np.arange(MAX_EXACT, 4 * REL_MAX_DIST, dtype=np.float64)
    large = MAX_EXACT + (np.log(n / MAX_EXACT) / math.log(REL_MAX_DIST / MAX_EXACT)
                         * (N_BUCKETS - MAX_EXACT)).astype(np.int64)
    large = np.minimum(large, N_BUCKETS - 1)
    return [int(n[np.argmax(large >= b)]) for b in range(MAX_EXACT + 1, N_BUCKETS)]


BUCKET_THRESHOLDS = _bucket_thresholds()


def _cparams(*sem):
    return pltpu.CompilerParams(dimension_semantics=sem, vmem_limit_bytes=VMEM_LIMIT)


def _silu(x):
    return x * jax.nn.sigmoid(x)


def _dot(a, b):
    return jnp.dot(a, b, preferred_element_type=F32)


def _dot_nt(a, b, precision=None):
    return lax.dot_general(a, b, (((1,), (1,)), ((), ())), precision=precision,
                           preferred_element_type=F32)


def _norm_matmul_kernel(x_ref, g_ref, w_ref, o_ref, h_ref):
    @pl.when(pl.program_id(1) == 0)
    def _():
        x = x_ref[...]
        ms = jnp.mean(x * x, axis=-1, keepdims=True)
        h_ref[...] = (x * lax.rsqrt(ms + EPS) * g_ref[...]).astype(BF16)

    o_ref[...] = _dot(h_ref[...], w_ref[...])


def _norm_matmul(x, gain, w_bf16, tm, tn):
    m, k = x.shape
    n = w_bf16.shape[1]
    return pl.pallas_call(
        _norm_matmul_kernel,
        grid=(m // tm, n // tn),
        in_specs=[pl.BlockSpec((tm, k), lambda i, j: (i, 0)),
                  pl.BlockSpec((1, k), lambda i, j: (0, 0)),
                  pl.BlockSpec((k, tn), lambda i, j: (0, j))],
        out_specs=pl.BlockSpec((tm, tn), lambda i, j: (i, j)),
        out_shape=jax.ShapeDtypeStruct((m, n), F32),
        scratch_shapes=[pltpu.VMEM((tm, k), BF16)],
        compiler_params=_cparams("parallel", "arbitrary"),
        name="norm_in_proj",
    )(x, gain.reshape(1, k), w_bf16)


def _forget_lower_bound(lbraw):
    e = jnp.exp(lbraw - jnp.max(lbraw, axis=0, keepdims=True))
    return e[0:1] / jnp.sum(e, axis=0, keepdims=True)


def _hgrn_seq_kernel(hq_ref, hf_ref, hi_ref, og_ref, lbraw_ref, gn_ref, o_ref, s_out_ref,
                     st_ref, q_s, k_s, b_s, v_s, acc_s, *, chunk):
    c = pl.program_id(1)

    @pl.when(c == 0)
    def _():
        st_ref[...] = jnp.zeros_like(st_ref)

    lb = _forget_lower_bound(lbraw_ref[...])
    q = _silu(hq_ref[...])
    f = lb + (1.0 - lb) * jax.nn.sigmoid(hf_ref[...])
    k = 1.0 - f
    v = hi_ref[...]

    b = jnp.log(f)
    row = lax.broadcasted_iota(jnp.int32, (chunk, HG_DK), 0)
    shift = 1
    while shift < chunk:
        b = b + jnp.where(row >= shift, pltpu.roll(b, shift, axis=0), 0.0)
        shift *= 2

    q_s[...] = q
    k_s[...] = k
    b_s[...] = b
    v_s[...] = v

    st = st_ref[...]
    o = _dot_nt((q * jnp.exp(b)).astype(BF16), st.astype(BF16))

    rr = lax.broadcasted_iota(jnp.int32, (chunk, chunk), 0)
    cc = lax.broadcasted_iota(jnp.int32, (chunk, chunk), 1)
    a = jnp.zeros((chunk, chunk), F32)
    hs = HG_SUB
    while hs < chunk:
        blk = 2 * hs
        ref_rows = jnp.concatenate(
            [jnp.broadcast_to(b_s[m0 + hs - 1:m0 + hs, :], (blk, HG_DK)) for m0 in range(0, chunk, blk)], axis=0)
        second = (row & (blk - 1)) >= hs
        qd = jnp.where(second, q * jnp.exp(jnp.minimum(b - ref_rows, 0.0)), 0.0)
        kd = jnp.where(second, 0.0, k * jnp.exp(jnp.minimum(ref_rows - b, 0.0)))
        a_l = _dot_nt(qd.astype(BF16), kd.astype(BF16))
        sh = int(math.log2(blk))
        a = a + jnp.where((rr >> sh) == (cc >> sh), a_l, 0.0)
        hs = blk
    o = o + _dot(a.astype(BF16), v.astype(BF16))

    sub = lax.broadcasted_iota(jnp.int32, (HG_SUB, HG_DK), 0)

    def diag(i, carry):
        r0 = pl.multiple_of(i * HG_SUB, HG_SUB)
        qi = q_s[pl.ds(r0, HG_SUB), :]
        bi = b_s[pl.ds(r0, HG_SUB), :]
        acc = jnp.zeros((HG_SUB, HG_DV), F32)
        for s in range(HG_SUB):
            ks = k_s[pl.ds(r0 + s, 1), :]
            bs = b_s[pl.ds(r0 + s, 1), :]
            vs = v_s[pl.ds(r0 + s, 1), :]
            e = jnp.where(sub >= s, jnp.exp(jnp.minimum(bi - bs, 0.0)), 0.0)
            w = jnp.sum(qi * ks * e, axis=-1, keepdims=True)
            acc = acc + w * vs
        acc_s[pl.ds(r0, HG_SUB), :] = acc
        return carry

    lax.fori_loop(0, chunk // HG_SUB, diag, 0)
    o = o + acc_s[...]

    b_last = b_s[chunk - 1:chunk, :]
    kd = k * jnp.exp(b_last - b)
    st_new = st * jnp.exp(b_last) + _dot(v.T.astype(BF16), kd.astype(BF16))
    st_ref[...] = st_new

    ms = jnp.mean(o * o, axis=-1, keepdims=True)
    o_ref[...] = (o * lax.rsqrt(ms + EPS) * gn_ref[...] * _silu(og_ref[...])).astype(o_ref.dtype)

    @pl.when(c == pl.num_programs(1) - 1)
    def _():
        s_out_ref[0] = st_new.T


def _hgrn_seq(proj, hg_lb, gn, chunk=HG_CHUNK):
    t = proj.shape[0]
    blk = lambda off: pl.BlockSpec((chunk, 128), lambda h, c, off=off: (c, off + h))
    return pl.pallas_call(
        functools.partial(_hgrn_seq_kernel, chunk=chunk),
        grid=(H_HG, t // chunk),
        in_specs=[blk(COL_HQ), blk(COL_HF), blk(COL_HI), blk(COL_OG),
                  pl.BlockSpec((hg_lb.shape[0], 128), lambda h, c: (0, h)),
                  pl.BlockSpec((1, HG_DV), lambda h, c: (0, 0))],
        out_specs=[pl.BlockSpec((chunk, 128), lambda h, c: (c, h)),
                   pl.BlockSpec((1, HG_DK, HG_DV), lambda h, c: (h, 0, 0))],
        out_shape=[jax.ShapeDtypeStruct((t, HG_VW), BF16),
                   jax.ShapeDtypeStruct((H_HG, HG_DK, HG_DV), F32)],
        scratch_shapes=[pltpu.VMEM((HG_DV, HG_DK), F32)] + [pltpu.VMEM((chunk, 128), F32)] * 5,
        compiler_params=_cparams("parallel", "arbitrary"),
        name="hgrn_seq",
    )(proj, proj, proj, proj, hg_lb, gn.reshape(1, HG_DV))


def _hgrn_step_kernel(hqc_ref, hfc_ref, hi_ref, og_ref, lbc_ref, gn_ref, s_ref, o_ref, s_out_ref):
    lbraw = lbc_ref[...]
    e = jnp.exp(lbraw - jnp.max(lbraw, axis=0, keepdims=True))
    lb = e[0] / jnp.sum(e, axis=0)
    q = _silu(hqc_ref[...])
    f = lb + (1.0 - lb) * jax.nn.sigmoid(hfc_ref[...])
    k = 1.0 - f
    v = hi_ref[...]
    s_new = f * s_ref[...] + k * v
    s_out_ref[...] = s_new
    o = jnp.sum(s_new * q, axis=1, keepdims=True)
    ms = jnp.mean(o * o, axis=-1, keepdims=True)
    o_ref[...] = (o * lax.rsqrt(ms + EPS) * gn_ref[...] * _silu(og_ref[...])).astype(o_ref.dtype)


def _hgrn_step(proj, state, hg_lb, gn):
    nb = proj.shape[0]
    col = lambda a: a.reshape(nb, H_HG, HG_DK, 1)
    rowv = lambda a: a.reshape(nb, H_HG, 1, HG_DV)
    hq = col(proj[:, 0:HG_W])
    hf = col(proj[:, HG_W:2 * HG_W])
    hi = rowv(proj[:, 2 * HG_W:2 * HG_W + HG_VW])
    og = rowv(proj[:, 2 * HG_W + HG_VW:2 * HG_W + 2 * HG_VW])
    nl = hg_lb.shape[0]
    cspec = pl.BlockSpec((None, H_HG, HG_DK, 1), lambda b: (b, 0, 0, 0))
    rspec = pl.BlockSpec((None, H_HG, 1, HG_DV), lambda b: (b, 0, 0, 0))
    sspec = pl.BlockSpec((None, H_HG, HG_DK, HG_DV), lambda b: (b, 0, 0, 0))
    return pl.pallas_call(
        _hgrn_step_kernel,
        grid=(nb,),
        in_specs=[cspec, cspec, rspec, rspec,
                  pl.BlockSpec((nl, H_HG, HG_DK, 1), lambda b: (0, 0, 0, 0)),
                  pl.BlockSpec((1, 1, HG_DV), lambda b: (0, 0, 0)),
                  sspec],
        out_specs=[rspec, sspec],
        out_shape=[jax.ShapeDtypeStruct((nb, H_HG, 1, HG_DV), BF16),
                   jax.ShapeDtypeStruct((nb, H_HG, HG_DK, HG_DV), F32)],
        compiler_params=_cparams("parallel"),
        name="hgrn_step",
    )(hq, hf, hi, og, hg_lb.reshape(nl, H_HG, HG_DK, 1), gn.reshape(1, 1, HG_DV), state)


def _bias_from_rel(rel, tab_ref, h):
    out = jnp.full(rel.shape, tab_ref[0, h], F32)
    for bkt in range(1, MAX_EXACT + 1):
        out = jnp.where(rel >= bkt, tab_ref[bkt, h], out)
    for bkt, thr in zip(range(MAX_EXACT + 1, N_BUCKETS), BUCKET_THRESHOLDS):
        out = jnp.where(rel >= thr, tab_ref[bkt, h], out)
    return jnp.where(rel >= 0, out, NEG_INF)


def _bias_tiles_kernel(tab_ref, o_ref, *, base, row_step, tile_step):
    h = pl.program_id(0)
    t = pl.program_id(1)
    shape = o_ref.shape
    r = lax.broadcasted_iota(jnp.int32, shape, 0)
    c = lax.broadcasted_iota(jnp.int32, shape, 1)
    rel = base + t * tile_step + r * row_step - c
    o_ref[...] = _bias_from_rel(rel, tab_ref, h)


def _bias_tiles(rel_bias, n_tiles, rows, cols, base, row_step, tile_step):
    return pl.pallas_call(
        functools.partial(_bias_tiles_kernel, base=base, row_step=row_step, tile_step=tile_step),
        grid=(H_ATT, n_tiles),
        in_specs=[pl.BlockSpec(memory_space=pltpu.SMEM)],
        out_specs=pl.BlockSpec((None, None, rows, cols), lambda h, t: (h, t, 0, 0)),
        out_shape=jax.ShapeDtypeStruct((H_ATT, n_tiles, rows, cols), F32),
        compiler_params=_cparams("parallel", "parallel"),
        name="rel_bias_tiles",
    )(rel_bias)


def _block_mean_kernel(k_ref, o_ref):
    o_ref[...] = jnp.mean(k_ref[...], axis=0, keepdims=True)


def _block_mean(proj):
    t = proj.shape[0]
    nb = t // MOBA_BLOCK
    out = pl.pallas_call(
        _block_mean_kernel,
        grid=(nb,),
        in_specs=[pl.BlockSpec((MOBA_BLOCK, ATT_W), lambda n: (n, COLK_AK))],
        out_specs=pl.BlockSpec((None, 1, ATT_W), lambda n: (n, 0, 0)),
        out_shape=jax.ShapeDtypeStruct((nb, 1, ATT_W), F32),
        compiler_params=_cparams("parallel"),
        name="moba_block_mean",
    )(proj)
    return out.reshape(nb, ATT_W)


def _top_blocks(s, n_valid, axis):
    n = float(s.shape[axis])
    idx = lax.broadcasted_iota(jnp.int32, s.shape, axis).astype(F32)
    s = jnp.where(idx < n_valid, s, NEG_INF)
    sel = jnp.zeros(s.shape, F32)
    for _ in range(MOBA_TOPK):
        m = jnp.max(s, axis=axis, keepdims=True)
        first = jnp.min(jnp.where(s == m, idx, n), axis=axis, keepdims=True)
        pick = jnp.logical_and(idx == first, m > NEG_INF)
        sel = jnp.where(pick, 1.0, sel)
        s = jnp.where(pick, NEG_INF, s)
    return sel


def _moba_seq_kernel(q_ref, k_ref, v_ref, km_ref, bias_ref, o_ref, kb_s, vb_s, m_s, l_s, acc_s, *, n_far):
    i = pl.program_id(1)
    tq = q_ref.shape[0]
    nb = km_ref.shape[0]

    @pl.when(i == 0)
    def _():
        kb_s[...] = k_ref[...].astype(BF16)
        vb_s[...] = v_ref[...].astype(BF16)

    qs = q_ref[...] * ATT_SCALE
    qb = qs.astype(BF16)
    scores = _dot_nt(qs, km_ref[...], precision=lax.Precision.HIGHEST)
    sel = _top_blocks(scores, i, axis=1)
    blk_id = lax.broadcasted_iota(jnp.int32, (tq, nb), 1)

    r0 = pl.multiple_of(i * MOBA_BLOCK, MOBA_BLOCK)
    s = _dot_nt(qb, kb_s[pl.ds(r0, MOBA_BLOCK), :]) + bias_ref[0]
    m = jnp.max(s, axis=-1, keepdims=True)
    p = jnp.exp(s - m)
    m_s[...] = m
    l_s[...] = jnp.sum(p, axis=-1, keepdims=True)
    acc_s[...] = _dot(p.astype(BF16), vb_s[pl.ds(r0, MOBA_BLOCK), :])

    def past(j, carry):
        c0 = pl.multiple_of(j * MOBA_BLOCK, MOBA_BLOCK)
        s = _dot_nt(qb, kb_s[pl.ds(c0, MOBA_BLOCK), :]) + bias_ref[jnp.minimum(i - j, n_far)]
        chosen = jnp.max(jnp.where(blk_id == j, sel, 0.0), axis=-1, keepdims=True) > 0.5
        s = jnp.where(chosen, s, NEG_INF)
        m_old = m_s[...]
        m_new = jnp.maximum(m_old, jnp.max(s, axis=-1, keepdims=True))
        alpha = jnp.exp(m_old - m_new)
        p = jnp.exp(s - m_new)
        l_s[...] = alpha * l_s[...] + jnp.sum(p, axis=-1, keepdims=True)
        acc_s[...] = alpha * acc_s[...] + _dot(p.astype(BF16), vb_s[pl.ds(c0, MOBA_BLOCK), :])
        m_s[...] = m_new
        return carry

    lax.fori_loop(0, i, past, 0)
    o_ref[...] = (acc_s[...] / l_s[...]).astype(o_ref.dtype)


def _moba_seq(proj, rel_bias):
    t = proj.shape[0]
    nq = t // MOBA_BLOCK
    kmean = _block_mean(proj)
    n_far = -(-(BUCKET_THRESHOLDS[-1] + MOBA_BLOCK) // MOBA_BLOCK)
    bias = _bias_tiles(rel_bias, n_far + 1, MOBA_BLOCK, MOBA_BLOCK, 0, 1, MOBA_BLOCK)
    return pl.pallas_call(
        functools.partial(_moba_seq_kernel, n_far=n_far),
        grid=(H_ATT, nq),
        in_specs=[pl.BlockSpec((MOBA_BLOCK, ATT_DH), lambda h, i: (i, COL_AQ + h)),
                  pl.BlockSpec((t, ATT_DH), lambda h, i: (0, COL_AK + h)),
                  pl.BlockSpec((t, ATT_DH), lambda h, i: (0, COL_AV + h)),
                  pl.BlockSpec((nq, ATT_DH), lambda h, i: (0, h)),
                  pl.BlockSpec((None, n_far + 1, MOBA_BLOCK, MOBA_BLOCK), lambda h, i: (h, 0, 0, 0))],
        out_specs=pl.BlockSpec((MOBA_BLOCK, ATT_DH), lambda h, i: (i, h)),
        out_shape=jax.ShapeDtypeStruct((t, ATT_W), BF16),
        scratch_shapes=[pltpu.VMEM((t, ATT_DH), BF16), pltpu.VMEM((t, ATT_DH), BF16),
                        pltpu.VMEM((MOBA_BLOCK, 1), F32), pltpu.VMEM((MOBA_BLOCK, 1), F32),
                        pltpu.VMEM((MOBA_BLOCK, ATT_DH), F32)],
        compiler_params=_cparams("parallel", "arbitrary"),
        name="moba_seq",
    )(proj, proj, proj, kmean, bias)


def _moba_select_kernel(pt_ref, k0_ref, k1_ref, q_ref, sel_ref, km_s, *, pages_per_block):
    n = pl.program_id(1)
    page_rows = k0_ref.shape[0]
    tot = jnp.sum(k0_ref[...], axis=0, keepdims=True) + jnp.sum(k1_ref[...], axis=0, keepdims=True)
    km_s[pl.ds(n, 1), :] = tot / (pages_per_block * page_rows)

    @pl.when(n == pl.num_programs(1) - 1)
    def _():
        prod = km_s[...] * (q_ref[...] * ATT_SCALE)
        s = jnp.concatenate(
            [jnp.sum(prod[:, h * ATT_DH:(h + 1) * ATT_DH], axis=-1, keepdims=True) for h in range(H_ATT)],
            axis=1)
        nb = s.shape[0]
        mask = _top_blocks(s, nb, axis=0)
        idx = lax.broadcasted_iota(jnp.int32, s.shape, 0).astype(F32)
        rows = []
        for _ in range(MOBA_TOPK):
            first = jnp.min(jnp.where(mask > 0.5, idx, float(nb)), axis=0, keepdims=True)
            rows.append(first)
            mask = jnp.where(idx == first, 0.0, mask)
        sel_ref[...] = jnp.concatenate(rows, axis=0).astype(jnp.int32)


def _moba_select(cache_k3, page_table, q_rows):
    nb_seq, n_pages = page_table.shape
    page_rows = cache_k3.shape[1]
    ppb = MOBA_BLOCK // page_rows
    assert ppb == 2 and n_pages % ppb == 0
    n_blocks = n_pages // ppb
    assert n_blocks >= MOBA_TOPK
    kspec = lambda half: pl.BlockSpec((None, page_rows, ATT_W),
                                      lambda b, n, pt, half=half: (pt[b * n_pages + ppb * n + half], 0, 0))
    return pl.pallas_call(
        functools.partial(_moba_select_kernel, pages_per_block=ppb),
        grid_spec=pltpu.PrefetchScalarGridSpec(
            num_scalar_prefetch=1,
            grid=(nb_seq, n_blocks),
            in_specs=[kspec(0), kspec(1),
                      pl.BlockSpec((None, 1, ATT_W), lambda b, n, pt: (b, 0, 0))],
            out_specs=pl.BlockSpec((None, MOBA_TOPK, H_ATT), lambda b, n, pt: (b, 0, 0)),
            scratch_shapes=[pltpu.VMEM((n_blocks, ATT_W), F32)]),
        out_shape=jax.ShapeDtypeStruct((nb_seq, MOBA_TOPK, H_ATT), jnp.int32),
        compiler_params=_cparams("parallel", "arbitrary"),
        name="moba_select",
    )(page_table.reshape(-1), cache_k3, cache_k3, q_rows.reshape(nb_seq, 1, ATT_W))


def _moba_step_kernel(pt_ref, sel_ref, q_ref, kn_ref, vn_ref, *refs):
    n_pg = MOBA_TOPK * 2
    k_refs = refs[0:n_pg]
    v_refs = refs[n_pg:2 * n_pg]
    b_refs = refs[2 * n_pg:2 * n_pg + MOBA_TOPK]
    bown_ref = refs[2 * n_pg + MOBA_TOPK]
    o_ref = refs[2 * n_pg + MOBA_TOPK + 1]
    page_rows = k_refs[0].shape[0]

    qs = q_ref[...] * ATT_SCALE
    qb = qs.astype(BF16)
    logits = []
    for r in range(MOBA_TOPK):
        bias = b_refs[r][...]
        for half in range(2):
            kk = k_refs[2 * r + half][...].astype(BF16)
            logits.append(_dot_nt(qb, kk) + bias[:, half * page_rows:(half + 1) * page_rows])
    l_own = jnp.sum(qs * kn_ref[...], axis=-1, keepdims=True) + bown_ref[:, 0:1]
    m = l_own
    for lg in logits:
        m = jnp.maximum(m, jnp.max(lg, axis=-1, keepdims=True))
    p_own = jnp.exp(l_own - m)
    den = p_own
    acc = p_own * vn_ref[...]
    for lg, v_ref in zip(logits, v_refs):
        p = jnp.exp(lg - m)
        den = den + jnp.sum(p, axis=-1, keepdims=True)
        acc = acc + _dot(p.astype(BF16), v_ref[...].astype(BF16))
    o_ref[...] = (acc / den).astype(o_ref.dtype)


def _moba_step(proj, cache_k3, cache_v3, page_table, rel_bias):
    nb_seq, n_pages = page_table.shape
    page_rows = cache_k3.shape[1]
    ppb = MOBA_BLOCK // page_rows
    n_blocks = n_pages // ppb
    n_past = n_pages * page_rows
    head_rows = lambda a: a.reshape(nb_seq, H_ATT, 1, ATT_DH)
    aq = proj[:, COL_AQ * 128:COL_AQ * 128 + ATT_W]
    ak = proj[:, COL_AK * 128:COL_AK * 128 + ATT_W]
    av = proj[:, COL_AV * 128:COL_AV * 128 + ATT_W]
    sel = _moba_select(cache_k3, page_table, aq)
    bias = _bias_tiles(rel_bias, n_blocks + 1, 1, MOBA_BLOCK, n_past, 0, -MOBA_BLOCK)

    def page_spec(r, half):
        def imap(b, h, pt, sl):
            blk = sl[(b * MOBA_TOPK + r) * H_ATT + h]
            return (pt[b * n_pages + ppb * blk + half], 0, h)
        return pl.BlockSpec((None, page_rows, ATT_DH), imap)

    def bias_spec(r):
        return pl.BlockSpec((None, None, 1, MOBA_BLOCK),
                            lambda b, h, pt, sl: (h, sl[(b * MOBA_TOPK + r) * H_ATT + h], 0, 0))

    row_spec = pl.BlockSpec((None, None, 1, ATT_DH), lambda b, h, pt, sl: (b, h, 0, 0))
    pages = [page_spec(r, half) for r in range(MOBA_TOPK) for half in range(2)]
    out = pl.pallas_call(
        _moba_step_kernel,
        grid_spec=pltpu.PrefetchScalarGridSpec(
            num_scalar_prefetch=2,
            grid=(nb_seq, H_ATT),
            in_specs=[row_spec, row_spec, row_spec] + pages + pages
                     + [bias_spec(r) for r in range(MOBA_TOPK)]
                     + [pl.BlockSpec((None, None, 1, MOBA_BLOCK), lambda b, h, pt, sl: (h, n_blocks, 0, 0))],
            out_specs=row_spec),
        out_shape=jax.ShapeDtypeStruct((nb_seq, H_ATT, 1, ATT_DH), BF16),
        compiler_params=_cparams("parallel", "arbitrary"),
        name="moba_step",
    )(page_table.reshape(-1), sel.reshape(-1), head_rows(aq), head_rows(ak), head_rows(av),
      *([cache_k3] * (2 * MOBA_TOPK)), *([cache_v3] * (2 * MOBA_TOPK)), *([bias] * (MOBA_TOPK + 1)))
    return out.reshape(nb_seq, ATT_W)


def _merge_kernel(oh_ref, oa_ref, wh_ref, wa_ref, gh_ref, ga_ref, o_ref):
    br_h = _dot(oh_ref[...], wh_ref[...])
    br_a = _dot(oa_ref[...], wa_ref[...])
    o_ref[...] = (jax.nn.sigmoid(gh_ref[...]) * br_h + jax.nn.sigmoid(ga_ref[...]) * br_a).astype(o_ref.dtype)


def _merge(o_hg, o_att, w_bh, w_bm, proj, tm):
    m = o_hg.shape[0]
    tn = 1024
    return pl.pallas_call(
        _merge_kernel,
        grid=(m // tm, D_MODEL // tn),
        in_specs=[pl.BlockSpec((tm, HG_VW), lambda i, n: (i, 0)),
                  pl.BlockSpec((tm, ATT_W), lambda i, n: (i, 0)),
                  pl.BlockSpec((HG_VW, tn), lambda i, n: (0, n)),
                  pl.BlockSpec((ATT_W, tn), lambda i, n: (0, n)),
                  pl.BlockSpec((tm, tn), lambda i, n: (i, COLK_GHG + n)),
                  pl.BlockSpec((tm, tn), lambda i, n: (i, COLK_GATT + n))],
        out_specs=pl.BlockSpec((tm, tn), lambda i, n: (i, n)),
        out_shape=jax.ShapeDtypeStruct((m, D_MODEL), BF16),
        compiler_params=_cparams("parallel", "arbitrary"),
        name="gated_merge",
    )(o_hg, o_att, w_bh, w_bm, proj, proj)


def _out_proj_kernel(m_ref, w_ref, x_ref, g_ref, o_ref):
    z = _dot(m_ref[...], w_ref[...])
    ms = jnp.mean(z * z, axis=-1, keepdims=True)
    o_ref[...] = x_ref[...] + z * lax.rsqrt(ms + EPS) * g_ref[...]


def _out_proj(merged, w_out, x, gain, tm):
    m = x.shape[0]
    return pl.pallas_call(
        _out_proj_kernel,
        grid=(m // tm,),
        in_specs=[pl.BlockSpec((tm, D_MODEL), lambda i: (i, 0)),
                  pl.BlockSpec((D_MODEL, D_MODEL), lambda i: (0, 0)),
                  pl.BlockSpec((tm, D_MODEL), lambda i: (i, 0)),
                  pl.BlockSpec((1, D_MODEL), lambda i: (0, 0))],
        out_specs=pl.BlockSpec((tm, D_MODEL), lambda i: (i, 0)),
        out_shape=jax.ShapeDtypeStruct((m, D_MODEL), F32),
        compiler_params=_cparams("parallel"),
        name="out_proj_residual",
    )(merged, w_out, x, gain.reshape(1, D_MODEL))


def _ffn_kernel(x_ref, gpre_ref, wg_ref, wu_ref, cw_ref, cb_ref, p0_ref, p1_ref, wd_ref, gpost_ref,
                o_ref, g_out_ref, h_s, acc_s, carry_s, *, seq, tf):
    i = pl.program_id(0)
    j = pl.program_id(1)
    tm = x_ref.shape[0]

    @pl.when(j == 0)
    def _():
        x = x_ref[...]
        ms = jnp.mean(x * x, axis=-1, keepdims=True)
        h_s[...] = (x * lax.rsqrt(ms + EPS) * gpre_ref[...]).astype(BF16)
        acc_s[...] = jnp.zeros_like(acc_s)

    h = h_s[...]
    g = _dot(h, wg_ref[...])
    u = _dot(h, wu_ref[...])
    if seq:
        @pl.when(i == 0)
        def _():
            carry_s[j, 0:1, :] = p0_ref[...]
            carry_s[j, 1:2, :] = p1_ref[...]

        row = lax.broadcasted_iota(jnp.int32, (tm, tf), 0)
        c0 = carry_s[j, 0:1, :]
        c1 = carry_s[j, 1:2, :]
        prev1 = jnp.where(row == 0, c1, pltpu.roll(g, 1, axis=0))
        prev2 = jnp.where(row == 0, c0, jnp.where(row == 1, c1, pltpu.roll(g, 2, axis=0)))
        carry_s[j, 0:2, :] = g[tm - 2:tm, :]
        g_out_ref[...] = g[tm - 2:tm, :]
    else:
        prev2 = p0_ref[...]
        prev1 = p1_ref[...]
        g_out_ref[...] = g
    cw = cw_ref[...]
    gc = cb_ref[...] + cw[0:1] * prev2 + cw[1:2] * prev1 + cw[2:3] * g
    a = jax.nn.gelu(gc, approximate=True) * u
    acc_s[...] += _dot(a.astype(BF16), wd_ref[...])

    @pl.when(j == pl.num_programs(1) - 1)
    def _():
        y = acc_s[...]
        ms = jnp.mean(y * y, axis=-1, keepdims=True)
        o_ref[...] = x_ref[...] + y * lax.rsqrt(ms + EPS) * gpost_ref[...]


def _ffn(x, gpre, w_up, conv_w, conv_b, prev0, prev1, w_down, gpost, *, seq, tm, tf=512):
    m = x.shape[0]
    nf = D_FF // tf
    assert CONV_W == 3 and nf * tf == D_FF
    prow = 1 if seq else tm
    pspec = pl.BlockSpec((prow, tf), (lambda i, j: (0, j)) if seq else (lambda i, j: (i, j)))
    if seq:
        gspec = pl.BlockSpec((None, CONV_W - 1, tf), lambda i, j: (i, 0, j))
        gshape = (m // tm, CONV_W - 1, D_FF)
    else:
        gspec = pl.BlockSpec((tm, tf), lambda i, j: (i, j))
        gshape = (m, D_FF)
    vec = lambda n: pl.BlockSpec((1, n), lambda i, j: (0, 0))
    y, g_out = pl.pallas_call(
        functools.partial(_ffn_kernel, seq=seq, tf=tf),
        grid=(m // tm, nf),
        in_specs=[pl.BlockSpec((tm, D_MODEL), lambda i, j: (i, 0)),
                  vec(D_MODEL),
                  pl.BlockSpec((D_MODEL, tf), lambda i, j: (0, j)),
                  pl.BlockSpec((D_MODEL, tf), lambda i, j: (0, nf + j)),
                  pl.BlockSpec((CONV_W, tf), lambda i, j: (0, j)),
                  pl.BlockSpec((1, tf), lambda i, j: (0, j)),
                  pspec, pspec,
                  pl.BlockSpec((tf, D_MODEL), lambda i, j: (j, 0)),
                  vec(D_MODEL)],
        out_specs=[pl.BlockSpec((tm, D_MODEL), lambda i, j: (i, 0)), gspec],
        out_shape=[jax.ShapeDtypeStruct((m, D_MODEL), F32),
                   jax.ShapeDtypeStruct(gshape, F32)],
        scratch_shapes=[pltpu.VMEM((tm, D_MODEL), BF16), pltpu.VMEM((tm, D_MODEL), F32),
                        pltpu.VMEM((nf, 8, tf), F32)],
        compiler_params=_cparams("arbitrary", "arbitrary"),
        name="conv_ffn",
    )(x, gpre.reshape(1, D_MODEL), w_up, w_up, conv_w, conv_b.reshape(1, D_FF), prev0, prev1, w_down,
      gpost.reshape(1, D_MODEL))
    return y, (g_out[-1] if seq else g_out)


def kernel(x_prompt, x_sample, cache_k, cache_v, state_hgrn, state_ffn_conv, page_table, rel_bias, hg_lb,
           norm_mix_pre, norm_mix_post, norm_ffn_pre, norm_ffn_post, w_in, hg_out_norm, w_branch_hgrn,
           w_branch_moba, w_out, w_ffn_up, ffn_conv_w, ffn_conv_b, w_ffn_down):
    nb, t, _ = x_prompt.shape
    db = x_sample.shape[0]
    depth = w_in.shape[0]
    assert nb == 1 and depth == 1 and x_sample.shape[1] == 1
    l = 0
    w_in_b = w_in[l].astype(BF16)
    w_bh = w_branch_hgrn[l].astype(BF16)
    w_bm = w_branch_moba[l].astype(BF16)
    w_o = w_out[l].astype(BF16)
    w_up = w_ffn_up[l].astype(BF16)
    w_dn = w_ffn_down[l].astype(BF16)

    xp = x_prompt.reshape(t, D_MODEL)
    proj = _norm_matmul(xp, norm_mix_pre[l], w_in_b, tm=1024, tn=512)
    o_hg, s_new = _hgrn_seq(proj, hg_lb, hg_out_norm[l])
    o_att = _moba_seq(proj, rel_bias)
    merged = _merge(o_hg, o_att, w_bh, w_bm, proj, tm=512)
    x1 = _out_proj(merged, w_o, xp, norm_mix_post[l], tm=512)
    zero_row = jnp.zeros((1, D_FF), F32)
    yp, conv_p = _ffn(x1, norm_ffn_pre[l], w_up, ffn_conv_w[l], ffn_conv_b[l], zero_row, zero_row, w_dn,
                      norm_ffn_post[l], seq=True, tm=512)
    kp = proj[:, COL_AK * 128:COL_AK * 128 + ATT_W].reshape(1, 1, t, H_ATT, ATT_DH)
    vp = proj[:, COL_AV * 128:COL_AV * 128 + ATT_W].reshape(1, 1, t, H_ATT, ATT_DH)

    xs = x_sample.reshape(db, D_MODEL)
    n_pool, page_rows = cache_k.shape[1], cache_k.shape[2]
    ck = cache_k[l].reshape(n_pool, page_rows, ATT_W)
    cv = cache_v[l].reshape(n_pool, page_rows, ATT_W)
    projs = _norm_matmul(xs, norm_mix_pre[l], w_in_b, tm=db, tn=1024)
    o_hg_s, s_new_s = _hgrn_step(projs, state_hgrn[l], hg_lb, hg_out_norm[l])
    o_att_s = _moba_step(projs, ck, cv, page_table, rel_bias)
    merged_s = _merge(o_hg_s.reshape(db, HG_VW), o_att_s, w_bh, w_bm, projs, tm=db)
    x1s = _out_proj(merged_s, w_o, xs, norm_mix_post[l], tm=db)
    buf = state_ffn_conv[l]
    ys, g_s = _ffn(x1s, norm_ffn_pre[l], w_up, ffn_conv_w[l], ffn_conv_b[l], buf[:, 0], buf[:, 1], w_dn,
                   norm_ffn_post[l], seq=False, tm=db)
    ks = projs[:, COL_AK * 128:COL_AK * 128 + ATT_W].reshape(1, db, 1, H_ATT, ATT_DH)
    vs = projs[:, COL_AV * 128:COL_AV * 128 + ATT_W].reshape(1, db, 1, H_ATT, ATT_DH)
    conv_s = jnp.stack([buf[:, 1], g_s], axis=1)

    return (yp.reshape(1, t, D_MODEL), ys.reshape(db, 1, D_MODEL), kp, vp,
            s_new.reshape(1, 1, H_HG, HG_DK, HG_DV), conv_p.reshape(1, 1, CONV_W - 1, D_FF),
            ks, vs, s_new_s.reshape(1, db, H_HG, HG_DK, HG_DV), conv_s.reshape(1, db, CONV_W - 1, D_FF))
```

```python
import functools
import math

import numpy as np
import jax
import jax.numpy as jnp
from jax import lax
from jax.experimental import pallas as pl
from jax.experimental.pallas import tpu as pltpu

F32 = jnp.float32
BF16 = jnp.bfloat16

D_MODEL = 2048
H_HG = 8
HG_DK = 128
HG_DV = 128
HG_W = H_HG * HG_DK
HG_VW = H_HG * HG_DV
H_ATT = 8
ATT_DH = 128
ATT_W = H_ATT * ATT_DH
MOBA_BLOCK = 256
MOBA_TOPK = 3
N_BUCKETS = 32
MAX_EXACT = N_BUCKETS // 2
REL_MAX_DIST = 1024
D_FF = 5632
CONV_W = 3
EPS = 1e-6
N_IN = 2 * HG_W + 2 * HG_VW + 3 * ATT_W + 2 * D_MODEL
ATT_SCALE = ATT_DH ** -0.5

COL_HQ, COL_HF, COL_HI, COL_OG = 0, 8, 16, 24
COL_AQ, COL_AK, COL_AV = 32, 40, 48
COLK_AK, COLK_GHG, COLK_GATT = 5, 7, 9

HG_CHUNK = 128
HG_SUB = 8
MOBA_GROUP = 4
VMEM_LIMIT = 56 * 1024 * 1024
NEG_INF = float("-inf")


def _bucket_thresholds():
    n = np.arange(MAX_EXACT, 4 * REL_MAX_DIST, dtype=np.float64)
    large = MAX_EXACT + (np.log(n / MAX_EXACT) / math.log(REL_MAX_DIST / MAX_EXACT)
                         * (N_BUCKETS - MAX_EXACT)).astype(np.int64)
    large = np.minimum(large, N_BUCKETS - 1)
    return [int(n[np.argmax(large >= b)]) for b in range(MAX_EXACT + 1, N_BUCKETS)]


BUCKET_THRESHOLDS = _bucket_thresholds()


def _cparams(*sem):
    return pltpu.CompilerParams(dimension_semantics=sem, vmem_limit_bytes=VMEM_LIMIT)


def _silu(x):
    return x * jax.nn.sigmoid(x)


def _dot(a, b):
    return jnp.dot(a, b, preferred_element_type=F32)


def _dot_nt(a, b, precision=None):
    return lax.dot_general(a, b, (((1,), (1,)), ((), ())), precision=precision,
                           preferred_element_type=F32)


def _norm_matmul_kernel(x_ref, g_ref, w_ref, o_ref, h_ref):
    @pl.when(pl.program_id(1) == 0)
    def _():
        x = x_ref[...]
        ms = jnp.mean(x * x, axis=-1, keepdims=True)
        h_ref[...] = (x * lax.rsqrt(ms + EPS) * g_ref[...]).astype(BF16)

    o_ref[...] = _dot(h_ref[...], w_ref[...])


def _norm_matmul(x, gain, w_bf16, tm, tn):
    m, k = x.shape
    n = w_bf16.shape[1]
    return pl.pallas_call(
        _norm_matmul_kernel,
        grid=(m // tm, n // tn),
        in_specs=[pl.BlockSpec((tm, k), lambda i, j: (i, 0)),
                  pl.BlockSpec((1, k), lambda i, j: (0, 0)),
                  pl.BlockSpec((k, tn), lambda i, j: (0, j))],
        out_specs=pl.BlockSpec((tm, tn), lambda i, j: (i, j)),
        out_shape=jax.ShapeDtypeStruct((m, n), F32),
        scratch_shapes=[pltpu.VMEM((tm, k), BF16)],
        compiler_params=_cparams("parallel", "arbitrary"),
        name="norm_in_proj",
    )(x, gain.reshape(1, k), w_bf16)


def _forget_lower_bound(lbraw):
    e = jnp.exp(lbraw - jnp.max(lbraw, axis=0, keepdims=True))
    return e[0:1] / jnp.sum(e, axis=0, keepdims=True)


def _hgrn_seq_kernel(hq_ref, hf_ref, hi_ref, og_ref, lbraw_ref, gn_ref, o_ref, s_out_ref,
                     st_ref, q_s, k_s, b_s, v_s, acc_s, *, chunk):
    c = pl.program_id(1)

    @pl.when(c == 0)
    def _():
        st_ref[...] = jnp.zeros_like(st_ref)

    lb = _forget_lower_bound(lbraw_ref[...])
    q = _silu(hq_ref[...])
    f = lb + (1.0 - lb) * jax.nn.sigmoid(hf_ref[...])
    k = 1.0 - f
    v = hi_ref[...]

    b = jnp.log(f)
    row = lax.broadcasted_iota(jnp.int32, (chunk, HG_DK), 0)
    shift = 1
    while shift < chunk:
        b = b + jnp.where(row >= shift, pltpu.roll(b, shift, axis=0), 0.0)
        shift *= 2

    q_s[...] = q
    k_s[...] = k
    b_s[...] = b
    v_s[...] = v

    st = st_ref[...]
    o = _dot_nt((q * jnp.exp(b)).astype(BF16), st.astype(BF16))

    rr = lax.broadcasted_iota(jnp.int32, (chunk, chunk), 0)
    cc = lax.broadcasted_iota(jnp.int32, (chunk, chunk), 1)
    a = jnp.zeros((chunk, chunk), F32)
    hs = HG_SUB
    while hs < chunk:
        blk = 2 * hs
        ref_rows = jnp.concatenate(
            [jnp.broadcast_to(b_s[m0 + hs - 1:m0 + hs, :], (blk, HG_DK)) for m0 in range(0, chunk, blk)], axis=0)
        second = (row & (blk - 1)) >= hs
        qd = jnp.where(second, q * jnp.exp(jnp.minimum(b - ref_rows, 0.0)), 0.0)
        kd = jnp.where(second, 0.0, k * jnp.exp(jnp.minimum(ref_rows - b, 0.0)))
        a_l = _dot_nt(qd.astype(BF16), kd.astype(BF16))
        sh = int(math.log2(blk))
        a = a + jnp.where((rr >> sh) == (cc >> sh), a_l, 0.0)
        hs = blk
    o = o + _dot(a.astype(BF16), v.astype(BF16))

    sub = lax.broadcasted_iota(jnp.int32, (HG_SUB, HG_DK), 0)

    for r0 in range(0, chunk, HG_SUB):
        qi = q_s[r0:r0 + HG_SUB, :]
        bi = b_s[r0:r0 + HG_SUB, :]
        acc = jnp.zeros((HG_SUB, HG_DV), F32)
        for s in range(HG_SUB):
            ks = k_s[r0 + s:r0 + s + 1, :]
            bs = b_s[r0 + s:r0 + s + 1, :]
            vs = v_s[r0 + s:r0 + s + 1, :]
            e = jnp.where(sub >= s, jnp.exp(jnp.minimum(bi - bs, 0.0)), 0.0)
            w = jnp.sum(qi * ks * e, axis=-1, keepdims=True)
            acc = acc + w * vs
        acc_s[r0:r0 + HG_SUB, :] = acc
    o = o + acc_s[...]

    b_last = b_s[chunk - 1:chunk, :]
    kd = k * jnp.exp(b_last - b)
    st_new = st * jnp.exp(b_last) + _dot(v.T.astype(BF16), kd.astype(BF16))
    st_ref[...] = st_new

    ms = jnp.mean(o * o, axis=-1, keepdims=True)
    o_ref[...] = (o * lax.rsqrt(ms + EPS) * gn_ref[...] * _silu(og_ref[...])).astype(o_ref.dtype)

    @pl.when(c == pl.num_programs(1) - 1)
    def _():
        s_out_ref[0] = st_new.T


def _hgrn_seq(proj, hg_lb, gn, chunk=HG_CHUNK):
    t = proj.shape[0]
    blk = lambda off: pl.BlockSpec((chunk, 128), lambda h, c, off=off: (c, off + h))
    return pl.pallas_call(
        functools.partial(_hgrn_seq_kernel, chunk=chunk),
        grid=(H_HG, t // chunk),
        in_specs=[blk(COL_HQ), blk(COL_HF), blk(COL_HI), blk(COL_OG),
                  pl.BlockSpec((hg_lb.shape[0], 128), lambda h, c: (0, h)),
                  pl.BlockSpec((1, HG_DV), lambda h, c: (0, 0))],
        out_specs=[pl.BlockSpec((chunk, 128), lambda h, c: (c, h)),
                   pl.BlockSpec((1, HG_DK, HG_DV), lambda h, c: (h, 0, 0))],
        out_shape=[jax.ShapeDtypeStruct((t, HG_VW), BF16),
                   jax.ShapeDtypeStruct((H_HG, HG_DK, HG_DV), F32)],
        scratch_shapes=[pltpu.VMEM((HG_DV, HG_DK), F32)] + [pltpu.VMEM((chunk, 128), F32)] * 5,
        compiler_params=_cparams("parallel", "arbitrary"),
        name="hgrn_seq",
    )(proj, proj, proj, proj, hg_lb, gn.reshape(1, HG_DV))


def _hgrn_step_kernel(hqc_ref, hfc_ref, hi_ref, og_ref, lbc_ref, gn_ref, s_ref, o_ref, s_out_ref):
    lbraw = lbc_ref[...]
    e = jnp.exp(lbraw - jnp.max(lbraw, axis=0, keepdims=True))
    lb = e[0] / jnp.sum(e, axis=0)
    q = _silu(hqc_ref[...])
    f = lb + (1.0 - lb) * jax.nn.sigmoid(hfc_ref[...])
    k = 1.0 - f
    v = hi_ref[...]
    s_new = f * s_ref[...] + k * v
    s_out_ref[...] = s_new
    o = jnp.sum(s_new * q, axis=1, keepdims=True)
    ms = jnp.mean(o * o, axis=-1, keepdims=True)
    o_ref[...] = (o * lax.rsqrt(ms + EPS) * gn_ref[...] * _silu(og_ref[...])).astype(o_ref.dtype)


def _hgrn_step(proj, state, hg_lb, gn):
    nb = proj.shape[0]
    col = lambda a: a.reshape(nb, H_HG, HG_DK, 1)
    rowv = lambda a: a.reshape(nb, H_HG, 1, HG_DV)
    hq = col(proj[:, 0:HG_W])
    hf = col(proj[:, HG_W:2 * HG_W])
    hi = rowv(proj[:, 2 * HG_W:2 * HG_W + HG_VW])
    og = rowv(proj[:, 2 * HG_W + HG_VW:2 * HG_W + 2 * HG_VW])
    nl = hg_lb.shape[0]
    cspec = pl.BlockSpec((None, H_HG, HG_DK, 1), lambda b: (b, 0, 0, 0))
    rspec = pl.BlockSpec((None, H_HG, 1, HG_DV), lambda b: (b, 0, 0, 0))
    sspec = pl.BlockSpec((None, H_HG, HG_DK, HG_DV), lambda b: (b, 0, 0, 0))
    return pl.pallas_call(
        _hgrn_step_kernel,
        grid=(nb,),
        in_specs=[cspec, cspec, rspec, rspec,
                  pl.BlockSpec((nl, H_HG, HG_DK, 1), lambda b: (0, 0, 0, 0)),
                  pl.BlockSpec((1, 1, HG_DV), lambda b: (0, 0, 0)),
                  sspec],
        out_specs=[rspec, sspec],
        out_shape=[jax.ShapeDtypeStruct((nb, H_HG, 1, HG_DV), BF16),
                   jax.ShapeDtypeStruct((nb, H_HG, HG_DK, HG_DV), F32)],
        compiler_params=_cparams("parallel"),
        name="hgrn_step",
    )(hq, hf, hi, og, hg_lb.reshape(nl, H_HG, HG_DK, 1), gn.reshape(1, 1, HG_DV), state)


def _bias_from_rel(rel, tab_ref, h):
    out = jnp.full(rel.shape, tab_ref[0, h], F32)
    for bkt in range(1, MAX_EXACT + 1):
        out = jnp.where(rel >= bkt, tab_ref[bkt, h], out)
    for bkt, thr in zip(range(MAX_EXACT + 1, N_BUCKETS), BUCKET_THRESHOLDS):
        out = jnp.where(rel >= thr, tab_ref[bkt, h], out)
    return jnp.where(rel >= 0, out, NEG_INF)


def _bias_tiles_kernel(tab_ref, o_ref, *, base, tile_step, row_step, col_step):
    h = pl.program_id(0)
    t = pl.program_id(1)
    shape = o_ref.shape
    r = lax.broadcasted_iota(jnp.int32, shape, 0)
    c = lax.broadcasted_iota(jnp.int32, shape, 1)
    rel = base + t * tile_step + r * row_step + c * col_step
    o_ref[...] = _bias_from_rel(rel, tab_ref, h)


def _bias_tiles(rel_bias, n_tiles, rows, cols, base, tile_step, row_step, col_step):
    return pl.pallas_call(
        functools.partial(_bias_tiles_kernel, base=base, tile_step=tile_step, row_step=row_step,
                          col_step=col_step),
        grid=(H_ATT, n_tiles),
        in_specs=[pl.BlockSpec(memory_space=pltpu.SMEM)],
        out_specs=pl.BlockSpec((None, None, rows, cols), lambda h, t: (h, t, 0, 0)),
        out_shape=jax.ShapeDtypeStruct((H_ATT, n_tiles, rows, cols), F32),
        compiler_params=_cparams("parallel", "parallel"),
        name="rel_bias_tiles",
    )(rel_bias)


def _block_mean_kernel(k_ref, o_ref):
    o_ref[...] = jnp.mean(k_ref[...], axis=0, keepdims=True)


def _block_mean(proj):
    t = proj.shape[0]
    nb = t // MOBA_BLOCK
    out = pl.pallas_call(
        _block_mean_kernel,
        grid=(nb,),
        in_specs=[pl.BlockSpec((MOBA_BLOCK, ATT_W), lambda n: (n, COLK_AK))],
        out_specs=pl.BlockSpec((None, 1, ATT_W), lambda n: (n, 0, 0)),
        out_shape=jax.ShapeDtypeStruct((nb, 1, ATT_W), F32),
        compiler_params=_cparams("parallel"),
        name="moba_block_mean",
    )(proj)
    return out.reshape(nb, ATT_W)


def _top_blocks(s, n_valid, axis):
    n = float(s.shape[axis])
    idx = lax.broadcasted_iota(jnp.int32, s.shape, axis).astype(F32)
    s = jnp.where(idx < n_valid, s, NEG_INF)
    sel = jnp.zeros(s.shape, F32)
    for _ in range(MOBA_TOPK):
        m = jnp.max(s, axis=axis, keepdims=True)
        first = jnp.min(jnp.where(s == m, idx, n), axis=axis, keepdims=True)
        pick = jnp.logical_and(idx == first, m > NEG_INF)
        sel = jnp.where(pick, 1.0, sel)
        s = jnp.where(pick, NEG_INF, s)
    return sel


def _moba_seq_kernel(q_ref, k_ref, v_ref, km_ref, bias_ref, o_ref, kb_s, vt_s, pen_s, acc_s, s_s, *, n_far):
    i = pl.program_id(1)
    nblk = vt_s.shape[0]

    @pl.when(i == 0)
    def _():
        kb_s[...] = k_ref[...].astype(BF16)

        def xpose(j, carry):
            r0 = pl.multiple_of(j * MOBA_BLOCK, MOBA_BLOCK)
            vt_s[j] = v_ref[pl.ds(r0, MOBA_BLOCK), :].T.astype(BF16)
            return carry

        lax.fori_loop(0, nblk, xpose, 0)

    tq = q_ref.shape[0]
    qt = (q_ref[...] * ATT_SCALE).T
    qtb = qt.astype(BF16)
    scores = jnp.dot(km_ref[...], qt, precision=lax.Precision.HIGHEST, preferred_element_type=F32)
    blk_id = lax.broadcasted_iota(jnp.int32, scores.shape, 0)
    keep = jnp.logical_or(_top_blocks(scores, i, axis=0) > 0.5, blk_id == i)
    pen_s[0:nblk, :] = jnp.where(keep, 0.0, NEG_INF)
    pen_s[nblk:nblk + 1, :] = jnp.full((1, tq), NEG_INF, F32)
    acc_s[...] = jnp.zeros_like(acc_s)

    def group(g, carry):
        m_old, l_old = carry
        blocks = []
        mg = m_old
        for u in range(MOBA_GROUP):
            pos = g * MOBA_GROUP + u
            j = jnp.maximum(i - pos, 0)
            pen_row = jnp.where(pos <= i, j, nblk)
            c0 = pl.multiple_of(j * MOBA_BLOCK, MOBA_BLOCK)
            s = (_dot(kb_s[pl.ds(c0, MOBA_BLOCK), :], qtb) + bias_ref[jnp.minimum(pos, n_far)]
                 + pen_s[pl.ds(pen_row, 1), :])
            s_s[u] = s
            mg = jnp.maximum(mg, jnp.max(s, axis=0, keepdims=True))
            blocks.append(j)
        alpha = jnp.exp(m_old - mg)
        l_new = alpha * l_old
        pv = None
        for u, j in enumerate(blocks):
            p = jnp.exp(s_s[u] - mg)
            l_new = l_new + jnp.sum(p, axis=0, keepdims=True)
            d = _dot(vt_s[j], p.astype(BF16))
            pv = d if pv is None else pv + d
        acc_s[...] = alpha * acc_s[...] + pv
        return mg, l_new

    init = (jnp.full((1, tq), NEG_INF, F32), jnp.zeros((1, tq), F32))
    m, l = lax.fori_loop(0, (i + MOBA_GROUP) // MOBA_GROUP, group, init)
    o_ref[...] = (acc_s[...] / l).T.astype(o_ref.dtype)


def _moba_seq(proj, rel_bias):
    t = proj.shape[0]
    nq = t // MOBA_BLOCK
    kmean = _block_mean(proj)
    n_far = -(-(BUCKET_THRESHOLDS[-1] + MOBA_BLOCK) // MOBA_BLOCK)
    bias = _bias_tiles(rel_bias, n_far + 1, MOBA_BLOCK, MOBA_BLOCK, 0, MOBA_BLOCK, -1, 1)
    return pl.pallas_call(
        functools.partial(_moba_seq_kernel, n_far=n_far),
        grid=(H_ATT, nq),
        in_specs=[pl.BlockSpec((MOBA_BLOCK, ATT_DH), lambda h, i: (i, COL_AQ + h)),
                  pl.BlockSpec((t, ATT_DH), lambda h, i: (0, COL_AK + h)),
                  pl.BlockSpec((t, ATT_DH), lambda h, i: (0, COL_AV + h)),
                  pl.BlockSpec((nq, ATT_DH), lambda h, i: (0, h)),
                  pl.BlockSpec((None, n_far + 1, MOBA_BLOCK, MOBA_BLOCK), lambda h, i: (h, 0, 0, 0))],
        out_specs=pl.BlockSpec((MOBA_BLOCK, ATT_DH), lambda h, i: (i, h)),
        out_shape=jax.ShapeDtypeStruct((t, ATT_W), BF16),
        scratch_shapes=[pltpu.VMEM((t, ATT_DH), BF16), pltpu.VMEM((nq, ATT_DH, MOBA_BLOCK), BF16),
                        pltpu.VMEM((nq + 8, MOBA_BLOCK), F32), pltpu.VMEM((ATT_DH, MOBA_BLOCK), F32),
                        pltpu.VMEM((MOBA_GROUP, MOBA_BLOCK, MOBA_BLOCK), F32)],
        compiler_params=_cparams("parallel", "arbitrary"),
        name="moba_seq",
    )(proj, proj, proj, kmean, bias)


SELECT_PAGES_PER_STEP = 8


def _moba_select_kernel(pt_ref, *refs, pages_per_block):
    n_pg = SELECT_PAGES_PER_STEP
    page_refs = refs[:n_pg]
    q_ref, sel_ref, km_s = refs[n_pg:]
    n = pl.program_id(1)
    page_rows = page_refs[0].shape[0]
    blocks_per_step = n_pg // pages_per_block
    for blk in range(blocks_per_step):
        tot = page_refs[blk * pages_per_block][...].sum(axis=0)
        for pg in range(1, pages_per_block):
            tot = tot + page_refs[blk * pages_per_block + pg][...].sum(axis=0)
        km_s[n * blocks_per_step + blk] = tot / (pages_per_block * page_rows)

    @pl.when(n == pl.num_programs(1) - 1)
    def _():
        prod = km_s[...] * (q_ref[...] * ATT_SCALE)[None]
        s = jnp.sum(prod, axis=-1, keepdims=True)
        nb = s.shape[0]
        mask = _top_blocks(s, nb, axis=0)
        idx = lax.broadcasted_iota(jnp.int32, s.shape, 0).astype(F32)
        for r in range(MOBA_TOPK):
            first = jnp.min(jnp.where(mask > 0.5, idx, float(nb)), axis=0)
            sel_ref[r] = first.astype(jnp.int32)
            mask = jnp.where(idx == first[None], 0.0, mask)


def _moba_select(cache_k4, page_table, q_heads):
    nb_seq, n_pages = page_table.shape
    page_rows = cache_k4.shape[1]
    ppb = MOBA_BLOCK // page_rows
    n_pg = SELECT_PAGES_PER_STEP
    assert ppb * page_rows == MOBA_BLOCK and n_pg % ppb == 0 and n_pages % n_pg == 0
    n_blocks = n_pages // ppb
    assert n_blocks >= MOBA_TOPK
    kspec = lambda p: pl.BlockSpec((None, page_rows, H_ATT, ATT_DH),
                                   lambda b, n, pt, p=p: (pt[b * n_pages + n_pg * n + p], 0, 0, 0))
    return pl.pallas_call(
        functools.partial(_moba_select_kernel, pages_per_block=ppb),
        grid_spec=pltpu.PrefetchScalarGridSpec(
            num_scalar_prefetch=1,
            grid=(nb_seq, n_pages // n_pg),
            in_specs=[kspec(p) for p in range(n_pg)]
                     + [pl.BlockSpec((None, H_ATT, ATT_DH), lambda b, n, pt: (b, 0, 0))],
            out_specs=pl.BlockSpec((None, MOBA_TOPK, H_ATT, 1), lambda b, n, pt: (b, 0, 0, 0)),
            scratch_shapes=[pltpu.VMEM((n_blocks, H_ATT, ATT_DH), F32)]),
        out_shape=jax.ShapeDtypeStruct((nb_seq, MOBA_TOPK, H_ATT, 1), jnp.int32),
        compiler_params=_cparams("parallel", "arbitrary"),
        name="moba_select",
    )(page_table.reshape(-1), *([cache_k4] * n_pg), q_heads)


def _moba_step_kernel(pt_ref, sel_ref, q_ref, kn_ref, vn_ref, b0_ref, b1_ref, b2_ref, bown_ref, ck_hbm, cv_hbm,
                      o_ref, kbuf, vbuf, sem, *, n_pages):
    n_pg = MOBA_TOPK * 2
    b_refs = (b0_ref, b1_ref, b2_ref)
    page_rows = kbuf.shape[2]
    b = pl.program_id(0)
    h = pl.program_id(1)
    n_heads = pl.num_programs(1)
    step = b * n_heads + h
    slot = step % 2

    def page_copies(bb, hh, sl):
        cps = []
        for r in range(MOBA_TOPK):
            blk = sel_ref[(bb * MOBA_TOPK + r) * n_heads + hh]
            for half in range(2):
                page = pt_ref[bb * n_pages + 2 * blk + half]
                idx = 2 * r + half
                cps.append(pltpu.make_async_copy(ck_hbm.at[page, :, hh, :], kbuf.at[sl, idx], sem.at[sl, idx]))
                cps.append(pltpu.make_async_copy(cv_hbm.at[page, :, hh, :], vbuf.at[sl, idx], sem.at[sl, n_pg + idx]))
        return cps

    @pl.when(step == 0)
    def _():
        for cp in page_copies(b, h, slot):
            cp.start()

    @pl.when(step + 1 < pl.num_programs(0) * n_heads)
    def _():
        nxt = step + 1
        for cp in page_copies(nxt // n_heads, nxt % n_heads, 1 - slot):
            cp.start()

    for cp in page_copies(b, h, slot):
        cp.wait()

    qs = q_ref[...] * ATT_SCALE
    qb = qs.astype(BF16)
    logits = []
    for r in range(MOBA_TOPK):
        bias = b_refs[r][...]
        for half in range(2):
            kk = kbuf[slot, 2 * r + half].astype(BF16)
            logits.append(_dot_nt(qb, kk) + bias[:, half * page_rows:(half + 1) * page_rows])
    l_own = jnp.sum(qs * kn_ref[...], axis=-1, keepdims=True) + bown_ref[:, 0:1]
    m = l_own
    for lg in logits:
        m = jnp.maximum(m, jnp.max(lg, axis=-1, keepdims=True))
    p_own = jnp.exp(l_own - m)
    den = p_own
    acc = p_own * vn_ref[...]
    for idx, lg in enumerate(logits):
        p = jnp.exp(lg - m)
        den = den + jnp.sum(p, axis=-1, keepdims=True)
        acc = acc + _dot(p.astype(BF16), vbuf[slot, idx].astype(BF16))
    o_ref[...] = (acc / den).astype(o_ref.dtype)


def _moba_step(proj, cache_k4, cache_v4, page_table, rel_bias):
    nb_seq, n_pages = page_table.shape
    page_rows = cache_k4.shape[1]
    ppb = MOBA_BLOCK // page_rows
    assert ppb == 2
    n_blocks = n_pages // ppb
    n_past = n_pages * page_rows
    head_rows = lambda a: a.reshape(nb_seq, H_ATT, 1, ATT_DH)
    aq = proj[:, COL_AQ * 128:COL_AQ * 128 + ATT_W]
    ak = proj[:, COL_AK * 128:COL_AK * 128 + ATT_W]
    av = proj[:, COL_AV * 128:COL_AV * 128 + ATT_W]
    sel = _moba_select(cache_k4, page_table, aq.reshape(nb_seq, H_ATT, ATT_DH))
    bias = _bias_tiles(rel_bias, n_blocks + 1, 1, MOBA_BLOCK, n_past, -MOBA_BLOCK, 0, -1)

    def bias_spec(r):
        return pl.BlockSpec((None, None, 1, MOBA_BLOCK),
                            lambda b, h, pt, sl: (h, sl[(b * MOBA_TOPK + r) * H_ATT + h], 0, 0))

    row_spec = pl.BlockSpec((None, None, 1, ATT_DH), lambda b, h, pt, sl: (b, h, 0, 0))
    hbm = pl.BlockSpec(memory_space=pl.ANY)
    n_pg = 2 * MOBA_TOPK
    out = pl.pallas_call(
        functools.partial(_moba_step_kernel, n_pages=n_pages),
        grid_spec=pltpu.PrefetchScalarGridSpec(
            num_scalar_prefetch=2,
            grid=(nb_seq, H_ATT),
            in_specs=[row_spec, row_spec, row_spec] + [bias_spec(r) for r in range(MOBA_TOPK)]
                     + [pl.BlockSpec((None, None, 1, MOBA_BLOCK), lambda b, h, pt, sl: (h, n_blocks, 0, 0)),
                        hbm, hbm],
            out_specs=row_spec,
            scratch_shapes=[pltpu.VMEM((2, n_pg, page_rows, ATT_DH), F32),
                            pltpu.VMEM((2, n_pg, page_rows, ATT_DH), F32),
                            pltpu.SemaphoreType.DMA((2, 2 * n_pg))]),
        out_shape=jax.ShapeDtypeStruct((nb_seq, H_ATT, 1, ATT_DH), BF16),
        compiler_params=_cparams("arbitrary", "arbitrary"),
        name="moba_step",
    )(page_table.reshape(-1), sel.reshape(-1), head_rows(aq), head_rows(ak), head_rows(av),
      bias, bias, bias, bias, cache_k4, cache_v4)
    return out.reshape(nb_seq, ATT_W)


def _merge_kernel(oh_ref, oa_ref, wh_ref, wa_ref, gh_ref, ga_ref, o_ref):
    br_h = _dot(oh_ref[...], wh_ref[...])
    br_a = _dot(oa_ref[...], wa_ref[...])
    o_ref[...] = (jax.nn.sigmoid(gh_ref[...]) * br_h + jax.nn.sigmoid(ga_ref[...]) * br_a).astype(o_ref.dtype)


def _merge(o_hg, o_att, w_bh, w_bm, proj, tm):
    m = o_hg.shape[0]
    tn = 1024
    return pl.pallas_call(
        _merge_kernel,
        grid=(m // tm, D_MODEL // tn),
        in_specs=[pl.BlockSpec((tm, HG_VW), lambda i, n: (i, 0)),
                  pl.BlockSpec((tm, ATT_W), lambda i, n: (i, 0)),
                  pl.BlockSpec((HG_VW, tn), lambda i, n: (0, n)),
                  pl.BlockSpec((ATT_W, tn), lambda i, n: (0, n)),
                  pl.BlockSpec((tm, tn), lambda i, n: (i, COLK_GHG + n)),
                  pl.BlockSpec((tm, tn), lambda i, n: (i, COLK_GATT + n))],
        out_specs=pl.BlockSpec((tm, tn), lambda i, n: (i, n)),
        out_shape=jax.ShapeDtypeStruct((m, D_MODEL), BF16),
        compiler_params=_cparams("parallel", "arbitrary"),
        name="gated_merge",
    )(o_hg, o_att, w_bh, w_bm, proj, proj)


def _out_proj_kernel(m_ref, w_ref, x_ref, g_ref, o_ref):
    z = _dot(m_ref[...], w_ref[...])
    ms = jnp.mean(z * z, axis=-1, keepdims=True)
    o_ref[...] = x_ref[...] + z * lax.rsqrt(ms + EPS) * g_ref[...]


def _out_proj(merged, w_out, x, gain, tm):
    m = x.shape[0]
    return pl.pallas_call(
        _out_proj_kernel,
        grid=(m // tm,),
        in_specs=[pl.BlockSpec((tm, D_MODEL), lambda i: (i, 0)),
                  pl.BlockSpec((D_MODEL, D_MODEL), lambda i: (0, 0)),
                  pl.BlockSpec((tm, D_MODEL), lambda i: (i, 0)),
                  pl.BlockSpec((1, D_MODEL), lambda i: (0, 0))],
        out_specs=pl.BlockSpec((tm, D_MODEL), lambda i: (i, 0)),
        out_shape=jax.ShapeDtypeStruct((m, D_MODEL), F32),
        compiler_params=_cparams("parallel"),
        name="out_proj_residual",
    )(merged, w_out, x, gain.reshape(1, D_MODEL))


def _ffn_kernel(x_ref, gpre_ref, wg_ref, wu_ref, cw_ref, cb_ref, p0_ref, p1_ref, wd_ref, gpost_ref,
                o_ref, g_out_ref, h_s, acc_s, carry_s, *, seq, tf):
    i = pl.program_id(0)
    j = pl.program_id(1)
    tm = x_ref.shape[0]

    @pl.when(j == 0)
    def _():
        x = x_ref[...]
        ms = jnp.mean(x * x, axis=-1, keepdims=True)
        h_s[...] = (x * lax.rsqrt(ms + EPS) * gpre_ref[...]).astype(BF16)
        acc_s[...] = jnp.zeros_like(acc_s)

    h = h_s[...]
    g = _dot(h, wg_ref[...])
    u = _dot(h, wu_ref[...])
    if seq:
        @pl.when(i == 0)
        def _():
            carry_s[j, 0:1, :] = p0_ref[...]
            carry_s[j, 1:2, :] = p1_ref[...]

        row = lax.broadcasted_iota(jnp.int32, (tm, tf), 0)
        c0 = carry_s[j, 0:1, :]
        c1 = carry_s[j, 1:2, :]
        prev1 = jnp.where(row == 0, c1, pltpu.roll(g, 1, axis=0))
        prev2 = jnp.where(row == 0, c0, jnp.where(row == 1, c1, pltpu.roll(g, 2, axis=0)))
        carry_s[j, 0:2, :] = g[tm - 2:tm, :]
        g_out_ref[...] = g[tm - 2:tm, :]
    else:
        prev2 = p0_ref[...]
        prev1 = p1_ref[...]
        g_out_ref[...] = g
    cw = cw_ref[...]
    gc = cb_ref[...] + cw[0:1] * prev2 + cw[1:2] * prev1 + cw[2:3] * g
    a = jax.nn.gelu(gc, approximate=True) * u
    acc_s[...] += _dot(a.astype(BF16), wd_ref[...])

    @pl.when(j == pl.num_programs(1) - 1)
    def _():
        y = acc_s[...]
        ms = jnp.mean(y * y, axis=-1, keepdims=True)
        o_ref[...] = x_ref[...] + y * lax.rsqrt(ms + EPS) * gpost_ref[...]


def _ffn(x, gpre, w_up, conv_w, conv_b, prev0, prev1, w_down, gpost, *, seq, tm, tf=512):
    m = x.shape[0]
    nf = D_FF // tf
    assert CONV_W == 3 and nf * tf == D_FF
    prow = 1 if seq else tm
    pspec = pl.BlockSpec((prow, tf), (lambda i, j: (0, j)) if seq else (lambda i, j: (i, j)))
    if seq:
        gspec = pl.BlockSpec((None, CONV_W - 1, tf), lambda i, j: (i, 0, j))
        gshape = (m // tm, CONV_W - 1, D_FF)
    else:
        gspec = pl.BlockSpec((tm, tf), lambda i, j: (i, j))
        gshape = (m, D_FF)
    vec = lambda n: pl.BlockSpec((1, n), lambda i, j: (0, 0))
    y, g_out = pl.pallas_call(
        functools.partial(_ffn_kernel, seq=seq, tf=tf),
        grid=(m // tm, nf),
        in_specs=[pl.BlockSpec((tm, D_MODEL), lambda i, j: (i, 0)),
                  vec(D_MODEL),
                  pl.BlockSpec((D_MODEL, tf), lambda i, j: (0, j)),
                  pl.BlockSpec((D_MODEL, tf), lambda i, j: (0, nf + j)),
                  pl.BlockSpec((CONV_W, tf), lambda i, j: (0, j)),
                  pl.BlockSpec((1, tf), lambda i, j: (0, j)),
                  pspec, pspec,
                  pl.BlockSpec((tf, D_MODEL), lambda i, j: (j, 0)),
                  vec(D_MODEL)],
        out_specs=[pl.BlockSpec((tm, D_MODEL), lambda i, j: (i, 0)), gspec],
        out_shape=[jax.ShapeDtypeStruct((m, D_MODEL), F32),
                   jax.ShapeDtypeStruct(gshape, F32)],
        scratch_shapes=[pltpu.VMEM((tm, D_MODEL), BF16), pltpu.VMEM((tm, D_MODEL), F32),
                        pltpu.VMEM((nf, 8, tf), F32)],
        compiler_params=_cparams("arbitrary", "arbitrary"),
        name="conv_ffn",
    )(x, gpre.reshape(1, D_MODEL), w_up, w_up, conv_w, conv_b.reshape(1, D_FF), prev0, prev1, w_down,
      gpost.reshape(1, D_MODEL))
    return y, (g_out[-1] if seq else g_out)


def kernel(x_prompt, x_sample, cache_k, cache_v, state_hgrn, state_ffn_conv, page_table, rel_bias, hg_lb,
           norm_mix_pre, norm_mix_post, norm_ffn_pre, norm_ffn_post, w_in, hg_out_norm, w_branch_hgrn,
           w_branch_moba, w_out, w_ffn_up, ffn_conv_w, ffn_conv_b, w_ffn_down):
    nb, t, _ = x_prompt.shape
    db = x_sample.shape[0]
    depth = w_in.shape[0]
    assert nb == 1 and depth == 1 and x_sample.shape[1] == 1
    l = 0
    w_in_b = w_in[l].astype(BF16)
    w_bh = w_branch_hgrn[l].astype(BF16)
    w_bm = w_branch_moba[l].astype(BF16)
    w_o = w_out[l].astype(BF16)
    w_up = w_ffn_up[l].astype(BF16)
    w_dn = w_ffn_down[l].astype(BF16)

    xp = x_prompt.reshape(t, D_MODEL)
    proj = _norm_matmul(xp, norm_mix_pre[l], w_in_b, tm=1024, tn=512)
    o_hg, s_new = _hgrn_seq(proj, hg_lb, hg_out_norm[l])
    o_att = _moba_seq(proj, rel_bias)
    merged = _merge(o_hg, o_att, w_bh, w_bm, proj, tm=512)
    x1 = _out_proj(merged, w_o, xp, norm_mix_post[l], tm=512)
    zero_row = jnp.zeros((1, D_FF), F32)
    yp, conv_p = _ffn(x1, norm_ffn_pre[l], w_up, ffn_conv_w[l], ffn_conv_b[l], zero_row, zero_row, w_dn,
                      norm_ffn_post[l], seq=True, tm=512)
    kp = proj[:, COL_AK * 128:COL_AK * 128 + ATT_W].reshape(1, 1, t, H_ATT, ATT_DH)
    vp = proj[:, COL_AV * 128:COL_AV * 128 + ATT_W].reshape(1, 1, t, H_ATT, ATT_DH)

    xs = x_sample.reshape(db, D_MODEL)
    n_pool, page_rows = cache_k.shape[1], cache_k.shape[2]
    ck = cache_k.reshape(depth * n_pool, page_rows, H_ATT, ATT_DH)
    cv = cache_v.reshape(depth * n_pool, page_rows, H_ATT, ATT_DH)
    projs = _norm_matmul(xs, norm_mix_pre[l], w_in_b, tm=db, tn=1024)
    o_hg_s, s_new_s = _hgrn_step(projs, state_hgrn[l], hg_lb, hg_out_norm[l])
    o_att_s = _moba_step(projs, ck, cv, page_table, rel_bias)
    merged_s = _merge(o_hg_s.reshape(db, HG_VW), o_att_s, w_bh, w_bm, projs, tm=db)
    x1s = _out_proj(merged_s, w_o, xs, norm_mix_post[l], tm=db)
    buf = state_ffn_conv[l]
    ys, g_s = _ffn(x1s, norm_ffn_pre[l], w_up, ffn_conv_w[l], ffn_conv_b[l], buf[:, 0], buf[:, 1], w_dn,
                   norm_ffn_post[l], seq=False, tm=db)
    ks = projs[:, COL_AK * 128:COL_AK * 128 + ATT_W].reshape(1, db, 1, H_ATT, ATT_DH)
    vs = projs[:, COL_AV * 128:COL_AV * 128 + ATT_W].reshape(1, db, 1, H_ATT, ATT_DH)
    conv_s = jnp.stack([buf[:, 1], g_s], axis=1)

    return (yp.reshape(1, t, D_MODEL), ys.reshape(db, 1, D_MODEL), kp, vp,
            s_new.reshape(1, 1, H_HG, HG_DK, HG_DV), conv_p.reshape(1, 1, CONV_W - 1, D_FF),
            ks, vs, s_new_s.reshape(1, db, H_HG, HG_DK, HG_DV), conv_s.reshape(1, db, CONV_W - 1, D_FF))
```

```python
import functools
import math

import numpy as np
import jax
import jax.numpy as jnp
from jax import lax
from jax.experimental import pallas as pl
from jax.experimental.pallas import tpu as pltpu

F32 = jnp.float32
BF16 = jnp.bfloat16

D_MODEL = 2048
H_HG = 8
HG_DK = 128
HG_DV = 128
HG_W = H_HG * HG_DK
HG_VW = H_HG * HG_DV
H_ATT = 8
ATT_DH = 128
ATT_W = H_ATT * ATT_DH
MOBA_BLOCK = 256
MOBA_TOPK = 3
N_BUCKETS = 32
MAX_EXACT = N_BUCKETS // 2
REL_MAX_DIST = 1024
D_FF = 5632
CONV_W = 3
EPS = 1e-6
N_IN = 2 * HG_W + 2 * HG_VW + 3 * ATT_W + 2 * D_MODEL
ATT_SCALE = ATT_DH ** -0.5
LOG2E = math.log2(math.e)

COL_HQ, COL_HF, COL_HI, COL_OG = 0, 8, 16, 24
COL_AQ, COL_AK, COL_AV = 32, 40, 48
COLK_AK, COLK_GHG, COLK_GATT = 5, 7, 9

HG_CHUNK = 128
HG_SUB = 8
HG_HEADS_PER_STEP = 2
MOBA_GROUP = 8
VMEM_LIMIT = 56 * 1024 * 1024
NEG_INF = float("-inf")


def _bucket_thresholds():
    n = np.arange(MAX_EXACT, 4 * REL_MAX_DIST, dtype=np.float64)
    large = MAX_EXACT + (np.log(n / MAX_EXACT) / math.log(REL_MAX_DIST / MAX_EXACT)
                         * (N_BUCKETS - MAX_EXACT)).astype(np.int64)
    large = np.minimum(large, N_BUCKETS - 1)
    return [int(n[np.argmax(large >= b)]) for b in range(MAX_EXACT + 1, N_BUCKETS)]


BUCKET_THRESHOLDS = _bucket_thresholds()


def _cparams(*sem):
    return pltpu.CompilerParams(dimension_semantics=sem, vmem_limit_bytes=VMEM_LIMIT)


def _silu(x):
    return x * jax.nn.sigmoid(x)


def _dot(a, b):
    return jnp.dot(a, b, preferred_element_type=F32)


def _dot_nt(a, b, precision=None):
    return lax.dot_general(a, b, (((1,), (1,)), ((), ())), precision=precision,
                           preferred_element_type=F32)


def _norm_matmul_kernel(x_ref, g_ref, w_ref, o_ref, h_ref):
    @pl.when(pl.program_id(1) == 0)
    def _():
        x = x_ref[...]
        ms = jnp.mean(x * x, axis=-1, keepdims=True)
        h_ref[...] = (x * lax.rsqrt(ms + EPS) * g_ref[...]).astype(BF16)

    o_ref[...] = _dot(h_ref[...], w_ref[...])


def _norm_matmul(x, gain, w_bf16, tm, tn):
    m, k = x.shape
    n = w_bf16.shape[1]
    return pl.pallas_call(
        _norm_matmul_kernel,
        grid=(m // tm, n // tn),
        in_specs=[pl.BlockSpec((tm, k), lambda i, j: (i, 0)),
                  pl.BlockSpec((1, k), lambda i, j: (0, 0)),
                  pl.BlockSpec((k, tn), lambda i, j: (0, j))],
        out_specs=pl.BlockSpec((tm, tn), lambda i, j: (i, j)),
        out_shape=jax.ShapeDtypeStruct((m, n), F32),
        scratch_shapes=[pltpu.VMEM((tm, k), BF16)],
        compiler_params=_cparams("parallel", "arbitrary"),
        name="norm_in_proj",
    )(x, gain.reshape(1, k), w_bf16)


def _forget_lower_bound(lbraw):
    e = jnp.exp(lbraw - jnp.max(lbraw, axis=0, keepdims=True))
    return e[0:1] / jnp.sum(e, axis=0, keepdims=True)


def _hgrn_seq_kernel(hq_ref, hf_ref, hi_ref, og_ref, lbraw_ref, gn_ref, o_ref, s_out_ref,
                     st_ref, q_s, k_s, b_s, v_s, acc_s, *, chunk, heads):
    c = pl.program_id(1)

    @pl.when(c == 0)
    def _():
        st_ref[...] = jnp.zeros_like(st_ref)

    row = lax.broadcasted_iota(jnp.int32, (chunk, HG_DK), 0)
    rr = lax.broadcasted_iota(jnp.int32, (chunk, chunk), 0)
    cc = lax.broadcasted_iota(jnp.int32, (chunk, chunk), 1)
    sub = lax.broadcasted_iota(jnp.int32, (HG_SUB, HG_DK), 0)

    for hh in range(heads):
        cols = slice(hh * 128, (hh + 1) * 128)
        lb = _forget_lower_bound(lbraw_ref[:, cols])
        q = _silu(hq_ref[:, cols])
        f = lb + (1.0 - lb) * jax.nn.sigmoid(hf_ref[:, cols])
        k = 1.0 - f
        v = hi_ref[:, cols]

        b = jnp.log(f)
        shift = 1
        while shift < chunk:
            b = b + jnp.where(row >= shift, pltpu.roll(b, shift, axis=0), 0.0)
            shift *= 2

        q_s[hh] = q
        k_s[hh] = k
        b_s[hh] = b
        v_s[hh] = v

        st = st_ref[hh]
        o = _dot_nt((q * jnp.exp(b)).astype(BF16), st.astype(BF16))

        a = jnp.zeros((chunk, chunk), F32)
        hs = HG_SUB
        while hs < chunk:
            blk = 2 * hs
            ref_rows = jnp.concatenate(
                [jnp.broadcast_to(b_s[hh, m0 + hs - 1:m0 + hs, :], (blk, HG_DK)) for m0 in range(0, chunk, blk)],
                axis=0)
            second = (row & (blk - 1)) >= hs
            qd = jnp.where(second, q * jnp.exp(jnp.minimum(b - ref_rows, 0.0)), 0.0)
            kd = jnp.where(second, 0.0, k * jnp.exp(jnp.minimum(ref_rows - b, 0.0)))
            a_l = _dot_nt(qd.astype(BF16), kd.astype(BF16))
            sh = int(math.log2(blk))
            a = a + jnp.where((rr >> sh) == (cc >> sh), a_l, 0.0)
            hs = blk
        o = o + _dot(a.astype(BF16), v.astype(BF16))

        for r0 in range(0, chunk, HG_SUB):
            qi = q_s[hh, r0:r0 + HG_SUB, :]
            bi = b_s[hh, r0:r0 + HG_SUB, :]
            acc = jnp.zeros((HG_SUB, HG_DV), F32)
            for s in range(HG_SUB):
                ks = k_s[hh, r0 + s:r0 + s + 1, :]
                bs = b_s[hh, r0 + s:r0 + s + 1, :]
                vs = v_s[hh, r0 + s:r0 + s + 1, :]
                e = jnp.where(sub >= s, jnp.exp(jnp.minimum(bi - bs, 0.0)), 0.0)
                w = jnp.sum(qi * ks * e, axis=-1, keepdims=True)
                acc = acc + w * vs
            acc_s[hh, r0:r0 + HG_SUB, :] = acc
        o = o + acc_s[hh]

        b_last = b_s[hh, chunk - 1:chunk, :]
        kd = k * jnp.exp(b_last - b)
        st_new = st * jnp.exp(b_last) + _dot(v.T.astype(BF16), kd.astype(BF16))
        st_ref[hh] = st_new

        ms = jnp.mean(o * o, axis=-1, keepdims=True)
        o_ref[:, cols] = (o * lax.rsqrt(ms + EPS) * gn_ref[...] * _silu(og_ref[:, cols])).astype(o_ref.dtype)

    @pl.when(c == pl.num_programs(1) - 1)
    def _():
        for hh in range(heads):
            s_out_ref[hh] = st_ref[hh].T


def _hgrn_seq(proj, hg_lb, gn, chunk=HG_CHUNK, heads=HG_HEADS_PER_STEP):
    t = proj.shape[0]
    w = 128 * heads
    blk = lambda off: pl.BlockSpec((chunk, w), lambda h, c, off=off: (c, off // heads + h))
    return pl.pallas_call(
        functools.partial(_hgrn_seq_kernel, chunk=chunk, heads=heads),
        grid=(H_HG // heads, t // chunk),
        in_specs=[blk(COL_HQ), blk(COL_HF), blk(COL_HI), blk(COL_OG),
                  pl.BlockSpec((hg_lb.shape[0], w), lambda h, c: (0, h)),
                  pl.BlockSpec((1, HG_DV), lambda h, c: (0, 0))],
        out_specs=[pl.BlockSpec((chunk, w), lambda h, c: (c, h)),
                   pl.BlockSpec((heads, HG_DK, HG_DV), lambda h, c: (h, 0, 0))],
        out_shape=[jax.ShapeDtypeStruct((t, HG_VW), BF16),
                   jax.ShapeDtypeStruct((H_HG, HG_DK, HG_DV), F32)],
        scratch_shapes=[pltpu.VMEM((heads, HG_DV, HG_DK), F32)] + [pltpu.VMEM((heads, chunk, 128), F32)] * 5,
        compiler_params=_cparams("parallel", "arbitrary"),
        name="hgrn_seq",
    )(proj, proj, proj, proj, hg_lb, gn.reshape(1, HG_DV))


def _hgrn_step_kernel(hqc_ref, hfc_ref, hi_ref, og_ref, lbc_ref, gn_ref, s_ref, o_ref, s_out_ref):
    lbraw = lbc_ref[...]
    e = jnp.exp(lbraw - jnp.max(lbraw, axis=0, keepdims=True))
    lb = e[0] / jnp.sum(e, axis=0)
    q = _silu(hqc_ref[...])
    f = lb + (1.0 - lb) * jax.nn.sigmoid(hfc_ref[...])
    k = 1.0 - f
    v = hi_ref[...]
    s_new = f * s_ref[...] + k * v
    s_out_ref[...] = s_new
    o = jnp.sum(s_new * q, axis=1, keepdims=True)
    ms = jnp.mean(o * o, axis=-1, keepdims=True)
    o_ref[...] = (o * lax.rsqrt(ms + EPS) * gn_ref[...] * _silu(og_ref[...])).astype(o_ref.dtype)


def _hgrn_step(proj, state, hg_lb, gn):
    nb = proj.shape[0]
    col = lambda a: a.reshape(nb, H_HG, HG_DK, 1)
    rowv = lambda a: a.reshape(nb, H_HG, 1, HG_DV)
    hq = col(proj[:, 0:HG_W])
    hf = col(proj[:, HG_W:2 * HG_W])
    hi = rowv(proj[:, 2 * HG_W:2 * HG_W + HG_VW])
    og = rowv(proj[:, 2 * HG_W + HG_VW:2 * HG_W + 2 * HG_VW])
    nl = hg_lb.shape[0]
    cspec = pl.BlockSpec((None, H_HG, HG_DK, 1), lambda b: (b, 0, 0, 0))
    rspec = pl.BlockSpec((None, H_HG, 1, HG_DV), lambda b: (b, 0, 0, 0))
    sspec = pl.BlockSpec((None, H_HG, HG_DK, HG_DV), lambda b: (b, 0, 0, 0))
    return pl.pallas_call(
        _hgrn_step_kernel,
        grid=(nb,),
        in_specs=[cspec, cspec, rspec, rspec,
                  pl.BlockSpec((nl, H_HG, HG_DK, 1), lambda b: (0, 0, 0, 0)),
                  pl.BlockSpec((1, 1, HG_DV), lambda b: (0, 0, 0)),
                  sspec],
        out_specs=[rspec, sspec],
        out_shape=[jax.ShapeDtypeStruct((nb, H_HG, 1, HG_DV), BF16),
                   jax.ShapeDtypeStruct((nb, H_HG, HG_DK, HG_DV), F32)],
        compiler_params=_cparams("parallel"),
        name="hgrn_step",
    )(hq, hf, hi, og, hg_lb.reshape(nl, H_HG, HG_DK, 1), gn.reshape(1, 1, HG_DV), state)


def _bias_from_rel(rel, tab_ref, h):
    out = jnp.full(rel.shape, tab_ref[0, h], F32)
    for bkt in range(1, MAX_EXACT + 1):
        out = jnp.where(rel >= bkt, tab_ref[bkt, h], out)
    for bkt, thr in zip(range(MAX_EXACT + 1, N_BUCKETS), BUCKET_THRESHOLDS):
        out = jnp.where(rel >= thr, tab_ref[bkt, h], out)
    return jnp.where(rel >= 0, out, NEG_INF)


def _bias_tiles_kernel(tab_ref, o_ref, *, base, tile_step, row_step, col_step, scale):
    h = pl.program_id(0)
    t = pl.program_id(1)
    shape = o_ref.shape
    r = lax.broadcasted_iota(jnp.int32, shape, 0)
    c = lax.broadcasted_iota(jnp.int32, shape, 1)
    rel = base + t * tile_step + r * row_step + c * col_step
    bias = _bias_from_rel(rel, tab_ref, h)
    o_ref[...] = bias if scale == 1.0 else bias * scale


def _bias_tiles(rel_bias, n_tiles, rows, cols, base, tile_step, row_step, col_step, scale=1.0):
    return pl.pallas_call(
        functools.partial(_bias_tiles_kernel, base=base, tile_step=tile_step, row_step=row_step,
                          col_step=col_step, scale=scale),
        grid=(H_ATT, n_tiles),
        in_specs=[pl.BlockSpec(memory_space=pltpu.SMEM)],
        out_specs=pl.BlockSpec((None, None, rows, cols), lambda h, t: (h, t, 0, 0)),
        out_shape=jax.ShapeDtypeStruct((H_ATT, n_tiles, rows, cols), F32),
        compiler_params=_cparams("parallel", "parallel"),
        name="rel_bias_tiles",
    )(rel_bias)


def _block_mean_kernel(k_ref, o_ref):
    o_ref[...] = jnp.mean(k_ref[...], axis=0, keepdims=True)


def _block_mean(proj):
    t = proj.shape[0]
    nb = t // MOBA_BLOCK
    out = pl.pallas_call(
        _block_mean_kernel,
        grid=(nb,),
        in_specs=[pl.BlockSpec((MOBA_BLOCK, ATT_W), lambda n: (n, COLK_AK))],
        out_specs=pl.BlockSpec((None, 1, ATT_W), lambda n: (n, 0, 0)),
        out_shape=jax.ShapeDtypeStruct((nb, 1, ATT_W), F32),
        compiler_params=_cparams("parallel"),
        name="moba_block_mean",
    )(proj)
    return out.reshape(nb, ATT_W)


def _top_blocks(s, n_valid, axis):
    n = float(s.shape[axis])
    idx = lax.broadcasted_iota(jnp.int32, s.shape, axis).astype(F32)
    s = jnp.where(idx < n_valid, s, NEG_INF)
    sel = jnp.zeros(s.shape, F32)
    for _ in range(MOBA_TOPK):
        m = jnp.max(s, axis=axis, keepdims=True)
        first = jnp.min(jnp.where(s == m, idx, n), axis=axis, keepdims=True)
        pick = jnp.logical_and(idx == first, m > NEG_INF)
        sel = jnp.where(pick, 1.0, sel)
        s = jnp.where(pick, NEG_INF, s)
    return sel


def _moba_seq_kernel(q_ref, k_ref, v_ref, km_ref, bias_ref, o_ref, kb_s, vt_s, pen_s, penf_s, acc_s, s_s, *,
                     n_far):
    i = pl.program_id(1)
    nblk = vt_s.shape[0]

    @pl.when(i == 0)
    def _():
        kb_s[...] = k_ref[...].astype(BF16)

        def xpose(j, carry):
            r0 = pl.multiple_of(j * MOBA_BLOCK, MOBA_BLOCK)
            vt_s[j] = v_ref[pl.ds(r0, MOBA_BLOCK), :].T.astype(BF16)
            return carry

        lax.fori_loop(0, nblk, xpose, 0)

    tq = q_ref.shape[0]
    qt = q_ref[...].T
    qtb = (qt * (ATT_SCALE * LOG2E)).astype(BF16)
    scores = jnp.dot(km_ref[...], qt * ATT_SCALE, precision=lax.Precision.HIGHEST,
                     preferred_element_type=F32)
    blk_id = lax.broadcasted_iota(jnp.int32, scores.shape, 0)
    keep = jnp.logical_or(_top_blocks(scores, i, axis=0) > 0.5, blk_id == i)
    pen = jnp.where(keep, 0.0, NEG_INF)
    far_bias = bias_ref[n_far, 0:1, :]
    pen_s[0:nblk, :] = pen
    penf_s[0:nblk, :] = pen + far_bias
    pen_s[nblk:nblk + 1, :] = jnp.full((1, tq), NEG_INF, F32)
    penf_s[nblk:nblk + 1, :] = jnp.full((1, tq), NEG_INF, F32)

    def group(g, m_old, l_old, first):
        blocks = []
        mg = m_old
        for u in range(MOBA_GROUP):
            pos = g * MOBA_GROUP + u
            j = jnp.maximum(i - pos, 0)
            pen_row = jnp.where(pos <= i, j, nblk)
            c0 = pl.multiple_of(j * MOBA_BLOCK, MOBA_BLOCK)
            s = _dot(kb_s[pl.ds(c0, MOBA_BLOCK), :], qtb)
            if first and u < n_far:
                s = s + bias_ref[u] + pen_s[pl.ds(pen_row, 1), :]
            else:
                s = s + penf_s[pl.ds(pen_row, 1), :]
            s_s[u] = s
            smax = jnp.max(s, axis=0, keepdims=True)
            mg = smax if mg is None else jnp.maximum(mg, smax)
            blocks.append(j)
        l_new = None if first else jnp.exp2(m_old - mg) * l_old
        pv = None
        for u, j in enumerate(blocks):
            p = jnp.exp2(s_s[u] - mg)
            psum = jnp.sum(p, axis=0, keepdims=True)
            l_new = psum if l_new is None else l_new + psum
            d = _dot(vt_s[j], p.astype(BF16))
            pv = d if pv is None else pv + d
        if first:
            acc_s[...] = pv
        else:
            acc_s[...] = jnp.exp2(m_old - mg) * acc_s[...] + pv
        return mg, l_new

    m, l = group(0, None, None, True)
    m, l = lax.fori_loop(1, (i + MOBA_GROUP) // MOBA_GROUP, lambda g, c: group(g, c[0], c[1], False), (m, l))
    o_ref[...] = (acc_s[...] / l).T.astype(o_ref.dtype)


def _moba_seq(proj, rel_bias):
    t = proj.shape[0]
    nq = t // MOBA_BLOCK
    kmean = _block_mean(proj)
    n_far = -(-(BUCKET_THRESHOLDS[-1] + MOBA_BLOCK) // MOBA_BLOCK)
    assert MOBA_GROUP >= n_far and nq % MOBA_GROUP == 0
    bias = _bias_tiles(rel_bias, n_far + 1, MOBA_BLOCK, MOBA_BLOCK, 0, MOBA_BLOCK, -1, 1, scale=LOG2E)
    return pl.pallas_call(
        functools.partial(_moba_seq_kernel, n_far=n_far),
        grid=(H_ATT, nq),
        in_specs=[pl.BlockSpec((MOBA_BLOCK, ATT_DH), lambda h, i: (i, COL_AQ + h)),
                  pl.BlockSpec((t, ATT_DH), lambda h, i: (0, COL_AK + h)),
                  pl.BlockSpec((t, ATT_DH), lambda h, i: (0, COL_AV + h)),
                  pl.BlockSpec((nq, ATT_DH), lambda h, i: (0, h)),
                  pl.BlockSpec((None, n_far + 1, MOBA_BLOCK, MOBA_BLOCK), lambda h, i: (h, 0, 0, 0))],
        out_specs=pl.BlockSpec((MOBA_BLOCK, ATT_DH), lambda h, i: (i, h)),
        out_shape=jax.ShapeDtypeStruct((t, ATT_W), BF16),
        scratch_shapes=[pltpu.VMEM((t, ATT_DH), BF16), pltpu.VMEM((nq, ATT_DH, MOBA_BLOCK), BF16),
                        pltpu.VMEM((nq + 8, MOBA_BLOCK), F32), pltpu.VMEM((nq + 8, MOBA_BLOCK), F32),
                        pltpu.VMEM((ATT_DH, MOBA_BLOCK), F32),
                        pltpu.VMEM((MOBA_GROUP, MOBA_BLOCK, MOBA_BLOCK), F32)],
        compiler_params=_cparams("parallel", "arbitrary"),
        name="moba_seq",
    )(proj, proj, proj, kmean, bias)


SELECT_PAGES_PER_STEP = 8
MOBA_STEP_SLOTS = 4


def _moba_select_kernel(pt_ref, *refs, pages_per_block):
    n_pg = SELECT_PAGES_PER_STEP
    page_refs = refs[:n_pg]
    q_ref, sel_ref, km_s = refs[n_pg:]
    n = pl.program_id(1)
    page_rows = page_refs[0].shape[0]
    blocks_per_step = n_pg // pages_per_block
    for blk in range(blocks_per_step):
        tot = page_refs[blk * pages_per_block][...].sum(axis=0)
        for pg in range(1, pages_per_block):
            tot = tot + page_refs[blk * pages_per_block + pg][...].sum(axis=0)
        km_s[n * blocks_per_step + blk] = tot / (pages_per_block * page_rows)

    @pl.when(n == pl.num_programs(1) - 1)
    def _():
        prod = km_s[...] * (q_ref[...] * ATT_SCALE)[None]
        s = jnp.sum(prod, axis=-1, keepdims=True)
        nb = s.shape[0]
        mask = _top_blocks(s, nb, axis=0)
        idx = lax.broadcasted_iota(jnp.int32, s.shape, 0).astype(F32)
        for r in range(MOBA_TOPK):
            first = jnp.min(jnp.where(mask > 0.5, idx, float(nb)), axis=0)
            sel_ref[r] = first.astype(jnp.int32)
            mask = jnp.where(idx == first[None], 0.0, mask)


def _moba_select(cache_k4, page_table, q_heads):
    nb_seq, n_pages = page_table.shape
    page_rows = cache_k4.shape[1]
    ppb = MOBA_BLOCK // page_rows
    n_pg = SELECT_PAGES_PER_STEP
    assert ppb * page_rows == MOBA_BLOCK and n_pg % ppb == 0 and n_pages % n_pg == 0
    n_blocks = n_pages // ppb
    assert n_blocks >= MOBA_TOPK
    kspec = lambda p: pl.BlockSpec((None, page_rows, H_ATT, ATT_DH),
                                   lambda b, n, pt, p=p: (pt[b * n_pages + n_pg * n + p], 0, 0, 0))
    return pl.pallas_call(
        functools.partial(_moba_select_kernel, pages_per_block=ppb),
        grid_spec=pltpu.PrefetchScalarGridSpec(
            num_scalar_prefetch=1,
            grid=(nb_seq, n_pages // n_pg),
            in_specs=[kspec(p) for p in range(n_pg)]
                     + [pl.BlockSpec((None, H_ATT, ATT_DH), lambda b, n, pt: (b, 0, 0))],
            out_specs=pl.BlockSpec((None, MOBA_TOPK, H_ATT, 1), lambda b, n, pt: (b, 0, 0, 0)),
            scratch_shapes=[pltpu.VMEM((n_blocks, H_ATT, ATT_DH), F32)]),
        out_shape=jax.ShapeDtypeStruct((nb_seq, MOBA_TOPK, H_ATT, 1), jnp.int32),
        compiler_params=_cparams("parallel", "arbitrary"),
        name="moba_select",
    )(page_table.reshape(-1), *([cache_k4] * n_pg), q_heads)


def _moba_step_kernel(pt_ref, sel_ref, q_ref, kn_ref, vn_ref, b0_ref, b1_ref, b2_ref, bown_ref, ck_hbm, cv_hbm,
                      o_ref, kbuf, vbuf, sem, *, n_pages):
    n_pg = MOBA_TOPK * 2
    b_refs = (b0_ref, b1_ref, b2_ref)
    n_slots = kbuf.shape[0]
    page_rows = kbuf.shape[2]
    b = pl.program_id(0)
    h = pl.program_id(1)
    n_heads = pl.num_programs(1)
    n_steps = pl.num_programs(0) * n_heads
    step = b * n_heads + h
    slot = step % n_slots

    def page_copies(bb, hh, sl):
        cps = []
        for r in range(MOBA_TOPK):
            blk = sel_ref[(bb * MOBA_TOPK + r) * n_heads + hh]
            for half in range(2):
                page = pt_ref[bb * n_pages + 2 * blk + half]
                idx = 2 * r + half
                cps.append(pltpu.make_async_copy(ck_hbm.at[page, :, hh, :], kbuf.at[sl, idx], sem.at[sl, idx]))
                cps.append(pltpu.make_async_copy(cv_hbm.at[page, :, hh, :], vbuf.at[sl, idx], sem.at[sl, n_pg + idx]))
        return cps

    def start_step(st):
        for cp in page_copies(st // n_heads, st % n_heads, st % n_slots):
            cp.start()

    @pl.when(step == 0)
    def _():
        for ahead in range(n_slots - 1):
            @pl.when(ahead < n_steps)
            def _():
                start_step(jnp.int32(ahead))

    @pl.when(step + n_slots - 1 < n_steps)
    def _():
        start_step(step + n_slots - 1)

    for cp in page_copies(b, h, slot):
        cp.wait()

    qs = q_ref[...] * ATT_SCALE
    qb = qs.astype(BF16)
    logits = []
    for r in range(MOBA_TOPK):
        bias = b_refs[r][...]
        for half in range(2):
            kk = kbuf[slot, 2 * r + half].astype(BF16)
            logits.append(_dot_nt(qb, kk) + bias[:, half * page_rows:(half + 1) * page_rows])
    l_own = jnp.sum(qs * kn_ref[...], axis=-1, keepdims=True) + bown_ref[:, 0:1]
    m = l_own
    for lg in logits:
        m = jnp.maximum(m, jnp.max(lg, axis=-1, keepdims=True))
    p_own = jnp.exp(l_own - m)
    den = p_own
    acc = p_own * vn_ref[...]
    for idx, lg in enumerate(logits):
        p = jnp.exp(lg - m)
        den = den + jnp.sum(p, axis=-1, keepdims=True)
        acc = acc + _dot(p.astype(BF16), vbuf[slot, idx].astype(BF16))
    o_ref[...] = (acc / den).astype(o_ref.dtype)


def _moba_step(proj, cache_k4, cache_v4, page_table, rel_bias):
    nb_seq, n_pages = page_table.shape
    page_rows = cache_k4.shape[1]
    ppb = MOBA_BLOCK // page_rows
    assert ppb == 2
    n_blocks = n_pages // ppb
    n_past = n_pages * page_rows
    head_rows = lambda a: a.reshape(nb_seq, H_ATT, 1, ATT_DH)
    aq = proj[:, COL_AQ * 128:COL_AQ * 128 + ATT_W]
    ak = proj[:, COL_AK * 128:COL_AK * 128 + ATT_W]
    av = proj[:, COL_AV * 128:COL_AV * 128 + ATT_W]
    sel = _moba_select(cache_k4, page_table, aq.reshape(nb_seq, H_ATT, ATT_DH))
    bias = _bias_tiles(rel_bias, n_blocks + 1, 1, MOBA_BLOCK, n_past, -MOBA_BLOCK, 0, -1)

    def bias_spec(r):
        return pl.BlockSpec((None, None, 1, MOBA_BLOCK),
                            lambda b, h, pt, sl: (h, sl[(b * MOBA_TOPK + r) * H_ATT + h], 0, 0))

    row_spec = pl.BlockSpec((None, None, 1, ATT_DH), lambda b, h, pt, sl: (b, h, 0, 0))
    hbm = pl.BlockSpec(memory_space=pl.ANY)
    n_pg = 2 * MOBA_TOPK
    out = pl.pallas_call(
        functools.partial(_moba_step_kernel, n_pages=n_pages),
        grid_spec=pltpu.PrefetchScalarGridSpec(
            num_scalar_prefetch=2,
            grid=(nb_seq, H_ATT),
            in_specs=[row_spec, row_spec, row_spec] + [bias_spec(r) for r in range(MOBA_TOPK)]
                     + [pl.BlockSpec((None, None, 1, MOBA_BLOCK), lambda b, h, pt, sl: (h, n_blocks, 0, 0)),
                        hbm, hbm],
            out_specs=row_spec,
            scratch_shapes=[pltpu.VMEM((MOBA_STEP_SLOTS, n_pg, page_rows, ATT_DH), F32),
                            pltpu.VMEM((MOBA_STEP_SLOTS, n_pg, page_rows, ATT_DH), F32),
                            pltpu.SemaphoreType.DMA((MOBA_STEP_SLOTS, 2 * n_pg))]),
        out_shape=jax.ShapeDtypeStruct((nb_seq, H_ATT, 1, ATT_DH), BF16),
        compiler_params=_cparams("arbitrary", "arbitrary"),
        name="moba_step",
    )(page_table.reshape(-1), sel.reshape(-1), head_rows(aq), head_rows(ak), head_rows(av),
      bias, bias, bias, bias, cache_k4, cache_v4)
    return out.reshape(nb_seq, ATT_W)


def _merge_kernel(oh_ref, oa_ref, wh_ref, wa_ref, gh_ref, ga_ref, o_ref):
    br_h = _dot(oh_ref[...], wh_ref[...])
    br_a = _dot(oa_ref[...], wa_ref[...])
    o_ref[...] = (jax.nn.sigmoid(gh_ref[...]) * br_h + jax.nn.sigmoid(ga_ref[...]) * br_a).astype(o_ref.dtype)


def _merge(o_hg, o_att, w_bh, w_bm, proj, tm):
    m = o_hg.shape[0]
    tn = 1024
    return pl.pallas_call(
        _merge_kernel,
        grid=(m // tm, D_MODEL // tn),
        in_specs=[pl.BlockSpec((tm, HG_VW), lambda i, n: (i, 0)),
                  pl.BlockSpec((tm, ATT_W), lambda i, n: (i, 0)),
                  pl.BlockSpec((HG_VW, tn), lambda i, n: (0, n)),
                  pl.BlockSpec((ATT_W, tn), lambda i, n: (0, n)),
                  pl.BlockSpec((tm, tn), lambda i, n: (i, COLK_GHG + n)),
                  pl.BlockSpec((tm, tn), lambda i, n: (i, COLK_GATT + n))],
        out_specs=pl.BlockSpec((tm, tn), lambda i, n: (i, n)),
        out_shape=jax.ShapeDtypeStruct((m, D_MODEL), BF16),
        compiler_params=_cparams("parallel", "arbitrary"),
        name="gated_merge",
    )(o_hg, o_att, w_bh, w_bm, proj, proj)


def _out_proj_kernel(m_ref, w_ref, x_ref, g_ref, o_ref):
    z = _dot(m_ref[...], w_ref[...])
    ms = jnp.mean(z * z, axis=-1, keepdims=True)
    o_ref[...] = x_ref[...] + z * lax.rsqrt(ms + EPS) * g_ref[...]


def _out_proj(merged, w_out, x, gain, tm):
    m = x.shape[0]
    return pl.pallas_call(
        _out_proj_kernel,
        grid=(m // tm,),
        in_specs=[pl.BlockSpec((tm, D_MODEL), lambda i: (i, 0)),
                  pl.BlockSpec((D_MODEL, D_MODEL), lambda i: (0, 0)),
                  pl.BlockSpec((tm, D_MODEL), lambda i: (i, 0)),
                  pl.BlockSpec((1, D_MODEL), lambda i: (0, 0))],
        out_specs=pl.BlockSpec((tm, D_MODEL), lambda i: (i, 0)),
        out_shape=jax.ShapeDtypeStruct((m, D_MODEL), F32),
        compiler_params=_cparams("parallel"),
        name="out_proj_residual",
    )(merged, w_out, x, gain.reshape(1, D_MODEL))


def _ffn_kernel(x_ref, gpre_ref, wg_ref, wu_ref, cw_ref, cb_ref, p0_ref, p1_ref, wd_ref, gpost_ref,
                o_ref, g_out_ref, h_s, acc_s, carry_s, *, seq, tf):
    i = pl.program_id(0)
    j = pl.program_id(1)
    tm = x_ref.shape[0]

    @pl.when(j == 0)
    def _():
        x = x_ref[...]
        ms = jnp.mean(x * x, axis=-1, keepdims=True)
        h_s[...] = (x * lax.rsqrt(ms + EPS) * gpre_ref[...]).astype(BF16)
        acc_s[...] = jnp.zeros_like(acc_s)

    h = h_s[...]
    g = _dot(h, wg_ref[...])
    u = _dot(h, wu_ref[...])
    if seq:
        @pl.when(i == 0)
        def _():
            carry_s[j, 0:1, :] = p0_ref[...]
            carry_s[j, 1:2, :] = p1_ref[...]

        row = lax.broadcasted_iota(jnp.int32, (tm, tf), 0)
        c0 = carry_s[j, 0:1, :]
        c1 = carry_s[j, 1:2, :]
        prev1 = jnp.where(row == 0, c1, pltpu.roll(g, 1, axis=0))
        prev2 = jnp.where(row == 0, c0, jnp.where(row == 1, c1, pltpu.roll(g, 2, axis=0)))
        carry_s[j, 0:2, :] = g[tm - 2:tm, :]
        g_out_ref[...] = g[tm - 2:tm, :]
    else:
        prev2 = p0_ref[...]
        prev1 = p1_ref[...]
        g_out_ref[...] = g
    cw = cw_ref[...]
    gc = cb_ref[...] + cw[0:1] * prev2 + cw[1:2] * prev1 + cw[2:3] * g
    a = jax.nn.gelu(gc, approximate=True) * u
    acc_s[...] += _dot(a.astype(BF16), wd_ref[...])

    @pl.when(j == pl.num_programs(1) - 1)
    def _():
        y = acc_s[...]
        ms = jnp.mean(y * y, axis=-1, keepdims=True)
        o_ref[...] = x_ref[...] + y * lax.rsqrt(ms + EPS) * gpost_ref[...]


def _ffn(x, gpre, w_up, conv_w, conv_b, prev0, prev1, w_down, gpost, *, seq, tm, tf=512):
    m = x.shape[0]
    nf = D_FF // tf
    assert CONV_W == 3 and nf * tf == D_FF
    prow = 1 if seq else tm
    pspec = pl.BlockSpec((prow, tf), (lambda i, j: (0, j)) if seq else (lambda i, j: (i, j)))
    if seq:
        gspec = pl.BlockSpec((None, CONV_W - 1, tf), lambda i, j: (i, 0, j))
        gshape = (m // tm, CONV_W - 1, D_FF)
    else:
        gspec = pl.BlockSpec((tm, tf), lambda i, j: (i, j))
        gshape = (m, D_FF)
    vec = lambda n: pl.BlockSpec((1, n), lambda i, j: (0, 0))
    y, g_out = pl.pallas_call(
        functools.partial(_ffn_kernel, seq=seq, tf=tf),
        grid=(m // tm, nf),
        in_specs=[pl.BlockSpec((tm, D_MODEL), lambda i, j: (i, 0)),
                  vec(D_MODEL),
                  pl.BlockSpec((D_MODEL, tf), lambda i, j: (0, j)),
                  pl.BlockSpec((D_MODEL, tf), lambda i, j: (0, nf + j)),
                  pl.BlockSpec((CONV_W, tf), lambda i, j: (0, j)),
                  pl.BlockSpec((1, tf), lambda i, j: (0, j)),
                  pspec, pspec,
                  pl.BlockSpec((tf, D_MODEL), lambda i, j: (j, 0)),
                  vec(D_MODEL)],
        out_specs=[pl.BlockSpec((tm, D_MODEL), lambda i, j: (i, 0)), gspec],
        out_shape=[jax.ShapeDtypeStruct((m, D_MODEL), F32),
                   jax.ShapeDtypeStruct(gshape, F32)],
        scratch_shapes=[pltpu.VMEM((tm, D_MODEL), BF16), pltpu.VMEM((tm, D_MODEL), F32),
                        pltpu.VMEM((nf, 8, tf), F32)],
        compiler_params=_cparams("arbitrary", "arbitrary"),
        name="conv_ffn",
    )(x, gpre.reshape(1, D_MODEL), w_up, w_up, conv_w, conv_b.reshape(1, D_FF), prev0, prev1, w_down,
      gpost.reshape(1, D_MODEL))
    return y, (g_out[-1] if seq else g_out)


def kernel(x_prompt, x_sample, cache_k, cache_v, state_hgrn, state_ffn_conv, page_table, rel_bias, hg_lb,
           norm_mix_pre, norm_mix_post, norm_ffn_pre, norm_ffn_post, w_in, hg_out_norm, w_branch_hgrn,
           w_branch_moba, w_out, w_ffn_up, ffn_conv_w, ffn_conv_b, w_ffn_down):
    nb, t, _ = x_prompt.shape
    db = x_sample.shape[0]
    depth = w_in.shape[0]
    assert nb == 1 and depth == 1 and x_sample.shape[1] == 1
    l = 0
    w_in_b = w_in[l].astype(BF16)
    w_bh = w_branch_hgrn[l].astype(BF16)
    w_bm = w_branch_moba[l].astype(BF16)
    w_o = w_out[l].astype(BF16)
    w_up = w_ffn_up[l].astype(BF16)
    w_dn = w_ffn_down[l].astype(BF16)

    xp = x_prompt.reshape(t, D_MODEL)
    proj = _norm_matmul(xp, norm_mix_pre[l], w_in_b, tm=1024, tn=1024)
    o_hg, s_new = _hgrn_seq(proj, hg_lb, hg_out_norm[l])
    o_att = _moba_seq(proj, rel_bias)
    merged = _merge(o_hg, o_att, w_bh, w_bm, proj, tm=512)
    x1 = _out_proj(merged, w_o, xp, norm_mix_post[l], tm=512)
    zero_row = jnp.zeros((1, D_FF), F32)
    yp, conv_p = _ffn(x1, norm_ffn_pre[l], w_up, ffn_conv_w[l], ffn_conv_b[l], zero_row, zero_row, w_dn,
                      norm_ffn_post[l], seq=True, tm=512)
    kp = proj[:, COL_AK * 128:COL_AK * 128 + ATT_W].reshape(1, 1, t, H_ATT, ATT_DH)
    vp = proj[:, COL_AV * 128:COL_AV * 128 + ATT_W].reshape(1, 1, t, H_ATT, ATT_DH)

    xs = x_sample.reshape(db, D_MODEL)
    n_pool, page_rows = cache_k.shape[1], cache_k.shape[2]
    ck = cache_k.reshape(depth * n_pool, page_rows, H_ATT, ATT_DH)
    cv = cache_v.reshape(depth * n_pool, page_rows, H_ATT, ATT_DH)
    projs = _norm_matmul(xs, norm_mix_pre[l], w_in_b, tm=db, tn=1024)
    o_hg_s, s_new_s = _hgrn_step(projs, state_hgrn[l], hg_lb, hg_out_norm[l])
    o_att_s = _moba_step(projs, ck, cv, page_table, rel_bias)
    merged_s = _merge(o_hg_s.reshape(db, HG_VW), o_att_s, w_bh, w_bm, projs, tm=db)
    x1s = _out_proj(merged_s, w_o, xs, norm_mix_post[l], tm=db)
    buf = state_ffn_conv[l]
    ys, g_s = _ffn(x1s, norm_ffn_pre[l], w_up, ffn_conv_w[l], ffn_conv_b[l], buf[:, 0], buf[:, 1], w_dn,
                   norm_ffn_post[l], seq=False, tm=db)
    ks = projs[:, COL_AK * 128:COL_AK * 128 + ATT_W].reshape(1, db, 1, H_ATT, ATT_DH)
    vs = projs[:, COL_AV * 128:COL_AV * 128 + ATT_W].reshape(1, db, 1, H_ATT, ATT_DH)
    conv_s = jnp.stack([buf[:, 1], g_s], axis=1)

    return (yp.reshape(1, t, D_MODEL), ys.reshape(db, 1, D_MODEL), kp, vp,
            s_new.reshape(1, 1, H_HG, HG_DK, HG_DV), conv_p.reshape(1, 1, CONV_W - 1, D_FF),
            ks, vs, s_new_s.reshape(1, db, H_HG, HG_DK, HG_DV), conv_s.reshape(1, db, CONV_W - 1, D_FF))
```

```python
import functools
import math

import numpy as np
import jax
import jax.numpy as jnp
from jax import lax
from jax.experimental import pallas as pl
from jax.experimental.pallas import tpu as pltpu

F32 = jnp.float32
BF16 = jnp.bfloat16

D_MODEL = 2048
H_HG = 8
HG_DK = 128
HG_DV = 128
HG_W = H_HG * HG_DK
HG_VW = H_HG * HG_DV
H_ATT = 8
ATT_DH = 128
ATT_W = H_ATT * ATT_DH
MOBA_BLOCK = 256
MOBA_TOPK = 3
N_BUCKETS = 32
MAX_EXACT = N_BUCKETS // 2
REL_MAX_DIST = 1024
D_FF = 5632
CONV_W = 3
EPS = 1e-6
N_IN = 2 * HG_W + 2 * HG_VW + 3 * ATT_W + 2 * D_MODEL
ATT_SCALE = ATT_DH ** -0.5
LOG2E = math.log2(math.e)

COL_HQ, COL_HF, COL_HI, COL_OG = 0, 8, 16, 24
COL_AQ, COL_AK, COL_AV = 32, 40, 48
COLK_AK, COLK_GHG, COLK_GATT = 5, 7, 9

HG_CHUNK = 128
HG_SUB = 8
HG_HEADS_PER_STEP = 2
MOBA_GROUP = 8
VMEM_LIMIT = 56 * 1024 * 1024
NEG_INF = float("-inf")


def _bucket_thresholds():
    n = np.arange(MAX_EXACT, 4 * REL_MAX_DIST, dtype=np.float64)
    large = MAX_EXACT + (np.log(n / MAX_EXACT) / math.log(REL_MAX_DIST / MAX_EXACT)
                         * (N_BUCKETS - MAX_EXACT)).astype(np.int64)
    large = np.minimum(large, N_BUCKETS - 1)
    return [int(n[np.argmax(large >= b)]) for b in range(MAX_EXACT + 1, N_BUCKETS)]


BUCKET_THRESHOLDS = _bucket_thresholds()


def _cparams(*sem):
    return pltpu.CompilerParams(dimension_semantics=sem, vmem_limit_bytes=VMEM_LIMIT)


def _silu(x):
    return x * jax.nn.sigmoid(x)


def _dot(a, b):
    return jnp.dot(a, b, preferred_element_type=F32)


def _dot_nt(a, b, precision=None):
    return lax.dot_general(a, b, (((1,), (1,)), ((), ())), precision=precision,
                           preferred_element_type=F32)


def _norm_matmul_kernel(x_ref, g_ref, w_ref, o_ref, h_ref):
    @pl.when(pl.program_id(1) == 0)
    def _():
        x = x_ref[...]
        ms = jnp.mean(x * x, axis=-1, keepdims=True)
        h_ref[...] = (x * lax.rsqrt(ms + EPS) * g_ref[...]).astype(BF16)

    o_ref[...] = _dot(h_ref[...], w_ref[...])


def _norm_matmul(x, gain, w_bf16, tm, tn):
    m, k = x.shape
    n = w_bf16.shape[1]
    return pl.pallas_call(
        _norm_matmul_kernel,
        grid=(m // tm, n // tn),
        in_specs=[pl.BlockSpec((tm, k), lambda i, j: (i, 0)),
                  pl.BlockSpec((1, k), lambda i, j: (0, 0)),
                  pl.BlockSpec((k, tn), lambda i, j: (0, j))],
        out_specs=pl.BlockSpec((tm, tn), lambda i, j: (i, j)),
        out_shape=jax.ShapeDtypeStruct((m, n), F32),
        scratch_shapes=[pltpu.VMEM((tm, k), BF16)],
        compiler_params=_cparams("parallel", "arbitrary"),
        name="norm_in_proj",
    )(x, gain.reshape(1, k), w_bf16)


def _forget_lower_bound(lbraw):
    e = jnp.exp(lbraw - jnp.max(lbraw, axis=0, keepdims=True))
    return e[0:1] / jnp.sum(e, axis=0, keepdims=True)


def _hgrn_seq_kernel(hq_ref, hf_ref, hi_ref, og_ref, lbraw_ref, gn_ref, o_ref, s_out_ref,
                     st_ref, q_s, k_s, b_s, v_s, acc_s, *, chunk, heads):
    c = pl.program_id(1)

    @pl.when(c == 0)
    def _():
        st_ref[...] = jnp.zeros_like(st_ref)

    row = lax.broadcasted_iota(jnp.int32, (chunk, HG_DK), 0)
    rr = lax.broadcasted_iota(jnp.int32, (chunk, chunk), 0)
    cc = lax.broadcasted_iota(jnp.int32, (chunk, chunk), 1)
    sub = lax.broadcasted_iota(jnp.int32, (HG_SUB, HG_DK), 0)

    for hh in range(heads):
        cols = slice(hh * 128, (hh + 1) * 128)
        lb = _forget_lower_bound(lbraw_ref[:, cols])
        q = _silu(hq_ref[:, cols])
        f = lb + (1.0 - lb) * jax.nn.sigmoid(hf_ref[:, cols])
        k = 1.0 - f
        v = hi_ref[:, cols]

        b = jnp.log(f)
        shift = 1
        while shift < chunk:
            b = b + jnp.where(row >= shift, pltpu.roll(b, shift, axis=0), 0.0)
            shift *= 2

        q_s[hh] = q
        k_s[hh] = k
        b_s[hh] = b
        v_s[hh] = v

        st = st_ref[hh]
        o = _dot_nt((q * jnp.exp(b)).astype(BF16), st.astype(BF16))

        a = jnp.zeros((chunk, chunk), F32)
        hs = HG_SUB
        while hs < chunk:
            blk = 2 * hs
            ref_rows = jnp.concatenate(
                [jnp.broadcast_to(b_s[hh, m0 + hs - 1:m0 + hs, :], (blk, HG_DK)) for m0 in range(0, chunk, blk)],
                axis=0)
            second = (row & (blk - 1)) >= hs
            qd = jnp.where(second, q * jnp.exp(jnp.minimum(b - ref_rows, 0.0)), 0.0)
            kd = jnp.where(second, 0.0, k * jnp.exp(jnp.minimum(ref_rows - b, 0.0)))
            a_l = _dot_nt(qd.astype(BF16), kd.astype(BF16))
            sh = int(math.log2(blk))
            a = a + jnp.where((rr >> sh) == (cc >> sh), a_l, 0.0)
            hs = blk
        o = o + _dot(a.astype(BF16), v.astype(BF16))

        for r0 in range(0, chunk, HG_SUB):
            qi = q_s[hh, r0:r0 + HG_SUB, :]
            bi = b_s[hh, r0:r0 + HG_SUB, :]
            acc = jnp.zeros((HG_SUB, HG_DV), F32)
            for s in range(HG_SUB):
                ks = k_s[hh, r0 + s:r0 + s + 1, :]
                bs = b_s[hh, r0 + s:r0 + s + 1, :]
                vs = v_s[hh, r0 + s:r0 + s + 1, :]
                e = jnp.where(sub >= s, jnp.exp(jnp.minimum(bi - bs, 0.0)), 0.0)
                w = jnp.sum(qi * ks * e, axis=-1, keepdims=True)
                acc = acc + w * vs
            acc_s[hh, r0:r0 + HG_SUB, :] = acc
        o = o + acc_s[hh]

        b_last = b_s[hh, chunk - 1:chunk, :]
        kd = k * jnp.exp(b_last - b)
        st_new = st * jnp.exp(b_last) + _dot(v.T.astype(BF16), kd.astype(BF16))
        st_ref[hh] = st_new

        ms = jnp.mean(o * o, axis=-1, keepdims=True)
        o_ref[:, cols] = (o * lax.rsqrt(ms + EPS) * gn_ref[...] * _silu(og_ref[:, cols])).astype(o_ref.dtype)

    @pl.when(c == pl.num_programs(1) - 1)
    def _():
        for hh in range(heads):
            s_out_ref[hh] = st_ref[hh].T


def _hgrn_seq(proj, hg_lb, gn, chunk=HG_CHUNK, heads=HG_HEADS_PER_STEP):
    t = proj.shape[0]
    w = 128 * heads
    blk = lambda off: pl.BlockSpec((chunk, w), lambda h, c, off=off: (c, off // heads + h))
    return pl.pallas_call(
        functools.partial(_hgrn_seq_kernel, chunk=chunk, heads=heads),
        grid=(H_HG // heads, t // chunk),
        in_specs=[blk(COL_HQ), blk(COL_HF), blk(COL_HI), blk(COL_OG),
                  pl.BlockSpec((hg_lb.shape[0], w), lambda h, c: (0, h)),
                  pl.BlockSpec((1, HG_DV), lambda h, c: (0, 0))],
        out_specs=[pl.BlockSpec((chunk, w), lambda h, c: (c, h)),
                   pl.BlockSpec((heads, HG_DK, HG_DV), lambda h, c: (h, 0, 0))],
        out_shape=[jax.ShapeDtypeStruct((t, HG_VW), BF16),
                   jax.ShapeDtypeStruct((H_HG, HG_DK, HG_DV), F32)],
        scratch_shapes=[pltpu.VMEM((heads, HG_DV, HG_DK), F32)] + [pltpu.VMEM((heads, chunk, 128), F32)] * 5,
        compiler_params=_cparams("parallel", "arbitrary"),
        name="hgrn_seq",
    )(proj, proj, proj, proj, hg_lb, gn.reshape(1, HG_DV))


def _hgrn_step_kernel(hqc_ref, hfc_ref, hi_ref, og_ref, lbc_ref, gn_ref, s_ref, o_ref, s_out_ref):
    lbraw = lbc_ref[...]
    e = jnp.exp(lbraw - jnp.max(lbraw, axis=0, keepdims=True))
    lb = e[0] / jnp.sum(e, axis=0)
    q = _silu(hqc_ref[...])
    f = lb + (1.0 - lb) * jax.nn.sigmoid(hfc_ref[...])
    k = 1.0 - f
    v = hi_ref[...]
    s_new = f * s_ref[...] + k * v
    s_out_ref[...] = s_new
    o = jnp.sum(s_new * q, axis=1, keepdims=True)
    ms = jnp.mean(o * o, axis=-1, keepdims=True)
    o_ref[...] = (o * lax.rsqrt(ms + EPS) * gn_ref[...] * _silu(og_ref[...])).astype(o_ref.dtype)


def _hgrn_step(proj, state, hg_lb, gn):
    nb = proj.shape[0]
    col = lambda a: a.reshape(nb, H_HG, HG_DK, 1)
    rowv = lambda a: a.reshape(nb, H_HG, 1, HG_DV)
    hq = col(proj[:, 0:HG_W])
    hf = col(proj[:, HG_W:2 * HG_W])
    hi = rowv(proj[:, 2 * HG_W:2 * HG_W + HG_VW])
    og = rowv(proj[:, 2 * HG_W + HG_VW:2 * HG_W + 2 * HG_VW])
    nl = hg_lb.shape[0]
    cspec = pl.BlockSpec((None, H_HG, HG_DK, 1), lambda b: (b, 0, 0, 0))
    rspec = pl.BlockSpec((None, H_HG, 1, HG_DV), lambda b: (b, 0, 0, 0))
    sspec = pl.BlockSpec((None, H_HG, HG_DK, HG_DV), lambda b: (b, 0, 0, 0))
    return pl.pallas_call(
        _hgrn_step_kernel,
        grid=(nb,),
        in_specs=[cspec, cspec, rspec, rspec,
                  pl.BlockSpec((nl, H_HG, HG_DK, 1), lambda b: (0, 0, 0, 0)),
                  pl.BlockSpec((1, 1, HG_DV), lambda b: (0, 0, 0)),
                  sspec],
        out_specs=[rspec, sspec],
        out_shape=[jax.ShapeDtypeStruct((nb, H_HG, 1, HG_DV), BF16),
                   jax.ShapeDtypeStruct((nb, H_HG, HG_DK, HG_DV), F32)],
        compiler_params=_cparams("parallel"),
        name="hgrn_step",
    )(hq, hf, hi, og, hg_lb.reshape(nl, H_HG, HG_DK, 1), gn.reshape(1, 1, HG_DV), state)


BUCKET_STARTS = list(range(1, MAX_EXACT + 1)) + BUCKET_THRESHOLDS


def _bias_from_rel(rel, lo, hi, tab_ref, h):
    first = sum(1 for start in BUCKET_STARTS if start <= lo)
    out = jnp.full(rel.shape, tab_ref[first, h], F32)
    for bkt in range(first + 1, N_BUCKETS):
        if BUCKET_STARTS[bkt - 1] <= hi:
            out = jnp.where(rel >= BUCKET_STARTS[bkt - 1], tab_ref[bkt, h], out)
    return jnp.where(rel >= 0, out, NEG_INF) if lo < 0 else out


def _bias_tiles_kernel(tab_ref, o_ref, *, base, tile_step, row_step, col_step, scale):
    h = pl.program_id(0)
    n_tiles, rows, cols = o_ref.shape
    r = lax.broadcasted_iota(jnp.int32, (rows, cols), 0)
    c = lax.broadcasted_iota(jnp.int32, (rows, cols), 1)
    in_tile = r * row_step + c * col_step
    spans = [row_step * (rows - 1), col_step * (cols - 1)]
    for t in range(n_tiles):
        off = base + t * tile_step
        lo = off + sum(min(sp, 0) for sp in spans)
        hi = off + sum(max(sp, 0) for sp in spans)
        bias = _bias_from_rel(off + in_tile, lo, hi, tab_ref, h)
        o_ref[t] = bias if scale == 1.0 else bias * scale


def _bias_tiles(rel_bias, n_tiles, rows, cols, base, tile_step, row_step, col_step, scale=1.0):
    return pl.pallas_call(
        functools.partial(_bias_tiles_kernel, base=base, tile_step=tile_step, row_step=row_step,
                          col_step=col_step, scale=scale),
        grid=(H_ATT,),
        in_specs=[pl.BlockSpec(memory_space=pltpu.SMEM)],
        out_specs=pl.BlockSpec((None, n_tiles, rows, cols), lambda h: (h, 0, 0, 0)),
        out_shape=jax.ShapeDtypeStruct((H_ATT, n_tiles, rows, cols), F32),
        compiler_params=_cparams("parallel"),
        name="rel_bias_tiles",
    )(rel_bias)


def _block_mean_kernel(k_ref, o_ref):
    o_ref[...] = jnp.mean(k_ref[...], axis=0, keepdims=True)


def _block_mean(proj):
    t = proj.shape[0]
    nb = t // MOBA_BLOCK
    out = pl.pallas_call(
        _block_mean_kernel,
        grid=(nb,),
        in_specs=[pl.BlockSpec((MOBA_BLOCK, ATT_W), lambda n: (n, COLK_AK))],
        out_specs=pl.BlockSpec((None, 1, ATT_W), lambda n: (n, 0, 0)),
        out_shape=jax.ShapeDtypeStruct((nb, 1, ATT_W), F32),
        compiler_params=_cparams("parallel"),
        name="moba_block_mean",
    )(proj)
    return out.reshape(nb, ATT_W)


def _top_blocks(s, n_valid, axis):
    n = float(s.shape[axis])
    idx = lax.broadcasted_iota(jnp.int32, s.shape, axis).astype(F32)
    s = jnp.where(idx < n_valid, s, NEG_INF)
    sel = jnp.zeros(s.shape, F32)
    for _ in range(MOBA_TOPK):
        m = jnp.max(s, axis=axis, keepdims=True)
        first = jnp.min(jnp.where(s == m, idx, n), axis=axis, keepdims=True)
        pick = jnp.logical_and(idx == first, m > NEG_INF)
        sel = jnp.where(pick, 1.0, sel)
        s = jnp.where(pick, NEG_INF, s)
    return sel


def _moba_seq_kernel(q_ref, k_ref, v_ref, km_ref, bias_ref, o_ref, kb_s, vt_s, pen_s, penf_s, acc_s, s_s, *,
                     n_far):
    i = pl.program_id(1)
    nblk = vt_s.shape[0]

    @pl.when(i == 0)
    def _():
        kb_s[...] = k_ref[...].astype(BF16)

        def xpose(j, carry):
            r0 = pl.multiple_of(j * MOBA_BLOCK, MOBA_BLOCK)
            vt_s[j] = v_ref[pl.ds(r0, MOBA_BLOCK), :].T.astype(BF16)
            return carry

        lax.fori_loop(0, nblk, xpose, 0)

    tq = q_ref.shape[0]
    qt = q_ref[...].T
    qtb = (qt * (ATT_SCALE * LOG2E)).astype(BF16)
    scores = jnp.dot(km_ref[...], qt * ATT_SCALE, precision=lax.Precision.HIGHEST,
                     preferred_element_type=F32)
    blk_id = lax.broadcasted_iota(jnp.int32, scores.shape, 0)
    keep = jnp.logical_or(_top_blocks(scores, i, axis=0) > 0.5, blk_id == i)
    pen = jnp.where(keep, 0.0, NEG_INF)
    far_bias = bias_ref[n_far, 0:1, :]
    pen_s[0:nblk, :] = pen
    penf_s[0:nblk, :] = pen + far_bias
    pen_s[nblk:nblk + 1, :] = jnp.full((1, tq), NEG_INF, F32)
    penf_s[nblk:nblk + 1, :] = jnp.full((1, tq), NEG_INF, F32)

    def group(g, m_old, l_old, first):
        blocks = []
        mg = m_old
        for u in range(MOBA_GROUP):
            pos = g * MOBA_GROUP + u
            j = jnp.maximum(i - pos, 0)
            pen_row = jnp.where(pos <= i, j, nblk)
            c0 = pl.multiple_of(j * MOBA_BLOCK, MOBA_BLOCK)
            s = _dot(kb_s[pl.ds(c0, MOBA_BLOCK), :], qtb)
            if first and u < n_far:
                s = s + bias_ref[u] + pen_s[pl.ds(pen_row, 1), :]
            else:
                s = s + penf_s[pl.ds(pen_row, 1), :]
            s_s[u] = s
            smax = jnp.max(s, axis=0, keepdims=True)
            mg = smax if mg is None else jnp.maximum(mg, smax)
            blocks.append(j)
        l_new = None if first else jnp.exp2(m_old - mg) * l_old
        pv = None
        for u, j in enumerate(blocks):
            p = jnp.exp2(s_s[u] - mg)
            psum = jnp.sum(p, axis=0, keepdims=True)
            l_new = psum if l_new is None else l_new + psum
            d = _dot(vt_s[j], p.astype(BF16))
            pv = d if pv is None else pv + d
        if first:
            acc_s[...] = pv
        else:
            acc_s[...] = jnp.exp2(m_old - mg) * acc_s[...] + pv
        return mg, l_new

    m, l = group(0, None, None, True)
    m, l = lax.fori_loop(1, (i + MOBA_GROUP) // MOBA_GROUP, lambda g, c: group(g, c[0], c[1], False), (m, l))
    o_ref[...] = (acc_s[...] / l).T.astype(o_ref.dtype)


def _moba_seq(proj, rel_bias):
    t = proj.shape[0]
    nq = t // MOBA_BLOCK
    kmean = _block_mean(proj)
    n_far = -(-(BUCKET_THRESHOLDS[-1] + MOBA_BLOCK) // MOBA_BLOCK)
    assert MOBA_GROUP >= n_far and nq % MOBA_GROUP == 0
    bias = _bias_tiles(rel_bias, n_far + 1, MOBA_BLOCK, MOBA_BLOCK, 0, MOBA_BLOCK, -1, 1, scale=LOG2E)
    return pl.pallas_call(
        functools.partial(_moba_seq_kernel, n_far=n_far),
        grid=(H_ATT, nq),
        in_specs=[pl.BlockSpec((MOBA_BLOCK, ATT_DH), lambda h, i: (i, COL_AQ + h)),
                  pl.BlockSpec((t, ATT_DH), lambda h, i: (0, COL_AK + h)),
                  pl.BlockSpec((t, ATT_DH), lambda h, i: (0, COL_AV + h)),
                  pl.BlockSpec((nq, ATT_DH), lambda h, i: (0, h)),
                  pl.BlockSpec((None, n_far + 1, MOBA_BLOCK, MOBA_BLOCK), lambda h, i: (h, 0, 0, 0))],
        out_specs=pl.BlockSpec((MOBA_BLOCK, ATT_DH), lambda h, i: (i, h)),
        out_shape=jax.ShapeDtypeStruct((t, ATT_W), BF16),
        scratch_shapes=[pltpu.VMEM((t, ATT_DH), BF16), pltpu.VMEM((nq, ATT_DH, MOBA_BLOCK), BF16),
                        pltpu.VMEM((nq + 8, MOBA_BLOCK), F32), pltpu.VMEM((nq + 8, MOBA_BLOCK), F32),
                        pltpu.VMEM((ATT_DH, MOBA_BLOCK), F32),
                        pltpu.VMEM((MOBA_GROUP, MOBA_BLOCK, MOBA_BLOCK), F32)],
        compiler_params=_cparams("parallel", "arbitrary"),
        name="moba_seq",
    )(proj, proj, proj, kmean, bias)


SELECT_PAGES_PER_STEP = 8
MOBA_STEP_SLOTS = 4


def _moba_select_kernel(pt_ref, *refs, pages_per_block):
    n_pg = SELECT_PAGES_PER_STEP
    page_refs = refs[:n_pg]
    q_ref, sel_ref, km_s = refs[n_pg:]
    n = pl.program_id(1)
    page_rows = page_refs[0].shape[0]
    blocks_per_step = n_pg // pages_per_block
    for blk in range(blocks_per_step):
        tot = page_refs[blk * pages_per_block][...].sum(axis=0)
        for pg in range(1, pages_per_block):
            tot = tot + page_refs[blk * pages_per_block + pg][...].sum(axis=0)
        km_s[n * blocks_per_step + blk] = tot / (pages_per_block * page_rows)

    @pl.when(n == pl.num_programs(1) - 1)
    def _():
        prod = km_s[...] * (q_ref[...] * ATT_SCALE)[None]
        s = jnp.sum(prod, axis=-1, keepdims=True)
        nb = s.shape[0]
        mask = _top_blocks(s, nb, axis=0)
        idx = lax.broadcasted_iota(jnp.int32, s.shape, 0).astype(F32)
        for r in range(MOBA_TOPK):
            first = jnp.min(jnp.where(mask > 0.5, idx, float(nb)), axis=0)
            sel_ref[r] = first.astype(jnp.int32)
            mask = jnp.where(idx == first[None], 0.0, mask)


def _moba_select(cache_k4, page_table, q_heads):
    nb_seq, n_pages = page_table.shape
    page_rows = cache_k4.shape[1]
    ppb = MOBA_BLOCK // page_rows
    n_pg = SELECT_PAGES_PER_STEP
    assert ppb * page_rows == MOBA_BLOCK and n_pg % ppb == 0 and n_pages % n_pg == 0
    n_blocks = n_pages // ppb
    assert n_blocks >= MOBA_TOPK
    kspec = lambda p: pl.BlockSpec((None, page_rows, H_ATT, ATT_DH),
                                   lambda b, n, pt, p=p: (pt[b * n_pages + n_pg * n + p], 0, 0, 0))
    return pl.pallas_call(
        functools.partial(_moba_select_kernel, pages_per_block=ppb),
        grid_spec=pltpu.PrefetchScalarGridSpec(
            num_scalar_prefetch=1,
            grid=(nb_seq, n_pages // n_pg),
            in_specs=[kspec(p) for p in range(n_pg)]
                     + [pl.BlockSpec((None, H_ATT, ATT_DH), lambda b, n, pt: (b, 0, 0))],
            out_specs=pl.BlockSpec((None, MOBA_TOPK, H_ATT, 1), lambda b, n, pt: (b, 0, 0, 0)),
            scratch_shapes=[pltpu.VMEM((n_blocks, H_ATT, ATT_DH), F32)]),
        out_shape=jax.ShapeDtypeStruct((nb_seq, MOBA_TOPK, H_ATT, 1), jnp.int32),
        compiler_params=_cparams("parallel", "arbitrary"),
        name="moba_select",
    )(page_table.reshape(-1), *([cache_k4] * n_pg), q_heads)


def _moba_step_kernel(pt_ref, sel_ref, q_ref, kn_ref, vn_ref, b0_ref, b1_ref, b2_ref, bown_ref, ck_hbm, cv_hbm,
                      o_ref, kbuf, vbuf, sem, *, n_pages):
    n_pg = MOBA_TOPK * 2
    b_refs = (b0_ref, b1_ref, b2_ref)
    n_slots = kbuf.shape[0]
    page_rows = kbuf.shape[2]
    b = pl.program_id(0)
    h = pl.program_id(1)
    n_heads = pl.num_programs(1)
    n_steps = pl.num_programs(0) * n_heads
    step = b * n_heads + h
    slot = step % n_slots

    def page_copies(bb, hh, sl):
        cps = []
        for r in range(MOBA_TOPK):
            blk = sel_ref[(bb * MOBA_TOPK + r) * n_heads + hh]
            for half in range(2):
                page = pt_ref[bb * n_pages + 2 * blk + half]
                idx = 2 * r + half
                cps.append(pltpu.make_async_copy(ck_hbm.at[page, :, hh, :], kbuf.at[sl, idx], sem.at[sl, idx]))
                cps.append(pltpu.make_async_copy(cv_hbm.at[page, :, hh, :], vbuf.at[sl, idx], sem.at[sl, n_pg + idx]))
        return cps

    def start_step(st):
        for cp in page_copies(st // n_heads, st % n_heads, st % n_slots):
            cp.start()

    @pl.when(step == 0)
    def _():
        for ahead in range(n_slots - 1):
            @pl.when(ahead < n_steps)
            def _():
                start_step(jnp.int32(ahead))

    @pl.when(step + n_slots - 1 < n_steps)
    def _():
        start_step(step + n_slots - 1)

    for cp in page_copies(b, h, slot):
        cp.wait()

    qs = q_ref[...] * ATT_SCALE
    qb = qs.astype(BF16)
    logits = []
    for r in range(MOBA_TOPK):
        bias = b_refs[r][...]
        for half in range(2):
            kk = kbuf[slot, 2 * r + half].astype(BF16)
            logits.append(_dot_nt(qb, kk) + bias[:, half * page_rows:(half + 1) * page_rows])
    l_own = jnp.sum(qs * kn_ref[...], axis=-1, keepdims=True) + bown_ref[:, 0:1]
    m = l_own
    for lg in logits:
        m = jnp.maximum(m, jnp.max(lg, axis=-1, keepdims=True))
    p_own = jnp.exp(l_own - m)
    den = p_own
    acc = p_own * vn_ref[...]
    for idx, lg in enumerate(logits):
        p = jnp.exp(lg - m)
        den = den + jnp.sum(p, axis=-1, keepdims=True)
        acc = acc + _dot(p.astype(BF16), vbuf[slot, idx].astype(BF16))
    o_ref[...] = (acc / den).astype(o_ref.dtype)


def _moba_step(proj, cache_k4, cache_v4, page_table, rel_bias):
    nb_seq, n_pages = page_table.shape
    page_rows = cache_k4.shape[1]
    ppb = MOBA_BLOCK // page_rows
    assert ppb == 2
    n_blocks = n_pages // ppb
    n_past = n_pages * page_rows
    head_rows = lambda a: a.reshape(nb_seq, H_ATT, 1, ATT_DH)
    aq = proj[:, COL_AQ * 128:COL_AQ * 128 + ATT_W]
    ak = proj[:, COL_AK * 128:COL_AK * 128 + ATT_W]
    av = proj[:, COL_AV * 128:COL_AV * 128 + ATT_W]
    sel = _moba_select(cache_k4, page_table, aq.reshape(nb_seq, H_ATT, ATT_DH))
    bias = _bias_tiles(rel_bias, n_blocks + 1, 1, MOBA_BLOCK, n_past, -MOBA_BLOCK, 0, -1)

    def bias_spec(r):
        return pl.BlockSpec((None, None, 1, MOBA_BLOCK),
                            lambda b, h, pt, sl: (h, sl[(b * MOBA_TOPK + r) * H_ATT + h], 0, 0))

    row_spec = pl.BlockSpec((None, None, 1, ATT_DH), lambda b, h, pt, sl: (b, h, 0, 0))
    hbm = pl.BlockSpec(memory_space=pl.ANY)
    n_pg = 2 * MOBA_TOPK
    out = pl.pallas_call(
        functools.partial(_moba_step_kernel, n_pages=n_pages),
        grid_spec=pltpu.PrefetchScalarGridSpec(
            num_scalar_prefetch=2,
            grid=(nb_seq, H_ATT),
            in_specs=[row_spec, row_spec, row_spec] + [bias_spec(r) for r in range(MOBA_TOPK)]
                     + [pl.BlockSpec((None, None, 1, MOBA_BLOCK), lambda b, h, pt, sl: (h, n_blocks, 0, 0)),
                        hbm, hbm],
            out_specs=row_spec,
            scratch_shapes=[pltpu.VMEM((MOBA_STEP_SLOTS, n_pg, page_rows, ATT_DH), F32),
                            pltpu.VMEM((MOBA_STEP_SLOTS, n_pg, page_rows, ATT_DH), F32),
                            pltpu.SemaphoreType.DMA((MOBA_STEP_SLOTS, 2 * n_pg))]),
        out_shape=jax.ShapeDtypeStruct((nb_seq, H_ATT, 1, ATT_DH), BF16),
        compiler_params=_cparams("arbitrary", "arbitrary"),
        name="moba_step",
    )(page_table.reshape(-1), sel.reshape(-1), head_rows(aq), head_rows(ak), head_rows(av),
      bias, bias, bias, bias, cache_k4, cache_v4)
    return out.reshape(nb_seq, ATT_W)


def _merge_kernel(oh_ref, oa_ref, wh_ref, wa_ref, gh_ref, ga_ref, o_ref):
    br_h = _dot(oh_ref[...], wh_ref[...])
    br_a = _dot(oa_ref[...], wa_ref[...])
    o_ref[...] = (jax.nn.sigmoid(gh_ref[...]) * br_h + jax.nn.sigmoid(ga_ref[...]) * br_a).astype(o_ref.dtype)


def _merge(o_hg, o_att, w_bh, w_bm, proj, tm):
    m = o_hg.shape[0]
    tn = 1024
    return pl.pallas_call(
        _merge_kernel,
        grid=(m // tm, D_MODEL // tn),
        in_specs=[pl.BlockSpec((tm, HG_VW), lambda i, n: (i, 0)),
                  pl.BlockSpec((tm, ATT_W), lambda i, n: (i, 0)),
                  pl.BlockSpec((HG_VW, tn), lambda i, n: (0, n)),
                  pl.BlockSpec((ATT_W, tn), lambda i, n: (0, n)),
                  pl.BlockSpec((tm, tn), lambda i, n: (i, COLK_GHG + n)),
                  pl.BlockSpec((tm, tn), lambda i, n: (i, COLK_GATT + n))],
        out_specs=pl.BlockSpec((tm, tn), lambda i, n: (i, n)),
        out_shape=jax.ShapeDtypeStruct((m, D_MODEL), BF16),
        compiler_params=_cparams("parallel", "arbitrary"),
        name="gated_merge",
    )(o_hg, o_att, w_bh, w_bm, proj, proj)


def _out_proj_kernel(m_ref, w_ref, x_ref, g_ref, gnext_ref, o_ref, h_ref):
    z = _dot(m_ref[...], w_ref[...])
    ms = jnp.mean(z * z, axis=-1, keepdims=True)
    x1 = x_ref[...] + z * lax.rsqrt(ms + EPS) * g_ref[...]
    o_ref[...] = x1
    ms1 = jnp.mean(x1 * x1, axis=-1, keepdims=True)
    h_ref[...] = (x1 * lax.rsqrt(ms1 + EPS) * gnext_ref[...]).astype(h_ref.dtype)


def _out_proj(merged, w_out, x, gain, gain_next, tm):
    m = x.shape[0]
    vec = pl.BlockSpec((1, D_MODEL), lambda i: (0, 0))
    row = pl.BlockSpec((tm, D_MODEL), lambda i: (i, 0))
    return pl.pallas_call(
        _out_proj_kernel,
        grid=(m // tm,),
        in_specs=[row, pl.BlockSpec((D_MODEL, D_MODEL), lambda i: (0, 0)), row, vec, vec],
        out_specs=[row, row],
        out_shape=[jax.ShapeDtypeStruct((m, D_MODEL), F32), jax.ShapeDtypeStruct((m, D_MODEL), BF16)],
        compiler_params=_cparams("parallel"),
        name="out_proj_residual",
    )(merged, w_out, x, gain.reshape(1, D_MODEL), gain_next.reshape(1, D_MODEL))


def _ffn_up_kernel(h_ref, wg_ref, wu_ref, cw_ref, cb_ref, p0_ref, p1_ref, a_ref, g_out_ref, carry_s, *, seq):
    i = pl.program_id(0)
    j = pl.program_id(1)
    tm, tf = a_ref.shape
    h = h_ref[...]
    g = _dot(h, wg_ref[...])
    u = _dot(h, wu_ref[...])
    if seq:
        @pl.when(i == 0)
        def _():
            carry_s[j, 0:1, :] = p0_ref[...]
            carry_s[j, 1:2, :] = p1_ref[...]

        c0 = carry_s[j, 0:1, :]
        c1 = carry_s[j, 1:2, :]
        r1 = pltpu.roll(g, 1, axis=0)
        r2 = pltpu.roll(g, 2, axis=0)
        top = lax.broadcasted_iota(jnp.int32, (8, tf), 0)
        prev1 = jnp.concatenate([jnp.where(top == 0, c1, r1[0:8]), r1[8:]], axis=0)
        prev2 = jnp.concatenate([jnp.where(top == 0, c0, jnp.where(top == 1, c1, r2[0:8])), r2[8:]], axis=0)
        carry_s[j, 0:2, :] = g[tm - 2:tm, :]
        g_out_ref[...] = g[tm - 2:tm, :]
    else:
        prev2 = p0_ref[...]
        prev1 = p1_ref[...]
        g_out_ref[...] = g
    cw = cw_ref[...]
    gc = cb_ref[...] + cw[0:1] * prev2 + cw[1:2] * prev1 + cw[2:3] * g
    a_ref[...] = (jax.nn.gelu(gc, approximate=True) * u).astype(a_ref.dtype)


def _ffn_down_kernel(a_ref, wd_ref, x_ref, g_ref, o_ref, y_s):
    n = pl.program_id(1)
    nn = y_s.shape[0]
    tn = y_s.shape[2]
    y_s[n] = _dot(a_ref[...], wd_ref[...])

    @pl.when(n == nn - 1)
    def _():
        ss = None
        for c in range(nn):
            y = y_s[c]
            part = jnp.sum(y * y, axis=-1, keepdims=True)
            ss = part if ss is None else ss + part
        inv = lax.rsqrt(ss / (nn * tn) + EPS)
        for c in range(nn):
            cols = slice(c * tn, (c + 1) * tn)
            o_ref[:, cols] = x_ref[:, cols] + y_s[c] * inv * g_ref[:, cols]


def _ffn(x, h, w_up, conv_w, conv_b, prev0, prev1, w_down, gpost, *, seq, tm_up, tm_down, tf=512, tn=512):
    m = x.shape[0]
    nf = D_FF // tf
    assert CONV_W == 3 and nf * tf == D_FF
    prow = 1 if seq else tm_up
    pspec = pl.BlockSpec((prow, tf), (lambda i, j: (0, j)) if seq else (lambda i, j: (i, j)))
    if seq:
        gspec = pl.BlockSpec((None, CONV_W - 1, tf), lambda i, j: (i, 0, j))
        gshape = (m // tm_up, CONV_W - 1, D_FF)
    else:
        gspec = pl.BlockSpec((tm_up, tf), lambda i, j: (i, j))
        gshape = (m, D_FF)
    act, g_out = pl.pallas_call(
        functools.partial(_ffn_up_kernel, seq=seq),
        grid=(m // tm_up, nf),
        in_specs=[pl.BlockSpec((tm_up, D_MODEL), lambda i, j: (i, 0)),
                  pl.BlockSpec((D_MODEL, tf), lambda i, j: (0, j)),
                  pl.BlockSpec((D_MODEL, tf), lambda i, j: (0, nf + j)),
                  pl.BlockSpec((CONV_W, tf), lambda i, j: (0, j)),
                  pl.BlockSpec((1, tf), lambda i, j: (0, j)),
                  pspec, pspec],
        out_specs=[pl.BlockSpec((tm_up, tf), lambda i, j: (i, j)), gspec],
        out_shape=[jax.ShapeDtypeStruct((m, D_FF), BF16), jax.ShapeDtypeStruct(gshape, F32)],
        scratch_shapes=[pltpu.VMEM((nf, 8, tf), F32)],
        compiler_params=_cparams("arbitrary", "arbitrary"),
        name="conv_ffn_up",
    )(h, w_up, w_up, conv_w, conv_b.reshape(1, D_FF), prev0, prev1)
    nn = D_MODEL // tn
    y = pl.pallas_call(
        _ffn_down_kernel,
        grid=(m // tm_down, nn),
        in_specs=[pl.BlockSpec((tm_down, D_FF), lambda i, n: (i, 0)),
                  pl.BlockSpec((D_FF, tn), lambda i, n: (0, n)),
                  pl.BlockSpec((tm_down, D_MODEL), lambda i, n: (i, 0)),
                  pl.BlockSpec((1, D_MODEL), lambda i, n: (0, 0))],
        out_specs=pl.BlockSpec((tm_down, D_MODEL), lambda i, n: (i, 0)),
        out_shape=jax.ShapeDtypeStruct((m, D_MODEL), F32),
        scratch_shapes=[pltpu.VMEM((nn, tm_down, tn), F32)],
        compiler_params=_cparams("parallel", "arbitrary"),
        name="ffn_down_residual",
    )(act, w_down, x, gpost.reshape(1, D_MODEL))
    return y, (g_out[-1] if seq else g_out)


def kernel(x_prompt, x_sample, cache_k, cache_v, state_hgrn, state_ffn_conv, page_table, rel_bias, hg_lb,
           norm_mix_pre, norm_mix_post, norm_ffn_pre, norm_ffn_post, w_in, hg_out_norm, w_branch_hgrn,
           w_branch_moba, w_out, w_ffn_up, ffn_conv_w, ffn_conv_b, w_ffn_down):
    nb, t, _ = x_prompt.shape
    db = x_sample.shape[0]
    depth = w_in.shape[0]
    assert nb == 1 and depth == 1 and x_sample.shape[1] == 1
    l = 0
    w_in_b = w_in[l].astype(BF16)
    w_bh = w_branch_hgrn[l].astype(BF16)
    w_bm = w_branch_moba[l].astype(BF16)
    w_o = w_out[l].astype(BF16)
    w_up = w_ffn_up[l].astype(BF16)
    w_dn = w_ffn_down[l].astype(BF16)

    xp = x_prompt.reshape(t, D_MODEL)
    proj = _norm_matmul(xp, norm_mix_pre[l], w_in_b, tm=1024, tn=1024)
    o_hg, s_new = _hgrn_seq(proj, hg_lb, hg_out_norm[l])
    o_att = _moba_seq(proj, rel_bias)
    merged = _merge(o_hg, o_att, w_bh, w_bm, proj, tm=512)
    x1, h2 = _out_proj(merged, w_o, xp, norm_mix_post[l], norm_ffn_pre[l], tm=512)
    zero_row = jnp.zeros((1, D_FF), F32)
    yp, conv_p = _ffn(x1, h2, w_up, ffn_conv_w[l], ffn_conv_b[l], zero_row, zero_row, w_dn,
                      norm_ffn_post[l], seq=True, tm_up=1024, tm_down=512)
    kp = proj[:, COL_AK * 128:COL_AK * 128 + ATT_W].reshape(1, 1, t, H_ATT, ATT_DH)
    vp = proj[:, COL_AV * 128:COL_AV * 128 + ATT_W].reshape(1, 1, t, H_ATT, ATT_DH)

    xs = x_sample.reshape(db, D_MODEL)
    n_pool, page_rows = cache_k.shape[1], cache_k.shape[2]
    ck = cache_k.reshape(depth * n_pool, page_rows, H_ATT, ATT_DH)
    cv = cache_v.reshape(depth * n_pool, page_rows, H_ATT, ATT_DH)
    projs = _norm_matmul(xs, norm_mix_pre[l], w_in_b, tm=db, tn=1024)
    o_hg_s, s_new_s = _hgrn_step(projs, state_hgrn[l], hg_lb, hg_out_norm[l])
    o_att_s = _moba_step(projs, ck, cv, page_table, rel_bias)
    merged_s = _merge(o_hg_s.reshape(db, HG_VW), o_att_s, w_bh, w_bm, projs, tm=db)
    x1s, h2s = _out_proj(merged_s, w_o, xs, norm_mix_post[l], norm_ffn_pre[l], tm=db)
    buf = state_ffn_conv[l]
    ys, g_s = _ffn(x1s, h2s, w_up, ffn_conv_w[l], ffn_conv_b[l], buf[:, 0], buf[:, 1], w_dn,
                   norm_ffn_post[l], seq=False, tm_up=db, tm_down=db)
    ks = projs[:, COL_AK * 128:COL_AK * 128 + ATT_W].reshape(1, db, 1, H_ATT, ATT_DH)
    vs = projs[:, COL_AV * 128:COL_AV * 128 + ATT_W].reshape(1, db, 1, H_ATT, ATT_DH)
    conv_s = jnp.stack([buf[:, 1], g_s], axis=1)

    return (yp.reshape(1, t, D_MODEL), ys.reshape(db, 1, D_MODEL), kp, vp,
            s_new.reshape(1, 1, H_HG, HG_DK, HG_DV), conv_p.reshape(1, 1, CONV_W - 1, D_FF),
            ks, vs, s_new_s.reshape(1, db, H_HG, HG_DK, HG_DV), conv_s.reshape(1, db, CONV_W - 1, D_FF))
```

```python
import functools
import math

import numpy as np
import jax
import jax.numpy as jnp
from jax import lax
from jax.experimental import pallas as pl
from jax.experimental.pallas import tpu as pltpu

F32 = jnp.float32
BF16 = jnp.bfloat16

D_MODEL = 2048
H_HG = 8
HG_DK = 128
HG_DV = 128
HG_W = H_HG * HG_DK
HG_VW = H_HG * HG_DV
H_ATT = 8
ATT_DH = 128
ATT_W = H_ATT * ATT_DH
MOBA_BLOCK = 256
MOBA_TOPK = 3
N_BUCKETS = 32
MAX_EXACT = N_BUCKETS // 2
REL_MAX_DIST = 1024
D_FF = 5632
CONV_W = 3
EPS = 1e-6
N_IN = 2 * HG_W + 2 * HG_VW + 3 * ATT_W + 2 * D_MODEL
ATT_SCALE = ATT_DH ** -0.5
LOG2E = math.log2(math.e)

COL_HQ, COL_HF, COL_HI, COL_OG = 0, 8, 16, 24
COL_AQ, COL_AK, COL_AV = 32, 40, 48
COLK_AK, COLK_GHG, COLK_GATT = 5, 7, 9

HG_CHUNK = 128
HG_SUB = 8
HG_HEADS_PER_STEP = 2
MOBA_GROUP = 8
VMEM_LIMIT = 56 * 1024 * 1024
NEG_INF = float("-inf")


def _bucket_thresholds():
    n = np.arange(MAX_EXACT, 4 * REL_MAX_DIST, dtype=np.float64)
    large = MAX_EXACT + (np.log(n / MAX_EXACT) / math.log(REL_MAX_DIST / MAX_EXACT)
                         * (N_BUCKETS - MAX_EXACT)).astype(np.int64)
    large = np.minimum(large, N_BUCKETS - 1)
    return [int(n[np.argmax(large >= b)]) for b in range(MAX_EXACT + 1, N_BUCKETS)]


BUCKET_THRESHOLDS = _bucket_thresholds()


def _cparams(*sem):
    return pltpu.CompilerParams(dimension_semantics=sem, vmem_limit_bytes=VMEM_LIMIT)


def _silu(x):
    return x * jax.nn.sigmoid(x)


def _dot(a, b):
    return jnp.dot(a, b, preferred_element_type=F32)


def _dot_nt(a, b, precision=None):
    return lax.dot_general(a, b, (((1,), (1,)), ((), ())), precision=precision,
                           preferred_element_type=F32)


def _norm_matmul_kernel(x_ref, g_ref, w_ref, o_ref, h_ref):
    @pl.when(pl.program_id(1) == 0)
    def _():
        x = x_ref[...]
        ms = jnp.mean(x * x, axis=-1, keepdims=True)
        h_ref[...] = (x * lax.rsqrt(ms + EPS) * g_ref[...]).astype(BF16)

    o_ref[...] = _dot(h_ref[...], w_ref[...])


def _norm_matmul(x, gain, w_bf16, tm, tn):
    m, k = x.shape
    n = w_bf16.shape[1]
    return pl.pallas_call(
        _norm_matmul_kernel,
        grid=(m // tm, n // tn),
        in_specs=[pl.BlockSpec((tm, k), lambda i, j: (i, 0)),
                  pl.BlockSpec((1, k), lambda i, j: (0, 0)),
                  pl.BlockSpec((k, tn), lambda i, j: (0, j))],
        out_specs=pl.BlockSpec((tm, tn), lambda i, j: (i, j)),
        out_shape=jax.ShapeDtypeStruct((m, n), F32),
        scratch_shapes=[pltpu.VMEM((tm, k), BF16)],
        compiler_params=_cparams("parallel", "arbitrary"),
        name="norm_in_proj",
    )(x, gain.reshape(1, k), w_bf16)


def _forget_lower_bound(lbraw):
    e = jnp.exp(lbraw - jnp.max(lbraw, axis=0, keepdims=True))
    return e[0:1] / jnp.sum(e, axis=0, keepdims=True)


def _hgrn_seq_kernel(hq_ref, hf_ref, hi_ref, og_ref, lbraw_ref, gn_ref, o_ref, s_out_ref,
                     st_ref, q_s, k_s, b_s, v_s, acc_s, *, chunk, heads):
    c = pl.program_id(1)

    @pl.when(c == 0)
    def _():
        st_ref[...] = jnp.zeros_like(st_ref)

    row = lax.broadcasted_iota(jnp.int32, (chunk, HG_DK), 0)
    rr = lax.broadcasted_iota(jnp.int32, (chunk, chunk), 0)
    cc = lax.broadcasted_iota(jnp.int32, (chunk, chunk), 1)
    sub = lax.broadcasted_iota(jnp.int32, (HG_SUB, HG_DK), 0)

    for hh in range(heads):
        cols = slice(hh * 128, (hh + 1) * 128)
        lb = _forget_lower_bound(lbraw_ref[:, cols])
        q = _silu(hq_ref[:, cols])
        f = lb + (1.0 - lb) * jax.nn.sigmoid(hf_ref[:, cols])
        k = 1.0 - f
        v = hi_ref[:, cols]

        b = jnp.log(f)
        shift = 1
        while shift < chunk:
            b = b + jnp.where(row >= shift, pltpu.roll(b, shift, axis=0), 0.0)
            shift *= 2

        q_s[hh] = q
        k_s[hh] = k
        b_s[hh] = b
        v_s[hh] = v

        st = st_ref[hh]
        o = _dot_nt((q * jnp.exp(b)).astype(BF16), st.astype(BF16))

        a = jnp.zeros((chunk, chunk), F32)
        hs = HG_SUB
        while hs < chunk:
            blk = 2 * hs
            ref_rows = jnp.concatenate(
                [jnp.broadcast_to(b_s[hh, m0 + hs - 1:m0 + hs, :], (blk, HG_DK)) for m0 in range(0, chunk, blk)],
                axis=0)
            second = (row & (blk - 1)) >= hs
            qd = jnp.where(second, q * jnp.exp(jnp.minimum(b - ref_rows, 0.0)), 0.0)
            kd = jnp.where(second, 0.0, k * jnp.exp(jnp.minimum(ref_rows - b, 0.0)))
            a_l = _dot_nt(qd.astype(BF16), kd.astype(BF16))
            sh = int(math.log2(blk))
            a = a + jnp.where((rr >> sh) == (cc >> sh), a_l, 0.0)
            hs = blk
        o = o + _dot(a.astype(BF16), v.astype(BF16))

        for r0 in range(0, chunk, HG_SUB):
            qi = q_s[hh, r0:r0 + HG_SUB, :]
            bi = b_s[hh, r0:r0 + HG_SUB, :]
            acc = jnp.zeros((HG_SUB, HG_DV), F32)
            for s in range(HG_SUB):
                ks = k_s[hh, r0 + s:r0 + s + 1, :]
                bs = b_s[hh, r0 + s:r0 + s + 1, :]
                vs = v_s[hh, r0 + s:r0 + s + 1, :]
                e = jnp.where(sub >= s, jnp.exp(jnp.minimum(bi - bs, 0.0)), 0.0)
                w = jnp.sum(qi * ks * e, axis=-1, keepdims=True)
                acc = acc + w * vs
            acc_s[hh, r0:r0 + HG_SUB, :] = acc
        o = o + acc_s[hh]

        b_last = b_s[hh, chunk - 1:chunk, :]
        kd = k * jnp.exp(b_last - b)
        st_new = st * jnp.exp(b_last) + _dot(v.T.astype(BF16), kd.astype(BF16))
        st_ref[hh] = st_new

        ms = jnp.mean(o * o, axis=-1, keepdims=True)
        o_ref[:, cols] = (o * lax.rsqrt(ms + EPS) * gn_ref[...] * _silu(og_ref[:, cols])).astype(o_ref.dtype)

    @pl.when(c == pl.num_programs(1) - 1)
    def _():
        for hh in range(heads):
            s_out_ref[hh] = st_ref[hh].T


def _hgrn_seq(proj, hg_lb, gn, chunk=HG_CHUNK, heads=HG_HEADS_PER_STEP):
    t = proj.shape[0]
    w = 128 * heads
    blk = lambda off: pl.BlockSpec((chunk, w), lambda h, c, off=off: (c, off // heads + h))
    return pl.pallas_call(
        functools.partial(_hgrn_seq_kernel, chunk=chunk, heads=heads),
        grid=(H_HG // heads, t // chunk),
        in_specs=[blk(COL_HQ), blk(COL_HF), blk(COL_HI), blk(COL_OG),
                  pl.BlockSpec((hg_lb.shape[0], w), lambda h, c: (0, h)),
                  pl.BlockSpec((1, HG_DV), lambda h, c: (0, 0))],
        out_specs=[pl.BlockSpec((chunk, w), lambda h, c: (c, h)),
                   pl.BlockSpec((heads, HG_DK, HG_DV), lambda h, c: (h, 0, 0))],
        out_shape=[jax.ShapeDtypeStruct((t, HG_VW), BF16),
                   jax.ShapeDtypeStruct((H_HG, HG_DK, HG_DV), F32)],
        scratch_shapes=[pltpu.VMEM((heads, HG_DV, HG_DK), F32)] + [pltpu.VMEM((heads, chunk, 128), F32)] * 5,
        compiler_params=_cparams("parallel", "arbitrary"),
        name="hgrn_seq",
    )(proj, proj, proj, proj, hg_lb, gn.reshape(1, HG_DV))


def _hgrn_step_kernel(hqc_ref, hfc_ref, hi_ref, og_ref, lbc_ref, gn_ref, s_ref, o_ref, s_out_ref):
    lbraw = lbc_ref[...]
    e = jnp.exp(lbraw - jnp.max(lbraw, axis=0, keepdims=True))
    lb = e[0] / jnp.sum(e, axis=0)
    q = _silu(hqc_ref[...])
    f = lb + (1.0 - lb) * jax.nn.sigmoid(hfc_ref[...])
    k = 1.0 - f
    v = hi_ref[...]
    s_new = f * s_ref[...] + k * v
    s_out_ref[...] = s_new
    o = jnp.sum(s_new * q, axis=1, keepdims=True)
    ms = jnp.mean(o * o, axis=-1, keepdims=True)
    o_ref[...] = (o * lax.rsqrt(ms + EPS) * gn_ref[...] * _silu(og_ref[...])).astype(o_ref.dtype)


def _hgrn_step(proj, state, hg_lb, gn):
    nb = proj.shape[0]
    col = lambda a: a.reshape(nb, H_HG, HG_DK, 1)
    rowv = lambda a: a.reshape(nb, H_HG, 1, HG_DV)
    hq = col(proj[:, 0:HG_W])
    hf = col(proj[:, HG_W:2 * HG_W])
    hi = rowv(proj[:, 2 * HG_W:2 * HG_W + HG_VW])
    og = rowv(proj[:, 2 * HG_W + HG_VW:2 * HG_W + 2 * HG_VW])
    nl = hg_lb.shape[0]
    cspec = pl.BlockSpec((None, H_HG, HG_DK, 1), lambda b: (b, 0, 0, 0))
    rspec = pl.BlockSpec((None, H_HG, 1, HG_DV), lambda b: (b, 0, 0, 0))
    sspec = pl.BlockSpec((None, H_HG, HG_DK, HG_DV), lambda b: (b, 0, 0, 0))
    return pl.pallas_call(
        _hgrn_step_kernel,
        grid=(nb,),
        in_specs=[cspec, cspec, rspec, rspec,
                  pl.BlockSpec((nl, H_HG, HG_DK, 1), lambda b: (0, 0, 0, 0)),
                  pl.BlockSpec((1, 1, HG_DV), lambda b: (0, 0, 0)),
                  sspec],
        out_specs=[rspec, sspec],
        out_shape=[jax.ShapeDtypeStruct((nb, H_HG, 1, HG_DV), BF16),
                   jax.ShapeDtypeStruct((nb, H_HG, HG_DK, HG_DV), F32)],
        compiler_params=_cparams("parallel"),
        name="hgrn_step",
    )(hq, hf, hi, og, hg_lb.reshape(nl, H_HG, HG_DK, 1), gn.reshape(1, 1, HG_DV), state)


BUCKET_STARTS = list(range(1, MAX_EXACT + 1)) + BUCKET_THRESHOLDS


def _bias_from_rel(rel, lo, hi, tab_ref, h):
    first = sum(1 for start in BUCKET_STARTS if start <= lo)
    out = jnp.full(rel.shape, tab_ref[first, h], F32)
    for bkt in range(first + 1, N_BUCKETS):
        if BUCKET_STARTS[bkt - 1] <= hi:
            out = jnp.where(rel >= BUCKET_STARTS[bkt - 1], tab_ref[bkt, h], out)
    return jnp.where(rel >= 0, out, NEG_INF) if lo < 0 else out


def _bias_tiles_kernel(tab_ref, o_ref, *, base, tile_step, row_step, col_step, scale):
    h = pl.program_id(0)
    n_tiles, rows, cols = o_ref.shape
    r = lax.broadcasted_iota(jnp.int32, (rows, cols), 0)
    c = lax.broadcasted_iota(jnp.int32, (rows, cols), 1)
    in_tile = r * row_step + c * col_step
    spans = [row_step * (rows - 1), col_step * (cols - 1)]
    for t in range(n_tiles):
        off = base + t * tile_step
        lo = off + sum(min(sp, 0) for sp in spans)
        hi = off + sum(max(sp, 0) for sp in spans)
        bias = _bias_from_rel(off + in_tile, lo, hi, tab_ref, h)
        o_ref[t] = bias if scale == 1.0 else bias * scale


def _bias_tiles(rel_bias, n_tiles, rows, cols, base, tile_step, row_step, col_step, scale=1.0):
    return pl.pallas_call(
        functools.partial(_bias_tiles_kernel, base=base, tile_step=tile_step, row_step=row_step,
                          col_step=col_step, scale=scale),
        grid=(H_ATT,),
        in_specs=[pl.BlockSpec(memory_space=pltpu.SMEM)],
        out_specs=pl.BlockSpec((None, n_tiles, rows, cols), lambda h: (h, 0, 0, 0)),
        out_shape=jax.ShapeDtypeStruct((H_ATT, n_tiles, rows, cols), F32),
        compiler_params=_cparams("parallel"),
        name="rel_bias_tiles",
    )(rel_bias)


def _block_mean_kernel(k_ref, o_ref):
    o_ref[...] = jnp.mean(k_ref[...], axis=0, keepdims=True)


def _block_mean(proj):
    t = proj.shape[0]
    nb = t // MOBA_BLOCK
    out = pl.pallas_call(
        _block_mean_kernel,
        grid=(nb,),
        in_specs=[pl.BlockSpec((MOBA_BLOCK, ATT_W), lambda n: (n, COLK_AK))],
        out_specs=pl.BlockSpec((None, 1, ATT_W), lambda n: (n, 0, 0)),
        out_shape=jax.ShapeDtypeStruct((nb, 1, ATT_W), F32),
        compiler_params=_cparams("parallel"),
        name="moba_block_mean",
    )(proj)
    return out.reshape(nb, ATT_W)


def _top_blocks(s, n_valid, axis):
    n = float(s.shape[axis])
    idx = lax.broadcasted_iota(jnp.int32, s.shape, axis).astype(F32)
    s = jnp.where(idx < n_valid, s, NEG_INF)
    sel = jnp.zeros(s.shape, F32)
    for _ in range(MOBA_TOPK):
        m = jnp.max(s, axis=axis, keepdims=True)
        first = jnp.min(jnp.where(s == m, idx, n), axis=axis, keepdims=True)
        pick = jnp.logical_and(idx == first, m > NEG_INF)
        sel = jnp.where(pick, 1.0, sel)
        s = jnp.where(pick, NEG_INF, s)
    return sel


def _moba_pen_kernel(tab_ref, q_ref, km_ref, pen_ref, penfar_ref):
    h = pl.program_id(0)
    nblk = km_ref.shape[0]
    tq = q_ref.shape[0]
    scores = _dot_nt(km_ref[...], q_ref[...] * ATT_SCALE, precision=lax.Precision.HIGHEST)
    pos = pl.program_id(1) * tq + lax.broadcasted_iota(jnp.int32, (1, tq), 1)
    own = pos // MOBA_BLOCK
    blk_id = lax.broadcasted_iota(jnp.int32, scores.shape, 0)
    chosen = _top_blocks(scores, own, axis=0) > 0.5
    pen_ref[0:nblk, :] = jnp.where(jnp.logical_or(chosen, blk_id == own), 0.0, NEG_INF)
    pen_ref[nblk:, :] = jnp.full((pen_ref.shape[0] - nblk, tq), NEG_INF, F32)
    far_bias = tab_ref[N_BUCKETS - 1, h] * LOG2E
    penfar_ref[...] = jnp.where(jnp.logical_and(chosen, blk_id <= own - MOBA_GROUP), far_bias, NEG_INF)


def _moba_pen(proj, kmean, rel_bias, tq=1024):
    t = proj.shape[0]
    nblk = t // MOBA_BLOCK
    return pl.pallas_call(
        _moba_pen_kernel,
        grid=(H_ATT, t // tq),
        in_specs=[pl.BlockSpec(memory_space=pltpu.SMEM),
                  pl.BlockSpec((tq, ATT_DH), lambda h, n: (n, COL_AQ + h)),
                  pl.BlockSpec((nblk, ATT_DH), lambda h, n: (0, h))],
        out_specs=[pl.BlockSpec((None, nblk + 8, tq), lambda h, n: (h, 0, n)),
                   pl.BlockSpec((None, nblk, tq), lambda h, n: (h, 0, n))],
        out_shape=[jax.ShapeDtypeStruct((H_ATT, nblk + 8, t), F32),
                   jax.ShapeDtypeStruct((H_ATT, nblk, t), F32)],
        compiler_params=_cparams("parallel", "parallel"),
        name="moba_block_masks",
    )(rel_bias, proj, kmean)


def _moba_seq_kernel(q_ref, k_ref, v_ref, bias_ref, pen_ref, penfar_ref, o_ref, kb_s, vt_s, acc_s, s_s, *, n_far):
    i = pl.program_id(1)
    nblk = vt_s.shape[0]

    @pl.when(i == 0)
    def _():
        kb_s[...] = k_ref[...].astype(BF16)

        def xpose(j, carry):
            r0 = pl.multiple_of(j * MOBA_BLOCK, MOBA_BLOCK)
            vt_s[j] = v_ref[pl.ds(r0, MOBA_BLOCK), :].T.astype(BF16)
            return carry

        lax.fori_loop(0, nblk, xpose, 0)

    tq = q_ref.shape[0]
    grp = MOBA_GROUP
    qb = (q_ref[...] * (ATT_SCALE * LOG2E)).astype(BF16)
    acc_s[...] = jnp.zeros_like(acc_s)

    def logits(j):
        c0 = pl.multiple_of(j * MOBA_BLOCK, MOBA_BLOCK)
        return _dot_nt(kb_s[pl.ds(c0, MOBA_BLOCK), :], qb)

    def keep_max(smax, s):
        cur = jnp.max(s, axis=0, keepdims=True)
        return cur if smax is None else jnp.maximum(smax, cur)

    smax = None
    for u in range(grp):
        j = jnp.maximum(i - u, 0)
        pen_row = jnp.where(u <= i, j, nblk)
        s = logits(j) + bias_ref[min(u, n_far)] + pen_ref[pl.ds(pen_row, 1), :]
        s_s[0, u] = s
        smax = keep_max(smax, s)

    def softmax_update(slot, m_old, l_old, smax, block_of):
        mg = jnp.maximum(m_old, smax)
        alpha = jnp.exp2(m_old - mg)
        l_new = alpha * l_old
        pv = None
        for u in range(grp):
            p = jnp.exp2(s_s[slot, u] - mg)
            l_new = l_new + jnp.sum(p, axis=0, keepdims=True)
            d = _dot(vt_s[block_of(u)], p.astype(BF16))
            pv = d if pv is None else pv + d
        acc_s[...] = alpha * acc_s[...] + pv
        return mg, l_new

    def trip_with_slots(t, carry, slot_prev):
        m_old, l_old, smax_prev = carry
        smax_new = None
        for u in range(grp):
            s = logits(t * grp + u) + penfar_ref[pl.ds(t * grp + u, 1), :]
            s_s[1 - slot_prev, u] = s
            smax_new = keep_max(smax_new, s)
        prev = lambda u: jnp.where(t == 0, jnp.maximum(i - u, 0), (t - 1) * grp + u)
        m_new, l_new = softmax_update(slot_prev, m_old, l_old, smax_prev, prev)
        return m_new, l_new, smax_new

    def trip(t, carry):
        return lax.cond(t % 2 == 0, lambda c: trip_with_slots(t, c, 0), lambda c: trip_with_slots(t, c, 1), carry)

    n_far_groups = i // grp
    init = (jnp.full((1, tq), NEG_INF, F32), jnp.zeros((1, tq), F32), smax)
    m, l, smax = lax.fori_loop(0, n_far_groups, trip, init)
    last = lambda u: jnp.where(n_far_groups == 0, jnp.maximum(i - u, 0), (n_far_groups - 1) * grp + u)
    m, l = softmax_update(n_far_groups % 2, m, l, smax, last)
    o_ref[...] = (acc_s[...] / l).T.astype(o_ref.dtype)


def _moba_seq(proj, rel_bias):
    t = proj.shape[0]
    nq = t // MOBA_BLOCK
    kmean = _block_mean(proj)
    n_far = -(-(BUCKET_THRESHOLDS[-1] + MOBA_BLOCK) // MOBA_BLOCK)
    assert MOBA_GROUP >= n_far and nq % MOBA_GROUP == 0
    bias = _bias_tiles(rel_bias, n_far + 1, MOBA_BLOCK, MOBA_BLOCK, 0, MOBA_BLOCK, -1, 1, scale=LOG2E)
    pen, penfar = _moba_pen(proj, kmean, rel_bias)
    return pl.pallas_call(
        functools.partial(_moba_seq_kernel, n_far=n_far),
        grid=(H_ATT, nq),
        in_specs=[pl.BlockSpec((MOBA_BLOCK, ATT_DH), lambda h, i: (i, COL_AQ + h)),
                  pl.BlockSpec((t, ATT_DH), lambda h, i: (0, COL_AK + h)),
                  pl.BlockSpec((t, ATT_DH), lambda h, i: (0, COL_AV + h)),
                  pl.BlockSpec((None, n_far + 1, MOBA_BLOCK, MOBA_BLOCK), lambda h, i: (h, 0, 0, 0)),
                  pl.BlockSpec((None, nq + 8, MOBA_BLOCK), lambda h, i: (h, 0, i)),
                  pl.BlockSpec((None, nq, MOBA_BLOCK), lambda h, i: (h, 0, i))],
        out_specs=pl.BlockSpec((MOBA_BLOCK, ATT_DH), lambda h, i: (i, h)),
        out_shape=jax.ShapeDtypeStruct((t, ATT_W), BF16),
        scratch_shapes=[pltpu.VMEM((t, ATT_DH), BF16), pltpu.VMEM((nq, ATT_DH, MOBA_BLOCK), BF16),
                        pltpu.VMEM((ATT_DH, MOBA_BLOCK), F32),
                        pltpu.VMEM((2, MOBA_GROUP, MOBA_BLOCK, MOBA_BLOCK), F32)],
        compiler_params=_cparams("parallel", "arbitrary"),
        name="moba_seq",
    )(proj, proj, proj, bias, pen, penfar)


SELECT_PAGES_PER_STEP = 8
MOBA_STEP_SLOTS = 4


def _moba_select_kernel(pt_ref, *refs, pages_per_block):
    n_pg = SELECT_PAGES_PER_STEP
    page_refs = refs[:n_pg]
    q_ref, sel_ref, km_s = refs[n_pg:]
    n = pl.program_id(1)
    page_rows = page_refs[0].shape[0]
    blocks_per_step = n_pg // pages_per_block
    for blk in range(blocks_per_step):
        tot = page_refs[blk * pages_per_block][...].sum(axis=0)
        for pg in range(1, pages_per_block):
            tot = tot + page_refs[blk * pages_per_block + pg][...].sum(axis=0)
        km_s[n * blocks_per_step + blk] = tot / (pages_per_block * page_rows)

    @pl.when(n == pl.num_programs(1) - 1)
    def _():
        prod = km_s[...] * (q_ref[...] * ATT_SCALE)[None]
        s = jnp.sum(prod, axis=-1, keepdims=True)
        nb = s.shape[0]
        mask = _top_blocks(s, nb, axis=0)
        idx = lax.broadcasted_iota(jnp.int32, s.shape, 0).astype(F32)
        for r in range(MOBA_TOPK):
            first = jnp.min(jnp.where(mask > 0.5, idx, float(nb)), axis=0)
            sel_ref[r] = first.astype(jnp.int32)
            mask = jnp.where(idx == first[None], 0.0, mask)


def _moba_select(cache_k4, page_table, q_heads):
    nb_seq, n_pages = page_table.shape
    page_rows = cache_k4.shape[1]
    ppb = MOBA_BLOCK // page_rows
    n_pg = SELECT_PAGES_PER_STEP
    assert ppb * page_rows == MOBA_BLOCK and n_pg % ppb == 0 and n_pages % n_pg == 0
    n_blocks = n_pages // ppb
    assert n_blocks >= MOBA_TOPK
    kspec = lambda p: pl.BlockSpec((None, page_rows, H_ATT, ATT_DH),
                                   lambda b, n, pt, p=p: (pt[b * n_pages + n_pg * n + p], 0, 0, 0))
    return pl.pallas_call(
        functools.partial(_moba_select_kernel, pages_per_block=ppb),
        grid_spec=pltpu.PrefetchScalarGridSpec(
            num_scalar_prefetch=1,
            grid=(nb_seq, n_pages // n_pg),
            in_specs=[kspec(p) for p in range(n_pg)]
                     + [pl.BlockSpec((None, H_ATT, ATT_DH), lambda b, n, pt: (b, 0, 0))],
            out_specs=pl.BlockSpec((None, MOBA_TOPK, H_ATT, 1), lambda b, n, pt: (b, 0, 0, 0)),
            scratch_shapes=[pltpu.VMEM((n_blocks, H_ATT, ATT_DH), F32)]),
        out_shape=jax.ShapeDtypeStruct((nb_seq, MOBA_TOPK, H_ATT, 1), jnp.int32),
        compiler_params=_cparams("parallel", "arbitrary"),
        name="moba_select",
    )(page_table.reshape(-1), *([cache_k4] * n_pg), q_heads)


def _moba_step_kernel(pt_ref, sel_ref, q_ref, kn_ref, vn_ref, b0_ref, b1_ref, b2_ref, bown_ref, ck_hbm, cv_hbm,
                      o_ref, kbuf, vbuf, sem, *, n_pages):
    n_pg = MOBA_TOPK * 2
    b_refs = (b0_ref, b1_ref, b2_ref)
    n_slots = kbuf.shape[0]
    page_rows = kbuf.shape[2]
    b = pl.program_id(0)
    h = pl.program_id(1)
    n_heads = pl.num_programs(1)
    n_steps = pl.num_programs(0) * n_heads
    step = b * n_heads + h
    slot = step % n_slots

    def page_copies(bb, hh, sl):
        cps = []
        for r in range(MOBA_TOPK):
            blk = sel_ref[(bb * MOBA_TOPK + r) * n_heads + hh]
            for half in range(2):
                page = pt_ref[bb * n_pages + 2 * blk + half]
                idx = 2 * r + half
                cps.append(pltpu.make_async_copy(ck_hbm.at[page, :, hh, :], kbuf.at[sl, idx], sem.at[sl, idx]))
                cps.append(pltpu.make_async_copy(cv_hbm.at[page, :, hh, :], vbuf.at[sl, idx], sem.at[sl, n_pg + idx]))
        return cps

    def start_step(st):
        for cp in page_copies(st // n_heads, st % n_heads, st % n_slots):
            cp.start()

    @pl.when(step == 0)
    def _():
        for ahead in range(n_slots - 1):
            @pl.when(ahead < n_steps)
            def _():
                start_step(jnp.int32(ahead))

    @pl.when(step + n_slots - 1 < n_steps)
    def _():
        start_step(step + n_slots - 1)

    for cp in page_copies(b, h, slot):
        cp.wait()

    qs = q_ref[...] * ATT_SCALE
    qb = qs.astype(BF16)
    logits = []
    for r in range(MOBA_TOPK):
        bias = b_refs[r][...]
        for half in range(2):
            kk = kbuf[slot, 2 * r + half].astype(BF16)
            logits.append(_dot_nt(qb, kk) + bias[:, half * page_rows:(half + 1) * page_rows])
    l_own = jnp.sum(qs * kn_ref[...], axis=-1, keepdims=True) + bown_ref[:, 0:1]
    m = l_own
    for lg in logits:
        m = jnp.maximum(m, jnp.max(lg, axis=-1, keepdims=True))
    p_own = jnp.exp(l_own - m)
    den = p_own
    acc = p_own * vn_ref[...]
    for idx, lg in enumerate(logits):
        p = jnp.exp(lg - m)
        den = den + jnp.sum(p, axis=-1, keepdims=True)
        acc = acc + _dot(p.astype(BF16), vbuf[slot, idx].astype(BF16))
    o_ref[...] = (acc / den).astype(o_ref.dtype)


def _moba_step(proj, cache_k4, cache_v4, page_table, rel_bias):
    nb_seq, n_pages = page_table.shape
    page_rows = cache_k4.shape[1]
    ppb = MOBA_BLOCK // page_rows
    assert ppb == 2
    n_blocks = n_pages // ppb
    n_past = n_pages * page_rows
    head_rows = lambda a: a.reshape(nb_seq, H_ATT, 1, ATT_DH)
    aq = proj[:, COL_AQ * 128:COL_AQ * 128 + ATT_W]
    ak = proj[:, COL_AK * 128:COL_AK * 128 + ATT_W]
    av = proj[:, COL_AV * 128:COL_AV * 128 + ATT_W]
    sel = _moba_select(cache_k4, page_table, aq.reshape(nb_seq, H_ATT, ATT_DH))
    bias = _bias_tiles(rel_bias, n_blocks + 1, 1, MOBA_BLOCK, n_past, -MOBA_BLOCK, 0, -1)

    def bias_spec(r):
        return pl.BlockSpec((None, None, 1, MOBA_BLOCK),
                            lambda b, h, pt, sl: (h, sl[(b * MOBA_TOPK + r) * H_ATT + h], 0, 0))

    row_spec = pl.BlockSpec((None, None, 1, ATT_DH), lambda b, h, pt, sl: (b, h, 0, 0))
    hbm = pl.BlockSpec(memory_space=pl.ANY)
    n_pg = 2 * MOBA_TOPK
    out = pl.pallas_call(
        functools.partial(_moba_step_kernel, n_pages=n_pages),
        grid_spec=pltpu.PrefetchScalarGridSpec(
            num_scalar_prefetch=2,
            grid=(nb_seq, H_ATT),
            in_specs=[row_spec, row_spec, row_spec] + [bias_spec(r) for r in range(MOBA_TOPK)]
                     + [pl.BlockSpec((None, None, 1, MOBA_BLOCK), lambda b, h, pt, sl: (h, n_blocks, 0, 0)),
                        hbm, hbm],
            out_specs=row_spec,
            scratch_shapes=[pltpu.VMEM((MOBA_STEP_SLOTS, n_pg, page_rows, ATT_DH), F32),
                            pltpu.VMEM((MOBA_STEP_SLOTS, n_pg, page_rows, ATT_DH), F32),
                            pltpu.SemaphoreType.DMA((MOBA_STEP_SLOTS, 2 * n_pg))]),
        out_shape=jax.ShapeDtypeStruct((nb_seq, H_ATT, 1, ATT_DH), BF16),
        compiler_params=_cparams("arbitrary", "arbitrary"),
        name="moba_step",
    )(page_table.reshape(-1), sel.reshape(-1), head_rows(aq), head_rows(ak), head_rows(av),
      bias, bias, bias, bias, cache_k4, cache_v4)
    return out.reshape(nb_seq, ATT_W)


def _merge_kernel(oh_ref, oa_ref, wh_ref, wa_ref, gh_ref, ga_ref, o_ref):
    br_h = _dot(oh_ref[...], wh_ref[...])
    br_a = _dot(oa_ref[...], wa_ref[...])
    o_ref[...] = (jax.nn.sigmoid(gh_ref[...]) * br_h + jax.nn.sigmoid(ga_ref[...]) * br_a).astype(o_ref.dtype)


def _merge(o_hg, o_att, w_bh, w_bm, proj, tm):
    m = o_hg.shape[0]
    tn = 1024
    return pl.pallas_call(
        _merge_kernel,
        grid=(m // tm, D_MODEL // tn),
        in_specs=[pl.BlockSpec((tm, HG_VW), lambda i, n: (i, 0)),
                  pl.BlockSpec((tm, ATT_W), lambda i, n: (i, 0)),
                  pl.BlockSpec((HG_VW, tn), lambda i, n: (0, n)),
                  pl.BlockSpec((ATT_W, tn), lambda i, n: (0, n)),
                  pl.BlockSpec((tm, tn), lambda i, n: (i, COLK_GHG + n)),
                  pl.BlockSpec((tm, tn), lambda i, n: (i, COLK_GATT + n))],
        out_specs=pl.BlockSpec((tm, tn), lambda i, n: (i, n)),
        out_shape=jax.ShapeDtypeStruct((m, D_MODEL), BF16),
        compiler_params=_cparams("parallel", "arbitrary"),
        name="gated_merge",
    )(o_hg, o_att, w_bh, w_bm, proj, proj)


def _out_proj_kernel(m_ref, w_ref, x_ref, g_ref, gnext_ref, o_ref, h_ref):
    z = _dot(m_ref[...], w_ref[...])
    ms = jnp.mean(z * z, axis=-1, keepdims=True)
    x1 = x_ref[...] + z * lax.rsqrt(ms + EPS) * g_ref[...]
    o_ref[...] = x1
    ms1 = jnp.mean(x1 * x1, axis=-1, keepdims=True)
    h_ref[...] = (x1 * lax.rsqrt(ms1 + EPS) * gnext_ref[...]).astype(h_ref.dtype)


def _out_proj(merged, w_out, x, gain, gain_next, tm):
    m = x.shape[0]
    vec = pl.BlockSpec((1, D_MODEL), lambda i: (0, 0))
    row = pl.BlockSpec((tm, D_MODEL), lambda i: (i, 0))
    return pl.pallas_call(
        _out_proj_kernel,
        grid=(m // tm,),
        in_specs=[row, pl.BlockSpec((D_MODEL, D_MODEL), lambda i: (0, 0)), row, vec, vec],
        out_specs=[row, row],
        out_shape=[jax.ShapeDtypeStruct((m, D_MODEL), F32), jax.ShapeDtypeStruct((m, D_MODEL), BF16)],
        compiler_params=_cparams("parallel"),
        name="out_proj_residual",
    )(merged, w_out, x, gain.reshape(1, D_MODEL), gain_next.reshape(1, D_MODEL))


def _ffn_up_kernel(h_ref, wg_ref, wu_ref, cw_ref, cb_ref, p0_ref, p1_ref, a_ref, g_out_ref, carry_s, *, seq):
    i = pl.program_id(0)
    j = pl.program_id(1)
    tm, tf = a_ref.shape
    h = h_ref[...]
    g = _dot(h, wg_ref[...])
    u = _dot(h, wu_ref[...])
    if seq:
        @pl.when(i == 0)
        def _():
            carry_s[j, 0:1, :] = p0_ref[...]
            carry_s[j, 1:2, :] = p1_ref[...]

        c0 = carry_s[j, 0:1, :]
        c1 = carry_s[j, 1:2, :]
        r1 = pltpu.roll(g, 1, axis=0)
        r2 = pltpu.roll(g, 2, axis=0)
        top = lax.broadcasted_iota(jnp.int32, (8, tf), 0)
        prev1 = jnp.concatenate([jnp.where(top == 0, c1, r1[0:8]), r1[8:]], axis=0)
        prev2 = jnp.concatenate([jnp.where(top == 0, c0, jnp.where(top == 1, c1, r2[0:8])), r2[8:]], axis=0)
        carry_s[j, 0:2, :] = g[tm - 2:tm, :]
        g_out_ref[...] = g[tm - 2:tm, :]
    else:
        prev2 = p0_ref[...]
        prev1 = p1_ref[...]
        g_out_ref[...] = g
    cw = cw_ref[...]
    gc = cb_ref[...] + cw[0:1] * prev2 + cw[1:2] * prev1 + cw[2:3] * g
    a_ref[...] = (jax.nn.gelu(gc, approximate=True) * u).astype(a_ref.dtype)


def _ffn_down_kernel(a_ref, wd_ref, x_ref, g_ref, o_ref, y_s):
    n = pl.program_id(1)
    nn = y_s.shape[0]
    tn = y_s.shape[2]
    y_s[n] = _dot(a_ref[...], wd_ref[...])

    @pl.when(n == nn - 1)
    def _():
        ss = None
        for c in range(nn):
            y = y_s[c]
            part = jnp.sum(y * y, axis=-1, keepdims=True)
            ss = part if ss is None else ss + part
        inv = lax.rsqrt(ss / (nn * tn) + EPS)
        for c in range(nn):
            cols = slice(c * tn, (c + 1) * tn)
            o_ref[:, cols] = x_ref[:, cols] + y_s[c] * inv * g_ref[:, cols]


def _ffn(x, h, w_up, conv_w, conv_b, prev0, prev1, w_down, gpost, *, seq, tm_up, tm_down, tf=512, tn=512):
    m = x.shape[0]
    nf = D_FF // tf
    assert CONV_W == 3 and nf * tf == D_FF
    prow = 1 if seq else tm_up
    pspec = pl.BlockSpec((prow, tf), (lambda i, j: (0, j)) if seq else (lambda i, j: (i, j)))
    if seq:
        gspec = pl.BlockSpec((None, CONV_W - 1, tf), lambda i, j: (i, 0, j))
        gshape = (m // tm_up, CONV_W - 1, D_FF)
    else:
        gspec = pl.BlockSpec((tm_up, tf), lambda i, j: (i, j))
        gshape = (m, D_FF)
    act, g_out = pl.pallas_call(
        functools.partial(_ffn_up_kernel, seq=seq),
        grid=(m // tm_up, nf),
        in_specs=[pl.BlockSpec((tm_up, D_MODEL), lambda i, j: (i, 0)),
                  pl.BlockSpec((D_MODEL, tf), lambda i, j: (0, j)),
                  pl.BlockSpec((D_MODEL, tf), lambda i, j: (0, nf + j)),
                  pl.BlockSpec((CONV_W, tf), lambda i, j: (0, j)),
                  pl.BlockSpec((1, tf), lambda i, j: (0, j)),
                  pspec, pspec],
        out_specs=[pl.BlockSpec((tm_up, tf), lambda i, j: (i, j)), gspec],
        out_shape=[jax.ShapeDtypeStruct((m, D_FF), BF16), jax.ShapeDtypeStruct(gshape, F32)],
        scratch_shapes=[pltpu.VMEM((nf, 8, tf), F32)],
        compiler_params=_cparams("arbitrary", "arbitrary"),
        name="conv_ffn_up",
    )(h, w_up, w_up, conv_w, conv_b.reshape(1, D_FF), prev0, prev1)
    nn = D_MODEL // tn
    y = pl.pallas_call(
        _ffn_down_kernel,
        grid=(m // tm_down, nn),
        in_specs=[pl.BlockSpec((tm_down, D_FF), lambda i, n: (i, 0)),
                  pl.BlockSpec((D_FF, tn), lambda i, n: (0, n)),
                  pl.BlockSpec((tm_down, D_MODEL), lambda i, n: (i, 0)),
                  pl.BlockSpec((1, D_MODEL), lambda i, n: (0, 0))],
        out_specs=pl.BlockSpec((tm_down, D_MODEL), lambda i, n: (i, 0)),
        out_shape=jax.ShapeDtypeStruct((m, D_MODEL), F32),
        scratch_shapes=[pltpu.VMEM((nn, tm_down, tn), F32)],
        compiler_params=_cparams("parallel", "arbitrary"),
        name="ffn_down_residual",
    )(act, w_down, x, gpost.reshape(1, D_MODEL))
    return y, (g_out[-1] if seq else g_out)


def kernel(x_prompt, x_sample, cache_k, cache_v, state_hgrn, state_ffn_conv, page_table, rel_bias, hg_lb,
           norm_mix_pre, norm_mix_post, norm_ffn_pre, norm_ffn_post, w_in, hg_out_norm, w_branch_hgrn,
           w_branch_moba, w_out, w_ffn_up, ffn_conv_w, ffn_conv_b, w_ffn_down):
    nb, t, _ = x_prompt.shape
    db = x_sample.shape[0]
    depth = w_in.shape[0]
    assert nb == 1 and depth == 1 and x_sample.shape[1] == 1
    l = 0
    w_in_b = w_in[l].astype(BF16)
    w_bh = w_branch_hgrn[l].astype(BF16)
    w_bm = w_branch_moba[l].astype(BF16)
    w_o = w_out[l].astype(BF16)
    w_up = w_ffn_up[l].astype(BF16)
    w_dn = w_ffn_down[l].astype(BF16)

    xp = x_prompt.reshape(t, D_MODEL)
    proj = _norm_matmul(xp, norm_mix_pre[l], w_in_b, tm=1024, tn=1024)
    o_hg, s_new = _hgrn_seq(proj, hg_lb, hg_out_norm[l])
    o_att = _moba_seq(proj, rel_bias)
    merged = _merge(o_hg, o_att, w_bh, w_bm, proj, tm=512)
    x1, h2 = _out_proj(merged, w_o, xp, norm_mix_post[l], norm_ffn_pre[l], tm=512)
    zero_row = jnp.zeros((1, D_FF), F32)
    yp, conv_p = _ffn(x1, h2, w_up, ffn_conv_w[l], ffn_conv_b[l], zero_row, zero_row, w_dn,
                      norm_ffn_post[l], seq=True, tm_up=1024, tm_down=512)
    kp = proj[:, COL_AK * 128:COL_AK * 128 + ATT_W].reshape(1, 1, t, H_ATT, ATT_DH)
    vp = proj[:, COL_AV * 128:COL_AV * 128 + ATT_W].reshape(1, 1, t, H_ATT, ATT_DH)

    xs = x_sample.reshape(db, D_MODEL)
    n_pool, page_rows = cache_k.shape[1], cache_k.shape[2]
    ck = cache_k.reshape(depth * n_pool, page_rows, H_ATT, ATT_DH)
    cv = cache_v.reshape(depth * n_pool, page_rows, H_ATT, ATT_DH)
    projs = _norm_matmul(xs, norm_mix_pre[l], w_in_b, tm=db, tn=1024)
    o_hg_s, s_new_s = _hgrn_step(projs, state_hgrn[l], hg_lb, hg_out_norm[l])
    o_att_s = _moba_step(projs, ck, cv, page_table, rel_bias)
    merged_s = _merge(o_hg_s.reshape(db, HG_VW), o_att_s, w_bh, w_bm, projs, tm=db)
    x1s, h2s = _out_proj(merged_s, w_o, xs, norm_mix_post[l], norm_ffn_pre[l], tm=db)
    buf = state_ffn_conv[l]
    ys, g_s = _ffn(x1s, h2s, w_up, ffn_conv_w[l], ffn_conv_b[l], buf[:, 0], buf[:, 1], w_dn,
                   norm_ffn_post[l], seq=False, tm_up=db, tm_down=db)
    ks = projs[:, COL_AK * 128:COL_AK * 128 + ATT_W].reshape(1, db, 1, H_ATT, ATT_DH)
    vs = projs[:, COL_AV * 128:COL_AV * 128 + ATT_W].reshape(1, db, 1, H_ATT, ATT_DH)
    conv_s = jnp.stack([buf[:, 1], g_s], axis=1)

    return (yp.reshape(1, t, D_MODEL), ys.reshape(db, 1, D_MODEL), kp, vp,
            s_new.reshape(1, 1, H_HG, HG_DK, HG_DV), conv_p.reshape(1, 1, CONV_W - 1, D_FF),
            ks, vs, s_new_s.reshape(1, db, H_HG, HG_DK, HG_DV), conv_s.reshape(1, db, CONV_W - 1, D_FF))
```

```python
import functools
import math

import numpy as np
import jax
import jax.numpy as jnp
from jax import lax
from jax.experimental import pallas as pl
from jax.experimental.pallas import tpu as pltpu

F32 = jnp.float32
BF16 = jnp.bfloat16

D_MODEL = 2048
H_HG = 8
HG_DK = 128
HG_DV = 128
HG_W = H_HG * HG_DK
HG_VW = H_HG * HG_DV
H_ATT = 8
ATT_DH = 128
ATT_W = H_ATT * ATT_DH
MOBA_BLOCK = 256
MOBA_TOPK = 3
N_BUCKETS = 32
MAX_EXACT = N_BUCKETS // 2
REL_MAX_DIST = 1024
D_FF = 5632
CONV_W = 3
EPS = 1e-6
N_IN = 2 * HG_W + 2 * HG_VW + 3 * ATT_W + 2 * D_MODEL
ATT_SCALE = ATT_DH ** -0.5
LOG2E = math.log2(math.e)

COL_HQ, COL_HF, COL_HI, COL_OG = 0, 8, 16, 24
COL_AQ, COL_AK, COL_AV = 32, 40, 48
COLK_AK, COLK_GHG, COLK_GATT = 5, 7, 9

HG_CHUNK = 128
HG_SUB = 8
HG_HEADS_PER_STEP = 2
MOBA_GROUP = 8
VMEM_LIMIT = 56 * 1024 * 1024
NEG_INF = float("-inf")


def _bucket_thresholds():
    n = np.arange(MAX_EXACT, 4 * REL_MAX_DIST, dtype=np.float64)
    large = MAX_EXACT + (np.log(n / MAX_EXACT) / math.log(REL_MAX_DIST / MAX_EXACT)
                         * (N_BUCKETS - MAX_EXACT)).astype(np.int64)
    large = np.minimum(large, N_BUCKETS - 1)
    return [int(n[np.argmax(large >= b)]) for b in range(MAX_EXACT + 1, N_BUCKETS)]


BUCKET_THRESHOLDS = _bucket_thresholds()


def _cparams(*sem):
    return pltpu.CompilerParams(dimension_semantics=sem, vmem_limit_bytes=VMEM_LIMIT)


def _silu(x):
    return x * jax.nn.sigmoid(x)


def _dot(a, b):
    return jnp.dot(a, b, preferred_element_type=F32)


def _dot_nt(a, b, precision=None):
    return lax.dot_general(a, b, (((1,), (1,)), ((), ())), precision=precision,
                           preferred_element_type=F32)


def _norm_matmul_kernel(x_ref, g_ref, w_ref, o_ref, h_ref):
    @pl.when(pl.program_id(1) == 0)
    def _():
        x = x_ref[...]
        ms = jnp.mean(x * x, axis=-1, keepdims=True)
        h_ref[...] = (x * lax.rsqrt(ms + EPS) * g_ref[...]).astype(BF16)

    o_ref[...] = _dot(h_ref[...], w_ref[...])


def _norm_matmul(x, gain, w_bf16, tm, tn):
    m, k = x.shape
    n = w_bf16.shape[1]
    return pl.pallas_call(
        _norm_matmul_kernel,
        grid=(m // tm, n // tn),
        in_specs=[pl.BlockSpec((tm, k), lambda i, j: (i, 0)),
                  pl.BlockSpec((1, k), lambda i, j: (0, 0)),
                  pl.BlockSpec((k, tn), lambda i, j: (0, j))],
        out_specs=pl.BlockSpec((tm, tn), lambda i, j: (i, j)),
        out_shape=jax.ShapeDtypeStruct((m, n), F32),
        scratch_shapes=[pltpu.VMEM((tm, k), BF16)],
        compiler_params=_cparams("parallel", "arbitrary"),
        name="norm_in_proj",
    )(x, gain.reshape(1, k), w_bf16)


def _forget_lower_bound(lbraw):
    e = jnp.exp(lbraw - jnp.max(lbraw, axis=0, keepdims=True))
    return e[0:1] / jnp.sum(e, axis=0, keepdims=True)


def _hgrn_seq_kernel(hq_ref, hf_ref, hi_ref, og_ref, lbraw_ref, gn_ref, o_ref, s_out_ref,
                     st_ref, q_s, k_s, b_s, v_s, acc_s, *, chunk, heads):
    c = pl.program_id(1)

    @pl.when(c == 0)
    def _():
        st_ref[...] = jnp.zeros_like(st_ref)

    row = lax.broadcasted_iota(jnp.int32, (chunk, HG_DK), 0)
    rr = lax.broadcasted_iota(jnp.int32, (chunk, chunk), 0)
    cc = lax.broadcasted_iota(jnp.int32, (chunk, chunk), 1)
    sub = lax.broadcasted_iota(jnp.int32, (HG_SUB, HG_DK), 0)

    for hh in range(heads):
        cols = slice(hh * 128, (hh + 1) * 128)
        lb = _forget_lower_bound(lbraw_ref[:, cols])
        q = _silu(hq_ref[:, cols])
        f = lb + (1.0 - lb) * jax.nn.sigmoid(hf_ref[:, cols])
        k = 1.0 - f
        v = hi_ref[:, cols]

        b = jnp.log(f)
        shift = 1
        while shift < chunk:
            b = b + jnp.where(row >= shift, pltpu.roll(b, shift, axis=0), 0.0)
            shift *= 2

        q_s[hh] = q
        k_s[hh] = k
        b_s[hh] = b
        v_s[hh] = v

        st = st_ref[hh]
        o = _dot_nt((q * jnp.exp(b)).astype(BF16), st.astype(BF16))

        a = jnp.zeros((chunk, chunk), F32)
        hs = HG_SUB
        while hs < chunk:
            blk = 2 * hs
            ref_rows = jnp.concatenate(
                [jnp.broadcast_to(b_s[hh, m0 + hs - 1:m0 + hs, :], (blk, HG_DK)) for m0 in range(0, chunk, blk)],
                axis=0)
            second = (row & (blk - 1)) >= hs
            qd = jnp.where(second, q * jnp.exp(jnp.minimum(b - ref_rows, 0.0)), 0.0)
            kd = jnp.where(second, 0.0, k * jnp.exp(jnp.minimum(ref_rows - b, 0.0)))
            a_l = _dot_nt(qd.astype(BF16), kd.astype(BF16))
            sh = int(math.log2(blk))
            a = a + jnp.where((rr >> sh) == (cc >> sh), a_l, 0.0)
            hs = blk
        o = o + _dot(a.astype(BF16), v.astype(BF16))

        for r0 in range(0, chunk, HG_SUB):
            qi = q_s[hh, r0:r0 + HG_SUB, :]
            bi = b_s[hh, r0:r0 + HG_SUB, :]
            acc = jnp.zeros((HG_SUB, HG_DV), F32)
            for s in range(HG_SUB):
                ks = k_s[hh, r0 + s:r0 + s + 1, :]
                bs = b_s[hh, r0 + s:r0 + s + 1, :]
                vs = v_s[hh, r0 + s:r0 + s + 1, :]
                e = jnp.where(sub >= s, jnp.exp(jnp.minimum(bi - bs, 0.0)), 0.0)
                w = jnp.sum(qi * ks * e, axis=-1, keepdims=True)
                acc = acc + w * vs
            acc_s[hh, r0:r0 + HG_SUB, :] = acc
        o = o + acc_s[hh]

        b_last = b_s[hh, chunk - 1:chunk, :]
        kd = k * jnp.exp(b_last - b)
        st_new = st * jnp.exp(b_last) + _dot(v.T.astype(BF16), kd.astype(BF16))
        st_ref[hh] = st_new

        ms = jnp.mean(o * o, axis=-1, keepdims=True)
        o_ref[:, cols] = (o * lax.rsqrt(ms + EPS) * gn_ref[...] * _silu(og_ref[:, cols])).astype(o_ref.dtype)

    @pl.when(c == pl.num_programs(1) - 1)
    def _():
        for hh in range(heads):
            s_out_ref[hh] = st_ref[hh].T


def _hgrn_seq(proj, hg_lb, gn, chunk=HG_CHUNK, heads=HG_HEADS_PER_STEP):
    t = proj.shape[0]
    w = 128 * heads
    blk = lambda off: pl.BlockSpec((chunk, w), lambda h, c, off=off: (c, off // heads + h))
    return pl.pallas_call(
        functools.partial(_hgrn_seq_kernel, chunk=chunk, heads=heads),
        grid=(H_HG // heads, t // chunk),
        in_specs=[blk(COL_HQ), blk(COL_HF), blk(COL_HI), blk(COL_OG),
                  pl.BlockSpec((hg_lb.shape[0], w), lambda h, c: (0, h)),
                  pl.BlockSpec((1, HG_DV), lambda h, c: (0, 0))],
        out_specs=[pl.BlockSpec((chunk, w), lambda h, c: (c, h)),
                   pl.BlockSpec((heads, HG_DK, HG_DV), lambda h, c: (h, 0, 0))],
        out_shape=[jax.ShapeDtypeStruct((t, HG_VW), BF16),
                   jax.ShapeDtypeStruct((H_HG, HG_DK, HG_DV), F32)],
        scratch_shapes=[pltpu.VMEM((heads, HG_DV, HG_DK), F32)] + [pltpu.VMEM((heads, chunk, 128), F32)] * 5,
        compiler_params=_cparams("parallel", "arbitrary"),
        name="hgrn_seq",
    )(proj, proj, proj, proj, hg_lb, gn.reshape(1, HG_DV))


def _hgrn_step_kernel(hqc_ref, hfc_ref, hi_ref, og_ref, lbc_ref, gn_ref, s_ref, o_ref, s_out_ref):
    lbraw = lbc_ref[...]
    e = jnp.exp(lbraw - jnp.max(lbraw, axis=0, keepdims=True))
    lb = e[0] / jnp.sum(e, axis=0)
    q = _silu(hqc_ref[...])
    f = lb + (1.0 - lb) * jax.nn.sigmoid(hfc_ref[...])
    k = 1.0 - f
    v = hi_ref[...]
    s_new = f * s_ref[...] + k * v
    s_out_ref[...] = s_new
    o = jnp.sum(s_new * q, axis=1, keepdims=True)
    ms = jnp.mean(o * o, axis=-1, keepdims=True)
    o_ref[...] = (o * lax.rsqrt(ms + EPS) * gn_ref[...] * _silu(og_ref[...])).astype(o_ref.dtype)


def _hgrn_step(proj, state, hg_lb, gn):
    nb = proj.shape[0]
    col = lambda a: a.reshape(nb, H_HG, HG_DK, 1)
    rowv = lambda a: a.reshape(nb, H_HG, 1, HG_DV)
    hq = col(proj[:, 0:HG_W])
    hf = col(proj[:, HG_W:2 * HG_W])
    hi = rowv(proj[:, 2 * HG_W:2 * HG_W + HG_VW])
    og = rowv(proj[:, 2 * HG_W + HG_VW:2 * HG_W + 2 * HG_VW])
    nl = hg_lb.shape[0]
    cspec = pl.BlockSpec((None, H_HG, HG_DK, 1), lambda b: (b, 0, 0, 0))
    rspec = pl.BlockSpec((None, H_HG, 1, HG_DV), lambda b: (b, 0, 0, 0))
    sspec = pl.BlockSpec((None, H_HG, HG_DK, HG_DV), lambda b: (b, 0, 0, 0))
    return pl.pallas_call(
        _hgrn_step_kernel,
        grid=(nb,),
        in_specs=[cspec, cspec, rspec, rspec,
                  pl.BlockSpec((nl, H_HG, HG_DK, 1), lambda b: (0, 0, 0, 0)),
                  pl.BlockSpec((1, 1, HG_DV), lambda b: (0, 0, 0)),
                  sspec],
        out_specs=[rspec, sspec],
        out_shape=[jax.ShapeDtypeStruct((nb, H_HG, 1, HG_DV), BF16),
                   jax.ShapeDtypeStruct((nb, H_HG, HG_DK, HG_DV), F32)],
        compiler_params=_cparams("parallel"),
        name="hgrn_step",
    )(hq, hf, hi, og, hg_lb.reshape(nl, H_HG, HG_DK, 1), gn.reshape(1, 1, HG_DV), state)


BUCKET_STARTS = list(range(1, MAX_EXACT + 1)) + BUCKET_THRESHOLDS


def _bias_from_rel(rel, lo, hi, tab_ref, h):
    first = sum(1 for start in BUCKET_STARTS if start <= lo)
    out = jnp.full(rel.shape, tab_ref[first, h], F32)
    for bkt in range(first + 1, N_BUCKETS):
        if BUCKET_STARTS[bkt - 1] <= hi:
            out = jnp.where(rel >= BUCKET_STARTS[bkt - 1], tab_ref[bkt, h], out)
    return jnp.where(rel >= 0, out, NEG_INF) if lo < 0 else out


def _bias_tiles_kernel(tab_ref, o_ref, *, base, tile_step, row_step, col_step, scale):
    h = pl.program_id(0)
    n_tiles, rows, cols = o_ref.shape
    r = lax.broadcasted_iota(jnp.int32, (rows, cols), 0)
    c = lax.broadcasted_iota(jnp.int32, (rows, cols), 1)
    in_tile = r * row_step + c * col_step
    spans = [row_step * (rows - 1), col_step * (cols - 1)]
    for t in range(n_tiles):
        off = base + t * tile_step
        lo = off + sum(min(sp, 0) for sp in spans)
        hi = off + sum(max(sp, 0) for sp in spans)
        bias = _bias_from_rel(off + in_tile, lo, hi, tab_ref, h)
        o_ref[t] = bias if scale == 1.0 else bias * scale


def _bias_tiles(rel_bias, n_tiles, rows, cols, base, tile_step, row_step, col_step, scale=1.0):
    return pl.pallas_call(
        functools.partial(_bias_tiles_kernel, base=base, tile_step=tile_step, row_step=row_step,
                          col_step=col_step, scale=scale),
        grid=(H_ATT,),
        in_specs=[pl.BlockSpec(memory_space=pltpu.SMEM)],
        out_specs=pl.BlockSpec((None, n_tiles, rows, cols), lambda h: (h, 0, 0, 0)),
        out_shape=jax.ShapeDtypeStruct((H_ATT, n_tiles, rows, cols), F32),
        compiler_params=_cparams("parallel"),
        name="rel_bias_tiles",
    )(rel_bias)


def _block_mean_kernel(k_ref, o_ref):
    o_ref[...] = jnp.mean(k_ref[...], axis=0, keepdims=True)


def _block_mean(proj):
    t = proj.shape[0]
    nb = t // MOBA_BLOCK
    out = pl.pallas_call(
        _block_mean_kernel,
        grid=(nb,),
        in_specs=[pl.BlockSpec((MOBA_BLOCK, ATT_W), lambda n: (n, COLK_AK))],
        out_specs=pl.BlockSpec((None, 1, ATT_W), lambda n: (n, 0, 0)),
        out_shape=jax.ShapeDtypeStruct((nb, 1, ATT_W), F32),
        compiler_params=_cparams("parallel"),
        name="moba_block_mean",
    )(proj)
    return out.reshape(nb, ATT_W)


def _top_blocks(s, n_valid, axis):
    n = float(s.shape[axis])
    idx = lax.broadcasted_iota(jnp.int32, s.shape, axis).astype(F32)
    s = jnp.where(idx < n_valid, s, NEG_INF)
    sel = jnp.zeros(s.shape, F32)
    for _ in range(MOBA_TOPK):
        m = jnp.max(s, axis=axis, keepdims=True)
        first = jnp.min(jnp.where(s == m, idx, n), axis=axis, keepdims=True)
        pick = jnp.logical_and(idx == first, m > NEG_INF)
        sel = jnp.where(pick, 1.0, sel)
        s = jnp.where(pick, NEG_INF, s)
    return sel


def _moba_pen_kernel(tab_ref, q_ref, km_ref, pen_ref, penfar_ref):
    h = pl.program_id(0)
    nblk = km_ref.shape[0]
    tq = q_ref.shape[0]
    scores = _dot_nt(km_ref[...], q_ref[...] * ATT_SCALE, precision=lax.Precision.HIGHEST)
    pos = pl.program_id(1) * tq + lax.broadcasted_iota(jnp.int32, (1, tq), 1)
    own = pos // MOBA_BLOCK
    blk_id = lax.broadcasted_iota(jnp.int32, scores.shape, 0)
    chosen = _top_blocks(scores, own, axis=0) > 0.5
    pen_ref[0:nblk, :] = jnp.where(jnp.logical_or(chosen, blk_id == own), 0.0, NEG_INF)
    pen_ref[nblk:, :] = jnp.full((pen_ref.shape[0] - nblk, tq), NEG_INF, F32)
    far_bias = tab_ref[N_BUCKETS - 1, h] * LOG2E
    penfar_ref[...] = jnp.where(jnp.logical_and(chosen, blk_id <= own - MOBA_GROUP), far_bias, NEG_INF)


def _moba_pen(proj, kmean, rel_bias, tq=1024):
    t = proj.shape[0]
    nblk = t // MOBA_BLOCK
    return pl.pallas_call(
        _moba_pen_kernel,
        grid=(H_ATT, t // tq),
        in_specs=[pl.BlockSpec(memory_space=pltpu.SMEM),
                  pl.BlockSpec((tq, ATT_DH), lambda h, n: (n, COL_AQ + h)),
                  pl.BlockSpec((nblk, ATT_DH), lambda h, n: (0, h))],
        out_specs=[pl.BlockSpec((None, nblk + 8, tq), lambda h, n: (h, 0, n)),
                   pl.BlockSpec((None, nblk, tq), lambda h, n: (h, 0, n))],
        out_shape=[jax.ShapeDtypeStruct((H_ATT, nblk + 8, t), F32),
                   jax.ShapeDtypeStruct((H_ATT, nblk, t), F32)],
        compiler_params=_cparams("parallel", "parallel"),
        name="moba_block_masks",
    )(rel_bias, proj, kmean)


def _stream_page_sums(pt_ref, pages_hbm, sums_ref, pbuf, psem, step, n_steps, pages_per_block):
    n_pg = pbuf.shape[1]
    total = pt_ref.shape[0]
    slot = step % 2

    def copies(st, sl):
        return [pltpu.make_async_copy(pages_hbm.at[pt_ref[jnp.minimum(st * n_pg + p, total - 1)]],
                                      pbuf.at[sl, p], psem.at[sl, p]) for p in range(n_pg)]

    @pl.when(step == 0)
    def _():
        for cp in copies(step, slot):
            cp.start()

    @pl.when(step < n_steps)
    def _():
        for cp in copies(step, slot):
            cp.wait()

    @pl.when(step + 1 < n_steps)
    def _():
        for cp in copies(step + 1, 1 - slot):
            cp.start()

    for blk in range(n_pg // pages_per_block):
        tot = pbuf[slot, blk * pages_per_block].sum(axis=0)
        for pg in range(1, pages_per_block):
            tot = tot + pbuf[slot, blk * pages_per_block + pg].sum(axis=0)
        sums_ref[blk] = tot


def _moba_seq_kernel(*refs, n_far, stream):
    if stream:
        n_stream_steps, pages_per_block = stream
        (pt_ref, q_ref, k_ref, v_ref, bias_ref, pen_ref, penfar_ref, pages_hbm, o_ref, sums_ref,
         kb_s, vt_s, acc_s, s_s, pbuf, psem) = refs
        _stream_page_sums(pt_ref, pages_hbm, sums_ref, pbuf, psem,
                          pl.program_id(0) * pl.num_programs(1) + pl.program_id(1), n_stream_steps, pages_per_block)
    else:
        q_ref, k_ref, v_ref, bias_ref, pen_ref, penfar_ref, o_ref, kb_s, vt_s, acc_s, s_s = refs
    i = pl.program_id(1)
    nblk = vt_s.shape[0]

    @pl.when(i == 0)
    def _():
        kb_s[...] = k_ref[...].astype(BF16)

        def xpose(j, carry):
            r0 = pl.multiple_of(j * MOBA_BLOCK, MOBA_BLOCK)
            vt_s[j] = v_ref[pl.ds(r0, MOBA_BLOCK), :].T.astype(BF16)
            return carry

        lax.fori_loop(0, nblk, xpose, 0)

    tq = q_ref.shape[0]
    grp = MOBA_GROUP
    qb = (q_ref[...] * (ATT_SCALE * LOG2E)).astype(BF16)
    acc_s[...] = jnp.zeros_like(acc_s)

    def logits(j):
        c0 = pl.multiple_of(j * MOBA_BLOCK, MOBA_BLOCK)
        return _dot_nt(kb_s[pl.ds(c0, MOBA_BLOCK), :], qb)

    def keep_max(smax, s):
        cur = jnp.max(s, axis=0, keepdims=True)
        return cur if smax is None else jnp.maximum(smax, cur)

    smax = None
    for u in range(grp):
        j = jnp.maximum(i - u, 0)
        pen_row = jnp.where(u <= i, j, nblk)
        s = logits(j) + bias_ref[min(u, n_far)] + pen_ref[pl.ds(pen_row, 1), :]
        s_s[0, u] = s
        smax = keep_max(smax, s)

    def softmax_update(slot, m_old, l_old, smax, block_of):
        mg = jnp.maximum(m_old, smax)
        alpha = jnp.exp2(m_old - mg)
        l_new = alpha * l_old
        pv = None
        for u in range(grp):
            p = jnp.exp2(s_s[slot, u] - mg)
            l_new = l_new + jnp.sum(p, axis=0, keepdims=True)
            d = _dot(vt_s[block_of(u)], p.astype(BF16))
            pv = d if pv is None else pv + d
        acc_s[...] = alpha * acc_s[...] + pv
        return mg, l_new

    def trip_with_slots(t, carry, slot_prev):
        m_old, l_old, smax_prev = carry
        smax_new = None
        for u in range(grp):
            s = logits(t * grp + u) + penfar_ref[pl.ds(t * grp + u, 1), :]
            s_s[1 - slot_prev, u] = s
            smax_new = keep_max(smax_new, s)
        prev = lambda u: jnp.where(t == 0, jnp.maximum(i - u, 0), (t - 1) * grp + u)
        m_new, l_new = softmax_update(slot_prev, m_old, l_old, smax_prev, prev)
        return m_new, l_new, smax_new

    def trip(t, carry):
        return lax.cond(t % 2 == 0, lambda c: trip_with_slots(t, c, 0), lambda c: trip_with_slots(t, c, 1), carry)

    n_far_groups = i // grp
    init = (jnp.full((1, tq), NEG_INF, F32), jnp.zeros((1, tq), F32), smax)
    m, l, smax = lax.fori_loop(0, n_far_groups, trip, init)
    last = lambda u: jnp.where(n_far_groups == 0, jnp.maximum(i - u, 0), (n_far_groups - 1) * grp + u)
    m, l = softmax_update(n_far_groups % 2, m, l, smax, last)
    o_ref[...] = (acc_s[...] / l).T.astype(o_ref.dtype)


def _moba_seq(proj, rel_bias, side_pool=None, side_page_table=None):
    t = proj.shape[0]
    nq = t // MOBA_BLOCK
    kmean = _block_mean(proj)
    n_far = -(-(BUCKET_THRESHOLDS[-1] + MOBA_BLOCK) // MOBA_BLOCK)
    assert MOBA_GROUP >= n_far and nq % MOBA_GROUP == 0
    bias = _bias_tiles(rel_bias, n_far + 1, MOBA_BLOCK, MOBA_BLOCK, 0, MOBA_BLOCK, -1, 1, scale=LOG2E)
    pen, penfar = _moba_pen(proj, kmean, rel_bias)
    n_steps = H_ATT * nq
    in_specs = [pl.BlockSpec((MOBA_BLOCK, ATT_DH), lambda h, i, *_: (i, COL_AQ + h)),
                pl.BlockSpec((t, ATT_DH), lambda h, i, *_: (0, COL_AK + h)),
                pl.BlockSpec((t, ATT_DH), lambda h, i, *_: (0, COL_AV + h)),
                pl.BlockSpec((None, n_far + 1, MOBA_BLOCK, MOBA_BLOCK), lambda h, i, *_: (h, 0, 0, 0)),
                pl.BlockSpec((None, nq + 8, MOBA_BLOCK), lambda h, i, *_: (h, 0, i)),
                pl.BlockSpec((None, nq, MOBA_BLOCK), lambda h, i, *_: (h, 0, i))]
    out_specs = [pl.BlockSpec((MOBA_BLOCK, ATT_DH), lambda h, i, *_: (i, h))]
    out_shape = [jax.ShapeDtypeStruct((t, ATT_W), BF16)]
    scratch = [pltpu.VMEM((t, ATT_DH), BF16), pltpu.VMEM((nq, ATT_DH, MOBA_BLOCK), BF16),
               pltpu.VMEM((ATT_DH, MOBA_BLOCK), F32),
               pltpu.VMEM((2, MOBA_GROUP, MOBA_BLOCK, MOBA_BLOCK), F32)]
    operands = [proj, proj, proj, bias, pen, penfar]
    stream = None
    prefetch = []
    if side_pool is not None:
        pt = side_page_table.reshape(-1)
        page_rows = side_pool.shape[1]
        ppb = MOBA_BLOCK // page_rows
        assert ppb * page_rows == MOBA_BLOCK and pt.shape[0] % ppb == 0
        n_pg = ppb * -(-pt.shape[0] // (ppb * n_steps))
        stream = (-(-pt.shape[0] // n_pg), ppb)
        in_specs.append(pl.BlockSpec(memory_space=pl.ANY))
        out_specs.append(pl.BlockSpec((None, n_pg // ppb, H_ATT, ATT_DH), lambda h, i, *_: (h * nq + i, 0, 0, 0)))
        out_shape.append(jax.ShapeDtypeStruct((n_steps, n_pg // ppb, H_ATT, ATT_DH), F32))
        scratch += [pltpu.VMEM((2, n_pg) + side_pool.shape[1:], F32), pltpu.SemaphoreType.DMA((2, n_pg))]
        operands.append(side_pool)
        prefetch = [pt]
    outs = pl.pallas_call(
        functools.partial(_moba_seq_kernel, n_far=n_far, stream=stream),
        grid_spec=pltpu.PrefetchScalarGridSpec(
            num_scalar_prefetch=len(prefetch), grid=(H_ATT, nq), in_specs=in_specs, out_specs=out_specs,
            scratch_shapes=scratch),
        out_shape=out_shape,
        compiler_params=_cparams("arbitrary", "arbitrary"),
        name="moba_seq",
    )(*prefetch, *operands)
    if side_pool is None:
        return outs[0], None
    return outs[0], outs[1].reshape(-1, H_ATT, ATT_DH)[:pt.shape[0] // ppb]


SELECT_PAGES_PER_STEP = 8
FFN_CHUNK = 256
MOBA_STEP_SLOTS = 4


def _moba_select_kernel(ksum_ref, q_ref, sel_ref):
    kmean = ksum_ref[...] / MOBA_BLOCK
    prod = kmean * (q_ref[...] * ATT_SCALE)[None]
    s = jnp.sum(prod, axis=-1, keepdims=True)
    nb = s.shape[0]
    mask = _top_blocks(s, nb, axis=0)
    idx = lax.broadcasted_iota(jnp.int32, s.shape, 0).astype(F32)
    for r in range(MOBA_TOPK):
        first = jnp.min(jnp.where(mask > 0.5, idx, float(nb)), axis=0)
        sel_ref[r] = first.astype(jnp.int32)
        mask = jnp.where(idx == first[None], 0.0, mask)


def _moba_select(block_sums, q_heads):
    nb_seq, n_blocks = block_sums.shape[:2]
    assert n_blocks >= MOBA_TOPK
    return pl.pallas_call(
        _moba_select_kernel,
        grid=(nb_seq,),
        in_specs=[pl.BlockSpec((None, n_blocks, H_ATT, ATT_DH), lambda b: (b, 0, 0, 0)),
                  pl.BlockSpec((None, H_ATT, ATT_DH), lambda b: (b, 0, 0))],
        out_specs=pl.BlockSpec((None, MOBA_TOPK, H_ATT, 1), lambda b: (b, 0, 0, 0)),
        out_shape=jax.ShapeDtypeStruct((nb_seq, MOBA_TOPK, H_ATT, 1), jnp.int32),
        compiler_params=_cparams("parallel"),
        name="moba_select",
    )(block_sums, q_heads)


def _page_block_sums_kernel(pt_ref, *refs, pages_per_block):
    page_refs, o_ref = refs[:-1], refs[-1]
    for blk in range(len(page_refs) // pages_per_block):
        tot = page_refs[blk * pages_per_block][...].sum(axis=0)
        for pg in range(1, pages_per_block):
            tot = tot + page_refs[blk * pages_per_block + pg][...].sum(axis=0)
        o_ref[blk] = tot


def _page_block_sums(pool, page_table):
    nb_seq, n_pages = page_table.shape
    page_rows = pool.shape[1]
    ppb = MOBA_BLOCK // page_rows
    n_pg = SELECT_PAGES_PER_STEP
    assert ppb * page_rows == MOBA_BLOCK and n_pg % ppb == 0 and n_pages % n_pg == 0
    kspec = lambda p: pl.BlockSpec((None,) + pool.shape[1:],
                                   lambda b, n, pt, p=p: (pt[b * n_pages + n_pg * n + p], 0, 0, 0))
    return pl.pallas_call(
        functools.partial(_page_block_sums_kernel, pages_per_block=ppb),
        grid_spec=pltpu.PrefetchScalarGridSpec(
            num_scalar_prefetch=1,
            grid=(nb_seq, n_pages // n_pg),
            in_specs=[kspec(p) for p in range(n_pg)],
            out_specs=pl.BlockSpec((None, n_pg // ppb, H_ATT, ATT_DH), lambda b, n, pt: (b, n, 0, 0))),
        out_shape=jax.ShapeDtypeStruct((nb_seq, n_pages // ppb, H_ATT, ATT_DH), F32),
        compiler_params=_cparams("parallel", "arbitrary"),
        name="page_block_sums",
    )(page_table.reshape(-1), *([pool] * n_pg))


def _moba_step_kernel(pt_ref, sel_ref, q_ref, kn_ref, vn_ref, b0_ref, b1_ref, b2_ref, bown_ref, ck_hbm, cv_hbm,
                      o_ref, kbuf, vbuf, sem, *, n_pages):
    n_pg = MOBA_TOPK * 2
    b_refs = (b0_ref, b1_ref, b2_ref)
    n_slots = kbuf.shape[0]
    page_rows = kbuf.shape[2]
    b = pl.program_id(0)
    h = pl.program_id(1)
    n_heads = pl.num_programs(1)
    n_steps = pl.num_programs(0) * n_heads
    step = b * n_heads + h
    slot = step % n_slots

    def page_copies(bb, hh, sl):
        cps = []
        for r in range(MOBA_TOPK):
            blk = sel_ref[(bb * MOBA_TOPK + r) * n_heads + hh]
            for half in range(2):
                page = pt_ref[bb * n_pages + 2 * blk + half]
                idx = 2 * r + half
                cps.append(pltpu.make_async_copy(ck_hbm.at[page, :, hh, :], kbuf.at[sl, idx], sem.at[sl, idx]))
                cps.append(pltpu.make_async_copy(cv_hbm.at[page, :, hh, :], vbuf.at[sl, idx], sem.at[sl, n_pg + idx]))
        return cps

    def start_step(st):
        for cp in page_copies(st // n_heads, st % n_heads, st % n_slots):
            cp.start()

    @pl.when(step == 0)
    def _():
        for ahead in range(n_slots - 1):
            @pl.when(ahead < n_steps)
            def _():
                start_step(jnp.int32(ahead))

    @pl.when(step + n_slots - 1 < n_steps)
    def _():
        start_step(step + n_slots - 1)

    for cp in page_copies(b, h, slot):
        cp.wait()

    qs = q_ref[...] * ATT_SCALE
    qb = qs.astype(BF16)
    logits = []
    for r in range(MOBA_TOPK):
        bias = b_refs[r][...]
        for half in range(2):
            kk = kbuf[slot, 2 * r + half].astype(BF16)
            logits.append(_dot_nt(qb, kk) + bias[:, half * page_rows:(half + 1) * page_rows])
    l_own = jnp.sum(qs * kn_ref[...], axis=-1, keepdims=True) + bown_ref[:, 0:1]
    m = l_own
    for lg in logits:
        m = jnp.maximum(m, jnp.max(lg, axis=-1, keepdims=True))
    p_own = jnp.exp(l_own - m)
    den = p_own
    acc = p_own * vn_ref[...]
    for idx, lg in enumerate(logits):
        p = jnp.exp(lg - m)
        den = den + jnp.sum(p, axis=-1, keepdims=True)
        acc = acc + _dot(p.astype(BF16), vbuf[slot, idx].astype(BF16))
    o_ref[...] = (acc / den).astype(o_ref.dtype)


def _moba_step(proj, block_sums, cache_k4, cache_v4, page_table, rel_bias):
    nb_seq, n_pages = page_table.shape
    page_rows = cache_k4.shape[1]
    ppb = MOBA_BLOCK // page_rows
    assert ppb == 2
    n_blocks = n_pages // ppb
    n_past = n_pages * page_rows
    head_rows = lambda a: a.reshape(nb_seq, H_ATT, 1, ATT_DH)
    aq = proj[:, COL_AQ * 128:COL_AQ * 128 + ATT_W]
    ak = proj[:, COL_AK * 128:COL_AK * 128 + ATT_W]
    av = proj[:, COL_AV * 128:COL_AV * 128 + ATT_W]
    sel = _moba_select(block_sums.reshape(nb_seq, n_blocks, H_ATT, ATT_DH), aq.reshape(nb_seq, H_ATT, ATT_DH))
    bias = _bias_tiles(rel_bias, n_blocks + 1, 1, MOBA_BLOCK, n_past, -MOBA_BLOCK, 0, -1)

    def bias_spec(r):
        return pl.BlockSpec((None, None, 1, MOBA_BLOCK),
                            lambda b, h, pt, sl: (h, sl[(b * MOBA_TOPK + r) * H_ATT + h], 0, 0))

    row_spec = pl.BlockSpec((None, None, 1, ATT_DH), lambda b, h, pt, sl: (b, h, 0, 0))
    hbm = pl.BlockSpec(memory_space=pl.ANY)
    n_pg = 2 * MOBA_TOPK
    out = pl.pallas_call(
        functools.partial(_moba_step_kernel, n_pages=n_pages),
        grid_spec=pltpu.PrefetchScalarGridSpec(
            num_scalar_prefetch=2,
            grid=(nb_seq, H_ATT),
            in_specs=[row_spec, row_spec, row_spec] + [bias_spec(r) for r in range(MOBA_TOPK)]
                     + [pl.BlockSpec((None, None, 1, MOBA_BLOCK), lambda b, h, pt, sl: (h, n_blocks, 0, 0)),
                        hbm, hbm],
            out_specs=row_spec,
            scratch_shapes=[pltpu.VMEM((MOBA_STEP_SLOTS, n_pg, page_rows, ATT_DH), F32),
                            pltpu.VMEM((MOBA_STEP_SLOTS, n_pg, page_rows, ATT_DH), F32),
                            pltpu.SemaphoreType.DMA((MOBA_STEP_SLOTS, 2 * n_pg))]),
        out_shape=jax.ShapeDtypeStruct((nb_seq, H_ATT, 1, ATT_DH), BF16),
        compiler_params=_cparams("arbitrary", "arbitrary"),
        name="moba_step",
    )(page_table.reshape(-1), sel.reshape(-1), head_rows(aq), head_rows(ak), head_rows(av),
      bias, bias, bias, bias, cache_k4, cache_v4)
    return out.reshape(nb_seq, ATT_W)


def _merge_kernel(oh_ref, oa_ref, wh_ref, wa_ref, gh_ref, ga_ref, o_ref):
    br_h = _dot(oh_ref[...], wh_ref[...])
    br_a = _dot(oa_ref[...], wa_ref[...])
    o_ref[...] = (jax.nn.sigmoid(gh_ref[...]) * br_h + jax.nn.sigmoid(ga_ref[...]) * br_a).astype(o_ref.dtype)


def _merge(o_hg, o_att, w_bh, w_bm, proj, tm):
    m = o_hg.shape[0]
    tn = 1024
    return pl.pallas_call(
        _merge_kernel,
        grid=(m // tm, D_MODEL // tn),
        in_specs=[pl.BlockSpec((tm, HG_VW), lambda i, n: (i, 0)),
                  pl.BlockSpec((tm, ATT_W), lambda i, n: (i, 0)),
                  pl.BlockSpec((HG_VW, tn), lambda i, n: (0, n)),
                  pl.BlockSpec((ATT_W, tn), lambda i, n: (0, n)),
                  pl.BlockSpec((tm, tn), lambda i, n: (i, COLK_GHG + n)),
                  pl.BlockSpec((tm, tn), lambda i, n: (i, COLK_GATT + n))],
        out_specs=pl.BlockSpec((tm, tn), lambda i, n: (i, n)),
        out_shape=jax.ShapeDtypeStruct((m, D_MODEL), BF16),
        compiler_params=_cparams("parallel", "arbitrary"),
        name="gated_merge",
    )(o_hg, o_att, w_bh, w_bm, proj, proj)


def _out_proj_kernel(m_ref, w_ref, x_ref, g_ref, gnext_ref, o_ref, h_ref):
    z = _dot(m_ref[...], w_ref[...])
    ms = jnp.mean(z * z, axis=-1, keepdims=True)
    x1 = x_ref[...] + z * lax.rsqrt(ms + EPS) * g_ref[...]
    o_ref[...] = x1
    ms1 = jnp.mean(x1 * x1, axis=-1, keepdims=True)
    h_ref[...] = (x1 * lax.rsqrt(ms1 + EPS) * gnext_ref[...]).astype(h_ref.dtype)


def _out_proj(merged, w_out, x, gain, gain_next, tm):
    m = x.shape[0]
    vec = pl.BlockSpec((1, D_MODEL), lambda i: (0, 0))
    row = pl.BlockSpec((tm, D_MODEL), lambda i: (i, 0))
    return pl.pallas_call(
        _out_proj_kernel,
        grid=(m // tm,),
        in_specs=[row, pl.BlockSpec((D_MODEL, D_MODEL), lambda i: (0, 0)), row, vec, vec],
        out_specs=[row, row],
        out_shape=[jax.ShapeDtypeStruct((m, D_MODEL), F32), jax.ShapeDtypeStruct((m, D_MODEL), BF16)],
        compiler_params=_cparams("parallel"),
        name="out_proj_residual",
    )(merged, w_out, x, gain.reshape(1, D_MODEL), gain_next.reshape(1, D_MODEL))


def _ffn_up_kernel(h_ref, wg_ref, wu_ref, cw_ref, cb_ref, p0_ref, p1_ref, a_ref, g_out_ref, carry_s, *, seq):
    i = pl.program_id(0)
    j = pl.program_id(1)
    tm, tf = a_ref.shape
    if seq:
        @pl.when(i == 0)
        def _():
            carry_s[j, 0:1, :] = p0_ref[...]
            carry_s[j, 1:2, :] = p1_ref[...]

    for c in range(tf // FFN_CHUNK):
        cols = slice(c * FFN_CHUNK, (c + 1) * FFN_CHUNK)
        g = _dot(h_ref[...], wg_ref[:, cols])
        u = _dot(h_ref[...], wu_ref[:, cols])
        if seq:
            c0 = carry_s[j, 0:1, cols]
            c1 = carry_s[j, 1:2, cols]
            r1 = pltpu.roll(g, 1, axis=0)
            r2 = pltpu.roll(g, 2, axis=0)
            top = lax.broadcasted_iota(jnp.int32, (8, FFN_CHUNK), 0)
            prev1 = jnp.concatenate([jnp.where(top == 0, c1, r1[0:8]), r1[8:]], axis=0)
            prev2 = jnp.concatenate([jnp.where(top == 0, c0, jnp.where(top == 1, c1, r2[0:8])), r2[8:]], axis=0)
            carry_s[j, 0:2, cols] = g[tm - 2:tm, :]
            g_out_ref[:, cols] = g[tm - 2:tm, :]
        else:
            prev2 = p0_ref[:, cols]
            prev1 = p1_ref[:, cols]
            g_out_ref[:, cols] = g
        gc = cb_ref[:, cols] + cw_ref[0:1, cols] * prev2 + cw_ref[1:2, cols] * prev1 + cw_ref[2:3, cols] * g
        a_ref[:, cols] = (jax.nn.gelu(gc, approximate=True) * u).astype(a_ref.dtype)


def _ffn_down_kernel(a_ref, wd_ref, x_ref, g_ref, o_ref, y_s):
    n = pl.program_id(1)
    nn = y_s.shape[0]
    tn = y_s.shape[2]
    y_s[n] = _dot(a_ref[...], wd_ref[...])

    @pl.when(n == nn - 1)
    def _():
        ss = None
        for c in range(nn):
            y = y_s[c]
            part = jnp.sum(y * y, axis=-1, keepdims=True)
            ss = part if ss is None else ss + part
        inv = lax.rsqrt(ss / (nn * tn) + EPS)
        for c in range(nn):
            cols = slice(c * tn, (c + 1) * tn)
            o_ref[:, cols] = x_ref[:, cols] + y_s[c] * inv * g_ref[:, cols]


def _ffn(x, h, w_up, conv_w, conv_b, prev0, prev1, w_down, gpost, *, seq, tm_up, tm_down, tf=512, tn=512):
    m = x.shape[0]
    nf = D_FF // tf
    assert CONV_W == 3 and nf * tf == D_FF
    prow = 1 if seq else tm_up
    pspec = pl.BlockSpec((prow, tf), (lambda i, j: (0, j)) if seq else (lambda i, j: (i, j)))
    if seq:
        gspec = pl.BlockSpec((None, CONV_W - 1, tf), lambda i, j: (i, 0, j))
        gshape = (m // tm_up, CONV_W - 1, D_FF)
    else:
        gspec = pl.BlockSpec((tm_up, tf), lambda i, j: (i, j))
        gshape = (m, D_FF)
    act, g_out = pl.pallas_call(
        functools.partial(_ffn_up_kernel, seq=seq),
        grid=(m // tm_up, nf),
        in_specs=[pl.BlockSpec((tm_up, D_MODEL), lambda i, j: (i, 0)),
                  pl.BlockSpec((D_MODEL, tf), lambda i, j: (0, j)),
                  pl.BlockSpec((D_MODEL, tf), lambda i, j: (0, nf + j)),
                  pl.BlockSpec((CONV_W, tf), lambda i, j: (0, j)),
                  pl.BlockSpec((1, tf), lambda i, j: (0, j)),
                  pspec, pspec],
        out_specs=[pl.BlockSpec((tm_up, tf), lambda i, j: (i, j)), gspec],
        out_shape=[jax.ShapeDtypeStruct((m, D_FF), BF16), jax.ShapeDtypeStruct(gshape, F32)],
        scratch_shapes=[pltpu.VMEM((nf, 8, tf), F32)],
        compiler_params=_cparams("arbitrary", "arbitrary"),
        name="conv_ffn_up",
    )(h, w_up, w_up, conv_w, conv_b.reshape(1, D_FF), prev0, prev1)
    nn = D_MODEL // tn
    y = pl.pallas_call(
        _ffn_down_kernel,
        grid=(m // tm_down, nn),
        in_specs=[pl.BlockSpec((tm_down, D_FF), lambda i, n: (i, 0)),
                  pl.BlockSpec((D_FF, tn), lambda i, n: (0, n)),
                  pl.BlockSpec((tm_down, D_MODEL), lambda i, n: (i, 0)),
                  pl.BlockSpec((1, D_MODEL), lambda i, n: (0, 0))],
        out_specs=pl.BlockSpec((tm_down, D_MODEL), lambda i, n: (i, 0)),
        out_shape=jax.ShapeDtypeStruct((m, D_MODEL), F32),
        scratch_shapes=[pltpu.VMEM((nn, tm_down, tn), F32)],
        compiler_params=_cparams("parallel", "arbitrary"),
        name="ffn_down_residual",
    )(act, w_down, x, gpost.reshape(1, D_MODEL))
    return y, (g_out[-1] if seq else g_out)


def kernel(x_prompt, x_sample, cache_k, cache_v, state_hgrn, state_ffn_conv, page_table, rel_bias, hg_lb,
           norm_mix_pre, norm_mix_post, norm_ffn_pre, norm_ffn_post, w_in, hg_out_norm, w_branch_hgrn,
           w_branch_moba, w_out, w_ffn_up, ffn_conv_w, ffn_conv_b, w_ffn_down):
    nb, t, _ = x_prompt.shape
    db = x_sample.shape[0]
    depth = w_in.shape[0]
    assert nb == 1 and depth == 1 and x_sample.shape[1] == 1
    l = 0
    w_in_b = w_in[l].astype(BF16)
    w_bh = w_branch_hgrn[l].astype(BF16)
    w_bm = w_branch_moba[l].astype(BF16)
    w_o = w_out[l].astype(BF16)
    w_up = w_ffn_up[l].astype(BF16)
    w_dn = w_ffn_down[l].astype(BF16)

    n_pool, page_rows = cache_k.shape[1], cache_k.shape[2]
    ck = cache_k.reshape(depth * n_pool, page_rows, H_ATT, ATT_DH)
    cv = cache_v.reshape(depth * n_pool, page_rows, H_ATT, ATT_DH)

    xp = x_prompt.reshape(t, D_MODEL)
    proj = _norm_matmul(xp, norm_mix_pre[l], w_in_b, tm=1024, tn=1024)
    o_hg, s_new = _hgrn_seq(proj, hg_lb, hg_out_norm[l])
    o_att, cache_block_sums = _moba_seq(proj, rel_bias, side_pool=ck, side_page_table=page_table)
    merged = _merge(o_hg, o_att, w_bh, w_bm, proj, tm=512)
    x1, h2 = _out_proj(merged, w_o, xp, norm_mix_post[l], norm_ffn_pre[l], tm=512)
    zero_row = jnp.zeros((1, D_FF), F32)
    yp, conv_p = _ffn(x1, h2, w_up, ffn_conv_w[l], ffn_conv_b[l], zero_row, zero_row, w_dn,
                      norm_ffn_post[l], seq=True, tm_up=1024, tm_down=512)
    kp = proj[:, COL_AK * 128:COL_AK * 128 + ATT_W].reshape(1, 1, t, H_ATT, ATT_DH)
    vp = proj[:, COL_AV * 128:COL_AV * 128 + ATT_W].reshape(1, 1, t, H_ATT, ATT_DH)

    xs = x_sample.reshape(db, D_MODEL)
    projs = _norm_matmul(xs, norm_mix_pre[l], w_in_b, tm=db, tn=1024)
    o_hg_s, s_new_s = _hgrn_step(projs, state_hgrn[l], hg_lb, hg_out_norm[l])
    o_att_s = _moba_step(projs, cache_block_sums, ck, cv, page_table, rel_bias)
    merged_s = _merge(o_hg_s.reshape(db, HG_VW), o_att_s, w_bh, w_bm, projs, tm=db)
    x1s, h2s = _out_proj(merged_s, w_o, xs, norm_mix_post[l], norm_ffn_pre[l], tm=db)
    buf = state_ffn_conv[l]
    ys, g_s = _ffn(x1s, h2s, w_up, ffn_conv_w[l], ffn_conv_b[l], buf[:, 0], buf[:, 1], w_dn,
                   norm_ffn_post[l], seq=False, tm_up=db, tm_down=db)
    ks = projs[:, COL_AK * 128:COL_AK * 128 + ATT_W].reshape(1, db, 1, H_ATT, ATT_DH)
    vs = projs[:, COL_AV * 128:COL_AV * 128 + ATT_W].reshape(1, db, 1, H_ATT, ATT_DH)
    conv_s = jnp.stack([buf[:, 1], g_s], axis=1)

    return (yp.reshape(1, t, D_MODEL), ys.reshape(db, 1, D_MODEL), kp, vp,
            s_new.reshape(1, 1, H_HG, HG_DK, HG_DV), conv_p.reshape(1, 1, CONV_W - 1, D_FF),
            ks, vs, s_new_s.reshape(1, db, H_HG, HG_DK, HG_DV), conv_s.reshape(1, db, CONV_W - 1, D_FF))
```

```python
import functools
import math

import numpy as np
import jax
import jax.numpy as jnp
from jax import lax
from jax.experimental import pallas as pl
from jax.experimental.pallas import tpu as pltpu

F32 = jnp.float32
BF16 = jnp.bfloat16

D_MODEL = 2048
H_HG = 8
HG_DK = 128
HG_DV = 128
HG_W = H_HG * HG_DK
HG_VW = H_HG * HG_DV
H_ATT = 8
ATT_DH = 128
ATT_W = H_ATT * ATT_DH
MOBA_BLOCK = 256
MOBA_TOPK = 3
N_BUCKETS = 32
MAX_EXACT = N_BUCKETS // 2
REL_MAX_DIST = 1024
D_FF = 5632
CONV_W = 3
EPS = 1e-6
N_IN = 2 * HG_W + 2 * HG_VW + 3 * ATT_W + 2 * D_MODEL
ATT_SCALE = ATT_DH ** -0.5
LOG2E = math.log2(math.e)

COL_HQ, COL_HF, COL_HI, COL_OG = 0, 8, 16, 24
COL_AQ, COL_AK, COL_AV = 32, 40, 48
COLK_AK, COLK_GHG, COLK_GATT = 5, 7, 9

HG_CHUNK = 128
HG_SUB = 8
HG_HEADS_PER_STEP = 2
MOBA_GROUP = 8
VMEM_LIMIT = 56 * 1024 * 1024
NEG_INF = float("-inf")


def _bucket_thresholds():
    n = np.arange(MAX_EXACT, 4 * REL_MAX_DIST, dtype=np.float64)
    large = MAX_EXACT + (np.log(n / MAX_EXACT) / math.log(REL_MAX_DIST / MAX_EXACT)
                         * (N_BUCKETS - MAX_EXACT)).astype(np.int64)
    large = np.minimum(large, N_BUCKETS - 1)
    return [int(n[np.argmax(large >= b)]) for b in range(MAX_EXACT + 1, N_BUCKETS)]


BUCKET_THRESHOLDS = _bucket_thresholds()


def _cparams(*sem):
    return pltpu.CompilerParams(dimension_semantics=sem, vmem_limit_bytes=VMEM_LIMIT)


def _silu(x):
    return x * jax.nn.sigmoid(x)


def _dot(a, b):
    return jnp.dot(a, b, preferred_element_type=F32)


def _dot_nt(a, b, precision=None):
    return lax.dot_general(a, b, (((1,), (1,)), ((), ())), precision=precision,
                           preferred_element_type=F32)


def _norm_matmul_kernel(x_ref, g_ref, w_ref, o_ref, h_ref):
    @pl.when(pl.program_id(1) == 0)
    def _():
        x = x_ref[...]
        ms = jnp.mean(x * x, axis=-1, keepdims=True)
        h_ref[...] = (x * lax.rsqrt(ms + EPS) * g_ref[...]).astype(BF16)

    o_ref[...] = _dot(h_ref[...], w_ref[...])


def _norm_matmul(x, gain, w_bf16, tm, tn):
    m, k = x.shape
    n = w_bf16.shape[1]
    return pl.pallas_call(
        _norm_matmul_kernel,
        grid=(m // tm, n // tn),
        in_specs=[pl.BlockSpec((tm, k), lambda i, j: (i, 0)),
                  pl.BlockSpec((1, k), lambda i, j: (0, 0)),
                  pl.BlockSpec((k, tn), lambda i, j: (0, j))],
        out_specs=pl.BlockSpec((tm, tn), lambda i, j: (i, j)),
        out_shape=jax.ShapeDtypeStruct((m, n), F32),
        scratch_shapes=[pltpu.VMEM((tm, k), BF16)],
        compiler_params=_cparams("parallel", "arbitrary"),
        name="norm_in_proj",
    )(x, gain.reshape(1, k), w_bf16)


def _forget_lower_bound(lbraw):
    e = jnp.exp(lbraw - jnp.max(lbraw, axis=0, keepdims=True))
    return e[0:1] / jnp.sum(e, axis=0, keepdims=True)


def _hgrn_seq_kernel(hq_ref, hf_ref, hi_ref, og_ref, lbraw_ref, gn_ref, o_ref, s_out_ref,
                     st_ref, q_s, k_s, b_s, v_s, acc_s, *, chunk, heads):
    c = pl.program_id(1)

    @pl.when(c == 0)
    def _():
        st_ref[...] = jnp.zeros_like(st_ref)

    row = lax.broadcasted_iota(jnp.int32, (chunk, HG_DK), 0)
    rr = lax.broadcasted_iota(jnp.int32, (chunk, chunk), 0)
    cc = lax.broadcasted_iota(jnp.int32, (chunk, chunk), 1)
    sub = lax.broadcasted_iota(jnp.int32, (HG_SUB, HG_DK), 0)

    for hh in range(heads):
        cols = slice(hh * 128, (hh + 1) * 128)
        lb = _forget_lower_bound(lbraw_ref[:, cols])
        q = _silu(hq_ref[:, cols])
        f = lb + (1.0 - lb) * jax.nn.sigmoid(hf_ref[:, cols])
        k = 1.0 - f
        v = hi_ref[:, cols]

        b = jnp.log(f)
        shift = 1
        while shift < chunk:
            b = b + jnp.where(row >= shift, pltpu.roll(b, shift, axis=0), 0.0)
            shift *= 2

        q_s[hh] = q
        k_s[hh] = k
        b_s[hh] = b
        v_s[hh] = v

        st = st_ref[hh]
        o = _dot_nt((q * jnp.exp(b)).astype(BF16), st.astype(BF16))

        a = jnp.zeros((chunk, chunk), F32)
        hs = HG_SUB
        while hs < chunk:
            blk = 2 * hs
            ref_rows = jnp.concatenate(
                [jnp.broadcast_to(b_s[hh, m0 + hs - 1:m0 + hs, :], (blk, HG_DK)) for m0 in range(0, chunk, blk)],
                axis=0)
            second = (row & (blk - 1)) >= hs
            qd = jnp.where(second, q * jnp.exp(jnp.minimum(b - ref_rows, 0.0)), 0.0)
            kd = jnp.where(second, 0.0, k * jnp.exp(jnp.minimum(ref_rows - b, 0.0)))
            a_l = _dot_nt(qd.astype(BF16), kd.astype(BF16))
            sh = int(math.log2(blk))
            a = a + jnp.where((rr >> sh) == (cc >> sh), a_l, 0.0)
            hs = blk
        o = o + _dot(a.astype(BF16), v.astype(BF16))

        for r0 in range(0, chunk, HG_SUB):
            qi = q_s[hh, r0:r0 + HG_SUB, :]
            bi = b_s[hh, r0:r0 + HG_SUB, :]
            acc = jnp.zeros((HG_SUB, HG_DV), F32)
            for s in range(HG_SUB):
                ks = k_s[hh, r0 + s:r0 + s + 1, :]
                bs = b_s[hh, r0 + s:r0 + s + 1, :]
                vs = v_s[hh, r0 + s:r0 + s + 1, :]
                e = jnp.where(sub >= s, jnp.exp(bi - bs), 0.0)
                w = jnp.sum(qi * ks * e, axis=-1, keepdims=True)
                acc = acc + w * vs
            acc_s[hh, r0:r0 + HG_SUB, :] = acc
        o = o + acc_s[hh]

        b_last = b_s[hh, chunk - 1:chunk, :]
        kd = k * jnp.exp(b_last - b)
        st_new = st * jnp.exp(b_last) + _dot(v.T.astype(BF16), kd.astype(BF16))
        st_ref[hh] = st_new

        ms = jnp.mean(o * o, axis=-1, keepdims=True)
        o_ref[:, cols] = (o * lax.rsqrt(ms + EPS) * gn_ref[...] * _silu(og_ref[:, cols])).astype(o_ref.dtype)

    @pl.when(c == pl.num_programs(1) - 1)
    def _():
        for hh in range(heads):
            s_out_ref[hh] = st_ref[hh].T


def _hgrn_seq(proj, hg_lb, gn, chunk=HG_CHUNK, heads=HG_HEADS_PER_STEP):
    t = proj.shape[0]
    w = 128 * heads
    blk = lambda off: pl.BlockSpec((chunk, w), lambda h, c, off=off: (c, off // heads + h))
    return pl.pallas_call(
        functools.partial(_hgrn_seq_kernel, chunk=chunk, heads=heads),
        grid=(H_HG // heads, t // chunk),
        in_specs=[blk(COL_HQ), blk(COL_HF), blk(COL_HI), blk(COL_OG),
                  pl.BlockSpec((hg_lb.shape[0], w), lambda h, c: (0, h)),
                  pl.BlockSpec((1, HG_DV), lambda h, c: (0, 0))],
        out_specs=[pl.BlockSpec((chunk, w), lambda h, c: (c, h)),
                   pl.BlockSpec((heads, HG_DK, HG_DV), lambda h, c: (h, 0, 0))],
        out_shape=[jax.ShapeDtypeStruct((t, HG_VW), BF16),
                   jax.ShapeDtypeStruct((H_HG, HG_DK, HG_DV), F32)],
        scratch_shapes=[pltpu.VMEM((heads, HG_DV, HG_DK), F32)] + [pltpu.VMEM((heads, chunk, 128), F32)] * 5,
        compiler_params=_cparams("parallel", "arbitrary"),
        name="hgrn_seq",
    )(proj, proj, proj, proj, hg_lb, gn.reshape(1, HG_DV))


def _hgrn_step_kernel(hqc_ref, hfc_ref, hi_ref, og_ref, lbc_ref, gn_ref, s_ref, o_ref, s_out_ref):
    lbraw = lbc_ref[...]
    e = jnp.exp(lbraw - jnp.max(lbraw, axis=0, keepdims=True))
    lb = e[0] / jnp.sum(e, axis=0)
    q = _silu(hqc_ref[...])
    f = lb + (1.0 - lb) * jax.nn.sigmoid(hfc_ref[...])
    k = 1.0 - f
    v = hi_ref[...]
    s_new = f * s_ref[...] + k * v
    s_out_ref[...] = s_new
    o = jnp.sum(s_new * q, axis=1, keepdims=True)
    ms = jnp.mean(o * o, axis=-1, keepdims=True)
    o_ref[...] = (o * lax.rsqrt(ms + EPS) * gn_ref[...] * _silu(og_ref[...])).astype(o_ref.dtype)


def _hgrn_step(proj, state, hg_lb, gn):
    nb = proj.shape[0]
    col = lambda a: a.reshape(nb, H_HG, HG_DK, 1)
    rowv = lambda a: a.reshape(nb, H_HG, 1, HG_DV)
    hq = col(proj[:, 0:HG_W])
    hf = col(proj[:, HG_W:2 * HG_W])
    hi = rowv(proj[:, 2 * HG_W:2 * HG_W + HG_VW])
    og = rowv(proj[:, 2 * HG_W + HG_VW:2 * HG_W + 2 * HG_VW])
    nl = hg_lb.shape[0]
    cspec = pl.BlockSpec((None, H_HG, HG_DK, 1), lambda b: (b, 0, 0, 0))
    rspec = pl.BlockSpec((None, H_HG, 1, HG_DV), lambda b: (b, 0, 0, 0))
    sspec = pl.BlockSpec((None, H_HG, HG_DK, HG_DV), lambda b: (b, 0, 0, 0))
    return pl.pallas_call(
        _hgrn_step_kernel,
        grid=(nb,),
        in_specs=[cspec, cspec, rspec, rspec,
                  pl.BlockSpec((nl, H_HG, HG_DK, 1), lambda b: (0, 0, 0, 0)),
                  pl.BlockSpec((1, 1, HG_DV), lambda b: (0, 0, 0)),
                  sspec],
        out_specs=[rspec, sspec],
        out_shape=[jax.ShapeDtypeStruct((nb, H_HG, 1, HG_DV), BF16),
                   jax.ShapeDtypeStruct((nb, H_HG, HG_DK, HG_DV), F32)],
        compiler_params=_cparams("parallel"),
        name="hgrn_step",
    )(hq, hf, hi, og, hg_lb.reshape(nl, H_HG, HG_DK, 1), gn.reshape(1, 1, HG_DV), state)


BUCKET_STARTS = list(range(1, MAX_EXACT + 1)) + BUCKET_THRESHOLDS


def _bias_from_rel(rel, lo, hi, tab_ref, h):
    first = sum(1 for start in BUCKET_STARTS if start <= lo)
    out = jnp.full(rel.shape, tab_ref[first, h], F32)
    for bkt in range(first + 1, N_BUCKETS):
        if BUCKET_STARTS[bkt - 1] <= hi:
            out = jnp.where(rel >= BUCKET_STARTS[bkt - 1], tab_ref[bkt, h], out)
    return jnp.where(rel >= 0, out, NEG_INF) if lo < 0 else out


def _bias_tiles_kernel(tab_ref, o_ref, *, base, tile_step, row_step, col_step, scale):
    h = pl.program_id(0)
    n_tiles, rows, cols = o_ref.shape
    r = lax.broadcasted_iota(jnp.int32, (rows, cols), 0)
    c = lax.broadcasted_iota(jnp.int32, (rows, cols), 1)
    in_tile = r * row_step + c * col_step
    spans = [row_step * (rows - 1), col_step * (cols - 1)]
    for t in range(n_tiles):
        off = base + t * tile_step
        lo = off + sum(min(sp, 0) for sp in spans)
        hi = off + sum(max(sp, 0) for sp in spans)
        bias = _bias_from_rel(off + in_tile, lo, hi, tab_ref, h)
        o_ref[t] = bias if scale == 1.0 else bias * scale


def _bias_tiles(rel_bias, n_tiles, rows, cols, base, tile_step, row_step, col_step, scale=1.0):
    return pl.pallas_call(
        functools.partial(_bias_tiles_kernel, base=base, tile_step=tile_step, row_step=row_step,
                          col_step=col_step, scale=scale),
        grid=(H_ATT,),
        in_specs=[pl.BlockSpec(memory_space=pltpu.SMEM)],
        out_specs=pl.BlockSpec((None, n_tiles, rows, cols), lambda h: (h, 0, 0, 0)),
        out_shape=jax.ShapeDtypeStruct((H_ATT, n_tiles, rows, cols), F32),
        compiler_params=_cparams("parallel"),
        name="rel_bias_tiles",
    )(rel_bias)


def _block_mean_kernel(k_ref, o_ref):
    for blk in range(o_ref.shape[0]):
        o_ref[blk] = jnp.mean(k_ref[blk * MOBA_BLOCK:(blk + 1) * MOBA_BLOCK, :], axis=0, keepdims=True)


def _block_mean(proj, blocks_per_step=4):
    t = proj.shape[0]
    nb = t // MOBA_BLOCK
    assert nb % blocks_per_step == 0
    out = pl.pallas_call(
        _block_mean_kernel,
        grid=(nb // blocks_per_step,),
        in_specs=[pl.BlockSpec((blocks_per_step * MOBA_BLOCK, ATT_W), lambda n: (n, COLK_AK))],
        out_specs=pl.BlockSpec((blocks_per_step, 1, ATT_W), lambda n: (n, 0, 0)),
        out_shape=jax.ShapeDtypeStruct((nb, 1, ATT_W), F32),
        compiler_params=_cparams("parallel"),
        name="moba_block_mean",
    )(proj)
    return out.reshape(nb, ATT_W)


def _top_blocks(s, n_valid, axis):
    n = float(s.shape[axis])
    idx = lax.broadcasted_iota(jnp.int32, s.shape, axis).astype(F32)
    s = jnp.where(idx < n_valid, s, NEG_INF)
    sel = jnp.zeros(s.shape, F32)
    for _ in range(MOBA_TOPK):
        m = jnp.max(s, axis=axis, keepdims=True)
        first = jnp.min(jnp.where(s == m, idx, n), axis=axis, keepdims=True)
        pick = jnp.logical_and(idx == first, m > NEG_INF)
        sel = jnp.where(pick, 1.0, sel)
        s = jnp.where(pick, NEG_INF, s)
    return sel


def _moba_pen_kernel(tab_ref, q_ref, km_ref, pen_ref, penfar_ref):
    h = pl.program_id(0)
    nblk = km_ref.shape[0]
    tq = q_ref.shape[0]
    scores = _dot_nt(km_ref[...].astype(BF16), (q_ref[...] * ATT_SCALE).astype(BF16))
    pos = pl.program_id(1) * tq + lax.broadcasted_iota(jnp.int32, (1, tq), 1)
    own = pos // MOBA_BLOCK
    blk_id = lax.broadcasted_iota(jnp.int32, scores.shape, 0)
    chosen = _top_blocks(scores, own, axis=0) > 0.5
    pen_ref[0:nblk, :] = jnp.where(jnp.logical_or(chosen, blk_id == own), 0.0, NEG_INF)
    pen_ref[nblk:, :] = jnp.full((pen_ref.shape[0] - nblk, tq), NEG_INF, F32)
    far_bias = tab_ref[N_BUCKETS - 1, h] * LOG2E
    penfar_ref[...] = jnp.where(jnp.logical_and(chosen, blk_id <= own - MOBA_GROUP), far_bias, NEG_INF)


def _moba_pen(proj, kmean, rel_bias, tq=1024):
    t = proj.shape[0]
    nblk = t // MOBA_BLOCK
    return pl.pallas_call(
        _moba_pen_kernel,
        grid=(H_ATT, t // tq),
        in_specs=[pl.BlockSpec(memory_space=pltpu.SMEM),
                  pl.BlockSpec((tq, ATT_DH), lambda h, n: (n, COL_AQ + h)),
                  pl.BlockSpec((nblk, ATT_DH), lambda h, n: (0, h))],
        out_specs=[pl.BlockSpec((None, nblk + 8, tq), lambda h, n: (h, 0, n)),
                   pl.BlockSpec((None, nblk, tq), lambda h, n: (h, 0, n))],
        out_shape=[jax.ShapeDtypeStruct((H_ATT, nblk + 8, t), F32),
                   jax.ShapeDtypeStruct((H_ATT, nblk, t), F32)],
        compiler_params=_cparams("parallel", "parallel"),
        name="moba_block_masks",
    )(rel_bias, proj, kmean)


def _stream_page_sums(pt_ref, pages_hbm, sums_ref, pbuf, psem, step, n_steps, pages_per_block):
    n_pg = pbuf.shape[1]
    total = pt_ref.shape[0]
    slot = step % 2

    def copies(st, sl):
        return [pltpu.make_async_copy(pages_hbm.at[pt_ref[jnp.minimum(st * n_pg + p, total - 1)]],
                                      pbuf.at[sl, p], psem.at[sl, p]) for p in range(n_pg)]

    @pl.when(step == 0)
    def _():
        for cp in copies(step, slot):
            cp.start()

    @pl.when(step < n_steps)
    def _():
        for cp in copies(step, slot):
            cp.wait()

    @pl.when(step + 1 < n_steps)
    def _():
        for cp in copies(step + 1, 1 - slot):
            cp.start()

    def reduce_pages():
        for blk in range(n_pg // pages_per_block):
            tot = pbuf[slot, blk * pages_per_block].sum(axis=0)
            for pg in range(1, pages_per_block):
                tot = tot + pbuf[slot, blk * pages_per_block + pg].sum(axis=0)
            sums_ref[blk] = tot

    return reduce_pages


def _moba_seq_kernel(*refs, n_far, stream):
    if stream:
        n_stream_steps, pages_per_block = stream
        (pt_ref, q_ref, k_ref, v_ref, bias_ref, pen_ref, penfar_ref, pages_hbm, o_ref, sums_ref,
         kb_s, vt_s, acc_s, s_s, pbuf, psem) = refs
        reduce_pages = _stream_page_sums(
            pt_ref, pages_hbm, sums_ref, pbuf, psem,
            pl.program_id(0) * pl.num_programs(1) + pl.program_id(1), n_stream_steps, pages_per_block)
    else:
        q_ref, k_ref, v_ref, bias_ref, pen_ref, penfar_ref, o_ref, kb_s, vt_s, acc_s, s_s = refs
        reduce_pages = None
    i = pl.program_id(1)
    nblk = vt_s.shape[0]

    @pl.when(i == 0)
    def _():
        kb_s[...] = k_ref[...].astype(BF16)

        def xpose(j, carry):
            r0 = pl.multiple_of(j * MOBA_BLOCK, MOBA_BLOCK)
            vt_s[j] = v_ref[pl.ds(r0, MOBA_BLOCK), :].T.astype(BF16)
            return carry

        lax.fori_loop(0, nblk, xpose, 0)

    if reduce_pages is not None:
        reduce_pages()

    tq = q_ref.shape[0]
    grp = MOBA_GROUP
    qb = (q_ref[...] * (ATT_SCALE * LOG2E)).astype(BF16)
    acc_s[...] = jnp.zeros_like(acc_s)

    def logits(j):
        c0 = pl.multiple_of(j * MOBA_BLOCK, MOBA_BLOCK)
        return _dot_nt(kb_s[pl.ds(c0, MOBA_BLOCK), :], qb)

    def keep_max(smax, s):
        cur = jnp.max(s, axis=0, keepdims=True)
        return cur if smax is None else jnp.maximum(smax, cur)

    smax = None
    for u in range(grp):
        j = jnp.maximum(i - u, 0)
        pen_row = jnp.where(u <= i, j, nblk)
        s = logits(j) + bias_ref[min(u, n_far)] + pen_ref[pl.ds(pen_row, 1), :]
        s_s[0, u] = s
        smax = keep_max(smax, s)

    def softmax_update(slot, m_old, l_old, smax, block_of):
        mg = jnp.maximum(m_old, smax)
        alpha = jnp.exp2(m_old - mg)
        l_new = alpha * l_old
        pv = None
        for u in range(grp):
            p = jnp.exp2(s_s[slot, u] - mg)
            l_new = l_new + jnp.sum(p, axis=0, keepdims=True)
            d = _dot(vt_s[block_of(u)], p.astype(BF16))
            pv = d if pv is None else pv + d
        acc_s[...] = alpha * acc_s[...] + pv
        return mg, l_new

    def trip_with_slots(t, carry, slot_prev):
        m_old, l_old, smax_prev = carry
        smax_new = None
        for u in range(grp):
            s = logits(t * grp + u) + penfar_ref[pl.ds(t * grp + u, 1), :]
            s_s[1 - slot_prev, u] = s
            smax_new = keep_max(smax_new, s)
        prev = lambda u: jnp.where(t == 0, jnp.maximum(i - u, 0), (t - 1) * grp + u)
        m_new, l_new = softmax_update(slot_prev, m_old, l_old, smax_prev, prev)
        return m_new, l_new, smax_new

    def trip(t, carry):
        return lax.cond(t % 2 == 0, lambda c: trip_with_slots(t, c, 0), lambda c: trip_with_slots(t, c, 1), carry)

    n_far_groups = i // grp
    init = (jnp.full((1, tq), NEG_INF, F32), jnp.zeros((1, tq), F32), smax)
    m, l, smax = lax.fori_loop(0, n_far_groups, trip, init)
    last = lambda u: jnp.where(n_far_groups == 0, jnp.maximum(i - u, 0), (n_far_groups - 1) * grp + u)
    m, l = softmax_update(n_far_groups % 2, m, l, smax, last)
    o_ref[...] = (acc_s[...] / l).T.astype(o_ref.dtype)


def _moba_seq(proj, rel_bias, side_pool=None, side_page_table=None):
    t = proj.shape[0]
    nq = t // MOBA_BLOCK
    kmean = _block_mean(proj)
    n_far = -(-(BUCKET_THRESHOLDS[-1] + MOBA_BLOCK) // MOBA_BLOCK)
    assert MOBA_GROUP >= n_far and nq % MOBA_GROUP == 0
    bias = _bias_tiles(rel_bias, n_far + 1, MOBA_BLOCK, MOBA_BLOCK, 0, MOBA_BLOCK, -1, 1, scale=LOG2E)
    pen, penfar = _moba_pen(proj, kmean, rel_bias)
    n_steps = H_ATT * nq
    in_specs = [pl.BlockSpec((MOBA_BLOCK, ATT_DH), lambda h, i, *_: (i, COL_AQ + h)),
                pl.BlockSpec((t, ATT_DH), lambda h, i, *_: (0, COL_AK + h)),
                pl.BlockSpec((t, ATT_DH), lambda h, i, *_: (0, COL_AV + h)),
                pl.BlockSpec((None, n_far + 1, MOBA_BLOCK, MOBA_BLOCK), lambda h, i, *_: (h, 0, 0, 0)),
                pl.BlockSpec((None, nq + 8, MOBA_BLOCK), lambda h, i, *_: (h, 0, i)),
                pl.BlockSpec((None, nq, MOBA_BLOCK), lambda h, i, *_: (h, 0, i))]
    out_specs = [pl.BlockSpec((MOBA_BLOCK, ATT_DH), lambda h, i, *_: (i, h))]
    out_shape = [jax.ShapeDtypeStruct((t, ATT_W), BF16)]
    scratch = [pltpu.VMEM((t, ATT_DH), BF16), pltpu.VMEM((nq, ATT_DH, MOBA_BLOCK), BF16),
               pltpu.VMEM((ATT_DH, MOBA_BLOCK), F32),
               pltpu.VMEM((2, MOBA_GROUP, MOBA_BLOCK, MOBA_BLOCK), F32)]
    operands = [proj, proj, proj, bias, pen, penfar]
    stream = None
    prefetch = []
    if side_pool is not None:
        pt = side_page_table.reshape(-1)
        page_rows = side_pool.shape[1]
        ppb = MOBA_BLOCK // page_rows
        assert ppb * page_rows == MOBA_BLOCK and pt.shape[0] % ppb == 0
        n_pg = ppb * -(-pt.shape[0] // (ppb * n_steps))
        stream = (-(-pt.shape[0] // n_pg), ppb)
        in_specs.append(pl.BlockSpec(memory_space=pl.ANY))
        out_specs.append(pl.BlockSpec((None, n_pg // ppb, H_ATT, ATT_DH), lambda h, i, *_: (h * nq + i, 0, 0, 0)))
        out_shape.append(jax.ShapeDtypeStruct((n_steps, n_pg // ppb, H_ATT, ATT_DH), F32))
        scratch += [pltpu.VMEM((2, n_pg) + side_pool.shape[1:], F32), pltpu.SemaphoreType.DMA((2, n_pg))]
        operands.append(side_pool)
        prefetch = [pt]
    outs = pl.pallas_call(
        functools.partial(_moba_seq_kernel, n_far=n_far, stream=stream),
        grid_spec=pltpu.PrefetchScalarGridSpec(
            num_scalar_prefetch=len(prefetch), grid=(H_ATT, nq), in_specs=in_specs, out_specs=out_specs,
            scratch_shapes=scratch),
        out_shape=out_shape,
        compiler_params=_cparams("arbitrary", "arbitrary"),
        name="moba_seq",
    )(*prefetch, *operands)
    if side_pool is None:
        return outs[0], None
    return outs[0], outs[1].reshape(-1, H_ATT, ATT_DH)[:pt.shape[0] // ppb]


SELECT_PAGES_PER_STEP = 8
FFN_CHUNK = 256
MOBA_STEP_SLOTS = 4


def _moba_select_kernel(ksum_ref, q_ref, sel_ref):
    kmean = ksum_ref[...] / MOBA_BLOCK
    prod = kmean.astype(BF16).astype(F32) * (q_ref[...] * ATT_SCALE).astype(BF16).astype(F32)[None]
    s = jnp.sum(prod, axis=-1, keepdims=True)
    nb = s.shape[0]
    mask = _top_blocks(s, nb, axis=0)
    idx = lax.broadcasted_iota(jnp.int32, s.shape, 0).astype(F32)
    for r in range(MOBA_TOPK):
        first = jnp.min(jnp.where(mask > 0.5, idx, float(nb)), axis=0)
        sel_ref[r] = first.astype(jnp.int32)
        mask = jnp.where(idx == first[None], 0.0, mask)


def _moba_select(block_sums, q_heads):
    nb_seq, n_blocks = block_sums.shape[:2]
    assert n_blocks >= MOBA_TOPK
    return pl.pallas_call(
        _moba_select_kernel,
        grid=(nb_seq,),
        in_specs=[pl.BlockSpec((None, n_blocks, H_ATT, ATT_DH), lambda b: (b, 0, 0, 0)),
                  pl.BlockSpec((None, H_ATT, ATT_DH), lambda b: (b, 0, 0))],
        out_specs=pl.BlockSpec((None, MOBA_TOPK, H_ATT, 1), lambda b: (b, 0, 0, 0)),
        out_shape=jax.ShapeDtypeStruct((nb_seq, MOBA_TOPK, H_ATT, 1), jnp.int32),
        compiler_params=_cparams("parallel"),
        name="moba_select",
    )(block_sums, q_heads)


def _page_block_sums_kernel(pt_ref, *refs, pages_per_block):
    page_refs, o_ref = refs[:-1], refs[-1]
    for blk in range(len(page_refs) // pages_per_block):
        tot = page_refs[blk * pages_per_block][...].sum(axis=0)
        for pg in range(1, pages_per_block):
            tot = tot + page_refs[blk * pages_per_block + pg][...].sum(axis=0)
        o_ref[blk] = tot


def _page_block_sums(pool, page_table):
    nb_seq, n_pages = page_table.shape
    page_rows = pool.shape[1]
    ppb = MOBA_BLOCK // page_rows
    n_pg = SELECT_PAGES_PER_STEP
    assert ppb * page_rows == MOBA_BLOCK and n_pg % ppb == 0 and n_pages % n_pg == 0
    kspec = lambda p: pl.BlockSpec((None,) + pool.shape[1:],
                                   lambda b, n, pt, p=p: (pt[b * n_pages + n_pg * n + p], 0, 0, 0))
    return pl.pallas_call(
        functools.partial(_page_block_sums_kernel, pages_per_block=ppb),
        grid_spec=pltpu.PrefetchScalarGridSpec(
            num_scalar_prefetch=1,
            grid=(nb_seq, n_pages // n_pg),
            in_specs=[kspec(p) for p in range(n_pg)],
            out_specs=pl.BlockSpec((None, n_pg // ppb, H_ATT, ATT_DH), lambda b, n, pt: (b, n, 0, 0))),
        out_shape=jax.ShapeDtypeStruct((nb_seq, n_pages // ppb, H_ATT, ATT_DH), F32),
        compiler_params=_cparams("parallel", "arbitrary"),
        name="page_block_sums",
    )(page_table.reshape(-1), *([pool] * n_pg))


def _moba_step_kernel(pt_ref, sel_ref, q_ref, kn_ref, vn_ref, b0_ref, b1_ref, b2_ref, bown_ref, ck_hbm, cv_hbm,
                      o_ref, kbuf, vbuf, sem, *, n_pages):
    n_pg = MOBA_TOPK * 2
    b_refs = (b0_ref, b1_ref, b2_ref)
    n_slots = kbuf.shape[0]
    page_rows = kbuf.shape[2]
    b = pl.program_id(0)
    h = pl.program_id(1)
    n_heads = pl.num_programs(1)
    n_steps = pl.num_programs(0) * n_heads
    step = b * n_heads + h
    slot = step % n_slots

    def page_copies(bb, hh, sl):
        cps = []
        for r in range(MOBA_TOPK):
            blk = sel_ref[(bb * MOBA_TOPK + r) * n_heads + hh]
            for half in range(2):
                page = pt_ref[bb * n_pages + 2 * blk + half]
                idx = 2 * r + half
                cps.append(pltpu.make_async_copy(ck_hbm.at[page, :, hh, :], kbuf.at[sl, idx], sem.at[sl, idx]))
                cps.append(pltpu.make_async_copy(cv_hbm.at[page, :, hh, :], vbuf.at[sl, idx], sem.at[sl, n_pg + idx]))
        return cps

    def start_step(st):
        for cp in page_copies(st // n_heads, st % n_heads, st % n_slots):
            cp.start()

    @pl.when(step == 0)
    def _():
        for ahead in range(n_slots - 1):
            @pl.when(ahead < n_steps)
            def _():
                start_step(jnp.int32(ahead))

    @pl.when(step + n_slots - 1 < n_steps)
    def _():
        start_step(step + n_slots - 1)

    for cp in page_copies(b, h, slot):
        cp.wait()

    qs = q_ref[...] * ATT_SCALE
    eye = (lax.broadcasted_iota(jnp.int32, (page_rows, page_rows), 0)
           == lax.broadcasted_iota(jnp.int32, (page_rows, page_rows), 1))
    logits = []
    for r in range(MOBA_TOPK):
        bias = b_refs[r][...]
        for half in range(2):
            lg = jnp.sum(kbuf[slot, 2 * r + half] * qs, axis=-1, keepdims=True)
            brow = bias[:, half * page_rows:(half + 1) * page_rows]
            bcol = jnp.sum(jnp.where(eye, brow, 0.0), axis=-1, keepdims=True)
            logits.append(lg + bcol)
    l_own = jnp.sum(qs * kn_ref[...], axis=-1, keepdims=True) + bown_ref[:, 0:1]
    m = l_own
    for lg in logits:
        m = jnp.maximum(m, jnp.max(lg, axis=0, keepdims=True))
    p_own = jnp.exp(l_own - m)
    den = p_own
    acc = p_own * vn_ref[...]
    for idx, lg in enumerate(logits):
        p = jnp.exp(lg - m)
        den = den + jnp.sum(p, axis=0, keepdims=True)
        acc = acc + jnp.sum(p * vbuf[slot, idx], axis=0, keepdims=True)
    o_ref[...] = (acc / den).astype(o_ref.dtype)


def _moba_step(proj, block_sums, cache_k4, cache_v4, page_table, rel_bias):
    nb_seq, n_pages = page_table.shape
    page_rows = cache_k4.shape[1]
    ppb = MOBA_BLOCK // page_rows
    assert ppb == 2
    n_blocks = n_pages // ppb
    n_past = n_pages * page_rows
    head_rows = lambda a: a.reshape(nb_seq, H_ATT, 1, ATT_DH)
    aq = proj[:, COL_AQ * 128:COL_AQ * 128 + ATT_W]
    ak = proj[:, COL_AK * 128:COL_AK * 128 + ATT_W]
    av = proj[:, COL_AV * 128:COL_AV * 128 + ATT_W]
    sel = _moba_select(block_sums.reshape(nb_seq, n_blocks, H_ATT, ATT_DH), aq.reshape(nb_seq, H_ATT, ATT_DH))
    bias = _bias_tiles(rel_bias, n_blocks + 1, 1, MOBA_BLOCK, n_past, -MOBA_BLOCK, 0, -1)

    def bias_spec(r):
        return pl.BlockSpec((None, None, 1, MOBA_BLOCK),
                            lambda b, h, pt, sl: (h, sl[(b * MOBA_TOPK + r) * H_ATT + h], 0, 0))

    row_spec = pl.BlockSpec((None, None, 1, ATT_DH), lambda b, h, pt, sl: (b, h, 0, 0))
    hbm = pl.BlockSpec(memory_space=pl.ANY)
    n_pg = 2 * MOBA_TOPK
    out = pl.pallas_call(
        functools.partial(_moba_step_kernel, n_pages=n_pages),
        grid_spec=pltpu.PrefetchScalarGridSpec(
            num_scalar_prefetch=2,
            grid=(nb_seq, H_ATT),
            in_specs=[row_spec, row_spec, row_spec] + [bias_spec(r) for r in range(MOBA_TOPK)]
                     + [pl.BlockSpec((None, None, 1, MOBA_BLOCK), lambda b, h, pt, sl: (h, n_blocks, 0, 0)),
                        hbm, hbm],
            out_specs=row_spec,
            scratch_shapes=[pltpu.VMEM((MOBA_STEP_SLOTS, n_pg, page_rows, ATT_DH), F32),
                            pltpu.VMEM((MOBA_STEP_SLOTS, n_pg, page_rows, ATT_DH), F32),
                            pltpu.SemaphoreType.DMA((MOBA_STEP_SLOTS, 2 * n_pg))]),
        out_shape=jax.ShapeDtypeStruct((nb_seq, H_ATT, 1, ATT_DH), BF16),
        compiler_params=_cparams("arbitrary", "arbitrary"),
        name="moba_step",
    )(page_table.reshape(-1), sel.reshape(-1), head_rows(aq), head_rows(ak), head_rows(av),
      bias, bias, bias, bias, cache_k4, cache_v4)
    return out.reshape(nb_seq, ATT_W)


def _merge_out_kernel(oh_ref, oa_ref, gh0_ref, gh1_ref, ga0_ref, ga1_ref, wh_ref, wa_ref, wo_ref, x_ref, g_ref,
                      gnext_ref, o_ref, h_ref):
    half = D_MODEL // 2
    br_h = _dot(oh_ref[...], wh_ref[...])
    br_a = _dot(oa_ref[...], wa_ref[...])
    merged = jnp.concatenate(
        [(jax.nn.sigmoid(gh_ref[...]) * br_h[:, c * half:(c + 1) * half]
          + jax.nn.sigmoid(ga_ref[...]) * br_a[:, c * half:(c + 1) * half]).astype(BF16)
         for c, (gh_ref, ga_ref) in enumerate(((gh0_ref, ga0_ref), (gh1_ref, ga1_ref)))], axis=1)
    z = _dot(merged, wo_ref[...])
    ms = jnp.mean(z * z, axis=-1, keepdims=True)
    x1 = x_ref[...] + z * lax.rsqrt(ms + EPS) * g_ref[...]
    o_ref[...] = x1
    ms1 = jnp.mean(x1 * x1, axis=-1, keepdims=True)
    h_ref[...] = (x1 * lax.rsqrt(ms1 + EPS) * gnext_ref[...]).astype(h_ref.dtype)


def _merge_out(o_hg, o_att, proj, w_bh, w_bm, w_out, x, gain, gain_next, tm):
    m = x.shape[0]
    half = D_MODEL // 2
    assert half == 1024
    resident = pl.Buffered(1)
    vec = pl.BlockSpec((1, D_MODEL), lambda i: (0, 0), pipeline_mode=resident)
    row = pl.BlockSpec((tm, D_MODEL), lambda i: (i, 0))
    gate = lambda blk: pl.BlockSpec((tm, half), lambda i, blk=blk: (i, blk))
    return pl.pallas_call(
        _merge_out_kernel,
        grid=(m // tm,),
        in_specs=[pl.BlockSpec((tm, HG_VW), lambda i: (i, 0)), pl.BlockSpec((tm, ATT_W), lambda i: (i, 0)),
                  gate(COLK_GHG), gate(COLK_GHG + 1), gate(COLK_GATT), gate(COLK_GATT + 1),
                  pl.BlockSpec((HG_VW, D_MODEL), lambda i: (0, 0), pipeline_mode=resident),
                  pl.BlockSpec((ATT_W, D_MODEL), lambda i: (0, 0), pipeline_mode=resident),
                  pl.BlockSpec((D_MODEL, D_MODEL), lambda i: (0, 0), pipeline_mode=resident),
                  row, vec, vec],
        out_specs=[row, row],
        out_shape=[jax.ShapeDtypeStruct((m, D_MODEL), F32), jax.ShapeDtypeStruct((m, D_MODEL), BF16)],
        compiler_params=_cparams("parallel"),
        name="merge_out_proj_residual",
    )(o_hg, o_att, proj, proj, proj, proj, w_bh, w_bm, w_out, x, gain.reshape(1, D_MODEL),
      gain_next.reshape(1, D_MODEL))


def _ffn_up_kernel(h_ref, wg_ref, wu_ref, cw_ref, cb_ref, p0_ref, p1_ref, a_ref, g_out_ref, carry_s, *, seq):
    i = pl.program_id(0)
    j = pl.program_id(1)
    tm, tf = a_ref.shape
    if seq:
        @pl.when(i == 0)
        def _():
            carry_s[j, 0:1, :] = p0_ref[...]
            carry_s[j, 1:2, :] = p1_ref[...]

    top = lax.broadcasted_iota(jnp.int32, (8, FFN_CHUNK), 0)
    for c in range(tf // FFN_CHUNK):
        cols = slice(c * FFN_CHUNK, (c + 1) * FFN_CHUNK)
        g = _dot(h_ref[...], wg_ref[:, cols].astype(BF16))
        u = _dot(h_ref[...], wu_ref[:, cols].astype(BF16))
        if seq:
            c0 = carry_s[j, 0:1, cols]
            c1 = carry_s[j, 1:2, cols]
            r1 = pltpu.roll(g, 1, axis=0)
            r2 = pltpu.roll(g, 2, axis=0)
            prev1 = jnp.concatenate([jnp.where(top == 0, c1, r1[0:8]), r1[8:]], axis=0)
            prev2 = jnp.concatenate([jnp.where(top == 0, c0, jnp.where(top == 1, c1, r2[0:8])), r2[8:]], axis=0)
            carry_s[j, 0:2, cols] = g[tm - 2:tm, :]
            g_out_ref[:, cols] = g[tm - 2:tm, :]
        else:
            prev2 = p0_ref[:, cols]
            prev1 = p1_ref[:, cols]
            g_out_ref[:, cols] = g
        gc = cb_ref[:, cols] + cw_ref[0:1, cols] * prev2 + cw_ref[1:2, cols] * prev1 + cw_ref[2:3, cols] * g
        a_ref[:, cols] = (jax.nn.gelu(gc, approximate=True) * u).astype(a_ref.dtype)


def _ffn_down_kernel(a_ref, wd_ref, x_ref, g_ref, o_ref):
    y = _dot(a_ref[...], wd_ref[...])
    ms = jnp.mean(y * y, axis=-1, keepdims=True)
    o_ref[...] = x_ref[...] + y * lax.rsqrt(ms + EPS) * g_ref[...]


def _ffn(x, h, w_up, conv_w, conv_b, prev0, prev1, w_down, gpost, *, seq, tm_up, tm_down, tf=512):
    m = x.shape[0]
    nf = D_FF // tf
    assert CONV_W == 3 and nf * tf == D_FF
    prow = 1 if seq else tm_up
    pspec = pl.BlockSpec((prow, tf), (lambda i, j: (0, j)) if seq else (lambda i, j: (i, j)))
    if seq:
        gspec = pl.BlockSpec((None, CONV_W - 1, tf), lambda i, j: (i, 0, j))
        gshape = (m // tm_up, CONV_W - 1, D_FF)
    else:
        gspec = pl.BlockSpec((tm_up, tf), lambda i, j: (i, j))
        gshape = (m, D_FF)
    act, g_out = pl.pallas_call(
        functools.partial(_ffn_up_kernel, seq=seq),
        grid=(m // tm_up, nf),
        in_specs=[pl.BlockSpec((tm_up, D_MODEL), lambda i, j: (i, 0)),
                  pl.BlockSpec((D_MODEL, tf), lambda i, j: (0, j)),
                  pl.BlockSpec((D_MODEL, tf), lambda i, j: (0, nf + j)),
                  pl.BlockSpec((CONV_W, tf), lambda i, j: (0, j)),
                  pl.BlockSpec((1, tf), lambda i, j: (0, j)),
                  pspec, pspec],
        out_specs=[pl.BlockSpec((tm_up, tf), lambda i, j: (i, j)), gspec],
        out_shape=[jax.ShapeDtypeStruct((m, D_FF), BF16), jax.ShapeDtypeStruct(gshape, F32)],
        scratch_shapes=[pltpu.VMEM((nf, 8, tf), F32)],
        compiler_params=_cparams("arbitrary", "arbitrary"),
        name="conv_ffn_up",
    )(h, w_up, w_up, conv_w, conv_b.reshape(1, D_FF), prev0, prev1)
    resident = pl.Buffered(1)
    y = pl.pallas_call(
        _ffn_down_kernel,
        grid=(m // tm_down,),
        in_specs=[pl.BlockSpec((tm_down, D_FF), lambda i: (i, 0)),
                  pl.BlockSpec((D_FF, D_MODEL), lambda i: (0, 0), pipeline_mode=resident),
                  pl.BlockSpec((tm_down, D_MODEL), lambda i: (i, 0)),
                  pl.BlockSpec((1, D_MODEL), lambda i: (0, 0), pipeline_mode=resident)],
        out_specs=pl.BlockSpec((tm_down, D_MODEL), lambda i: (i, 0)),
        out_shape=jax.ShapeDtypeStruct((m, D_MODEL), F32),
        compiler_params=_cparams("parallel"),
        name="ffn_down_residual",
    )(act, w_down, x, gpost.reshape(1, D_MODEL))
    return y, (g_out[-1] if seq else g_out)


def kernel(x_prompt, x_sample, cache_k, cache_v, state_hgrn, state_ffn_conv, page_table, rel_bias, hg_lb,
           norm_mix_pre, norm_mix_post, norm_ffn_pre, norm_ffn_post, w_in, hg_out_norm, w_branch_hgrn,
           w_branch_moba, w_out, w_ffn_up, ffn_conv_w, ffn_conv_b, w_ffn_down):
    nb, t, _ = x_prompt.shape
    db = x_sample.shape[0]
    depth = w_in.shape[0]
    assert nb == 1 and depth == 1 and x_sample.shape[1] == 1
    l = 0
    w_in_b = w_in[l].astype(BF16)
    w_bh = w_branch_hgrn[l].astype(BF16)
    w_bm = w_branch_moba[l].astype(BF16)
    w_o = w_out[l].astype(BF16)
    w_up = w_ffn_up[l]
    w_dn = w_ffn_down[l].astype(BF16)

    n_pool, page_rows = cache_k.shape[1], cache_k.shape[2]
    ck = cache_k.reshape(depth * n_pool, page_rows, H_ATT, ATT_DH)
    cv = cache_v.reshape(depth * n_pool, page_rows, H_ATT, ATT_DH)

    xp = x_prompt.reshape(t, D_MODEL)
    proj = _norm_matmul(xp, norm_mix_pre[l], w_in_b, tm=1024, tn=1024)
    o_hg, s_new = _hgrn_seq(proj, hg_lb, hg_out_norm[l])
    o_att, cache_block_sums = _moba_seq(proj, rel_bias, side_pool=ck, side_page_table=page_table)
    x1, h2 = _merge_out(o_hg, o_att, proj, w_bh, w_bm, w_o, xp, norm_mix_post[l], norm_ffn_pre[l], tm=256)
    zero_row = jnp.zeros((1, D_FF), F32)
    yp, conv_p = _ffn(x1, h2, w_up, ffn_conv_w[l], ffn_conv_b[l], zero_row, zero_row, w_dn,
                      norm_ffn_post[l], seq=True, tm_up=1024, tm_down=256)
    kp = proj[:, COL_AK * 128:COL_AK * 128 + ATT_W].reshape(1, 1, t, H_ATT, ATT_DH)
    vp = proj[:, COL_AV * 128:COL_AV * 128 + ATT_W].reshape(1, 1, t, H_ATT, ATT_DH)

    xs = x_sample.reshape(db, D_MODEL)
    projs = _norm_matmul(xs, norm_mix_pre[l], w_in_b, tm=db, tn=1024)
    o_hg_s, s_new_s = _hgrn_step(projs, state_hgrn[l], hg_lb, hg_out_norm[l])
    o_att_s = _moba_step(projs, cache_block_sums, ck, cv, page_table, rel_bias)
    x1s, h2s = _merge_out(o_hg_s.reshape(db, HG_VW), o_att_s, projs, w_bh, w_bm, w_o, xs, norm_mix_post[l],
                          norm_ffn_pre[l], tm=db)
    buf = state_ffn_conv[l]
    ys, g_s = _ffn(x1s, h2s, w_up, ffn_conv_w[l], ffn_conv_b[l], buf[:, 0], buf[:, 1], w_dn,
                   norm_ffn_post[l], seq=False, tm_up=db, tm_down=db)
    ks = projs[:, COL_AK * 128:COL_AK * 128 + ATT_W].reshape(1, db, 1, H_ATT, ATT_DH)
    vs = projs[:, COL_AV * 128:COL_AV * 128 + ATT_W].reshape(1, db, 1, H_ATT, ATT_DH)
    conv_s = jnp.stack([buf[:, 1], g_s], axis=1)

    return (yp.reshape(1, t, D_MODEL), ys.reshape(db, 1, D_MODEL), kp, vp,
            s_new.reshape(1, 1, H_HG, HG_DK, HG_DV), conv_p.reshape(1, 1, CONV_W - 1, D_FF),
            ks, vs, s_new_s.reshape(1, db, H_HG, HG_DK, HG_DV), conv_s.reshape(1, db, CONV_W - 1, D_FF))
```

```python
import functools
import math

import numpy as np
import jax
import jax.numpy as jnp
from jax import lax
from jax.experimental import pallas as pl
from jax.experimental.pallas import tpu as pltpu

F32 = jnp.float32
BF16 = jnp.bfloat16

D_MODEL = 2048
H_HG = 8
HG_DK = 128
HG_DV = 128
HG_W = H_HG * HG_DK
HG_VW = H_HG * HG_DV
H_ATT = 8
ATT_DH = 128
ATT_W = H_ATT * ATT_DH
MOBA_BLOCK = 256
MOBA_TOPK = 3
N_BUCKETS = 32
MAX_EXACT = N_BUCKETS // 2
REL_MAX_DIST = 1024
D_FF = 5632
CONV_W = 3
EPS = 1e-6
N_IN = 2 * HG_W + 2 * HG_VW + 3 * ATT_W + 2 * D_MODEL
ATT_SCALE = ATT_DH ** -0.5
LOG2E = math.log2(math.e)

COL_HQ, COL_HF, COL_HI, COL_OG = 0, 8, 16, 24
COL_AQ, COL_AK, COL_AV = 32, 40, 48
COLK_AK, COLK_GHG, COLK_GATT = 5, 7, 9

HG_CHUNK = 128
HG_SUB = 8
HG_HEADS_PER_STEP = 8
MOBA_GROUP = 8
VMEM_LIMIT = 56 * 1024 * 1024
NEG_INF = float("-inf")


def _bucket_thresholds():
    n = np.arange(MAX_EXACT, 4 * REL_MAX_DIST, dtype=np.float64)
    large = MAX_EXACT + (np.log(n / MAX_EXACT) / math.log(REL_MAX_DIST / MAX_EXACT)
                         * (N_BUCKETS - MAX_EXACT)).astype(np.int64)
    large = np.minimum(large, N_BUCKETS - 1)
    return [int(n[np.argmax(large >= b)]) for b in range(MAX_EXACT + 1, N_BUCKETS)]


BUCKET_THRESHOLDS = _bucket_thresholds()


def _cparams(*sem):
    return pltpu.CompilerParams(dimension_semantics=sem, vmem_limit_bytes=VMEM_LIMIT)


def _silu(x):
    return x * jax.nn.sigmoid(x)


def _dot(a, b):
    return jnp.dot(a, b, preferred_element_type=F32)


def _dot_nt(a, b, precision=None):
    return lax.dot_general(a, b, (((1,), (1,)), ((), ())), precision=precision,
                           preferred_element_type=F32)


def _norm_matmul_kernel(x_ref, g_ref, w_ref, o_ref, h_ref):
    @pl.when(pl.program_id(1) == 0)
    def _():
        x = x_ref[...]
        ms = jnp.mean(x * x, axis=-1, keepdims=True)
        h_ref[...] = (x * lax.rsqrt(ms + EPS) * g_ref[...]).astype(BF16)

    o_ref[...] = _dot(h_ref[...], w_ref[...].astype(BF16))


def _norm_matmul(x, gain, w, tm, tn):
    m, k = x.shape
    n = w.shape[1]
    return pl.pallas_call(
        _norm_matmul_kernel,
        grid=(m // tm, n // tn),
        in_specs=[pl.BlockSpec((tm, k), lambda i, j: (i, 0)),
                  pl.BlockSpec((1, k), lambda i, j: (0, 0)),
                  pl.BlockSpec((k, tn), lambda i, j: (0, j))],
        out_specs=pl.BlockSpec((tm, tn), lambda i, j: (i, j)),
        out_shape=jax.ShapeDtypeStruct((m, n), F32),
        scratch_shapes=[pltpu.VMEM((tm, k), BF16)],
        compiler_params=_cparams("parallel", "arbitrary"),
        name="norm_in_proj",
    )(x, gain.reshape(1, k), w)


def _forget_lower_bound(lbraw):
    e = jnp.exp(lbraw - jnp.max(lbraw, axis=0, keepdims=True))
    return e[0:1] / jnp.sum(e, axis=0, keepdims=True)


def _hgrn_seq_kernel(hq_ref, hf_ref, hi_ref, og_ref, lbraw_ref, gn_ref, o_ref, s_out_ref,
                     st_ref, q_s, k_s, b_s, v_s, acc_s, *, chunk, heads):
    c = pl.program_id(1)

    @pl.when(c == 0)
    def _():
        st_ref[...] = jnp.zeros_like(st_ref)

    row = lax.broadcasted_iota(jnp.int32, (chunk, HG_DK), 0)
    rr = lax.broadcasted_iota(jnp.int32, (chunk, chunk), 0)
    cc = lax.broadcasted_iota(jnp.int32, (chunk, chunk), 1)
    sub = lax.broadcasted_iota(jnp.int32, (HG_SUB, HG_DK), 0)

    for hh in range(heads):
        cols = slice(hh * 128, (hh + 1) * 128)
        lb = _forget_lower_bound(lbraw_ref[:, cols])
        q = _silu(hq_ref[:, cols])
        f = lb + (1.0 - lb) * jax.nn.sigmoid(hf_ref[:, cols])
        k = 1.0 - f
        v = hi_ref[:, cols]

        b = jnp.log2(f)
        shift = 1
        while shift < chunk:
            b = b + jnp.where(row >= shift, pltpu.roll(b, shift, axis=0), 0.0)
            shift *= 2

        q_s[hh] = q
        k_s[hh] = k
        b_s[hh] = b
        v_s[hh] = v

        st = st_ref[hh]
        o = _dot_nt((q * jnp.exp2(b)).astype(BF16), st.astype(BF16))

        a = jnp.zeros((chunk, chunk), F32)
        hs = HG_SUB
        while hs < chunk:
            blk = 2 * hs
            ref_rows = jnp.concatenate(
                [jnp.broadcast_to(b_s[hh, m0 + hs - 1:m0 + hs, :], (blk, HG_DK)) for m0 in range(0, chunk, blk)],
                axis=0)
            second = (row & (blk - 1)) >= hs
            qd = jnp.where(second, q * jnp.exp2(jnp.minimum(b - ref_rows, 0.0)), 0.0)
            kd = jnp.where(second, 0.0, k * jnp.exp2(jnp.minimum(ref_rows - b, 0.0)))
            a_l = _dot_nt(qd.astype(BF16), kd.astype(BF16))
            sh = int(math.log2(blk))
            a = a + jnp.where((rr >> sh) == (cc >> sh), a_l, 0.0)
            hs = blk
        o = o + _dot(a.astype(BF16), v.astype(BF16))

        for r0 in range(0, chunk, HG_SUB):
            qi = q_s[hh, r0:r0 + HG_SUB, :]
            bi = b_s[hh, r0:r0 + HG_SUB, :]
            acc = jnp.zeros((HG_SUB, HG_DV), F32)
            for s in range(HG_SUB):
                ks = k_s[hh, r0 + s:r0 + s + 1, :]
                bs = b_s[hh, r0 + s:r0 + s + 1, :]
                vs = v_s[hh, r0 + s:r0 + s + 1, :]
                e = jnp.where(sub >= s, jnp.exp2(bi - bs), 0.0)
                w = jnp.sum(qi * ks * e, axis=-1, keepdims=True)
                acc = acc + w * vs
            acc_s[hh, r0:r0 + HG_SUB, :] = acc
        o = o + acc_s[hh]

        b_last = b_s[hh, chunk - 1:chunk, :]
        kd = k * jnp.exp2(b_last - b)
        st_new = st * jnp.exp2(b_last) + _dot(v.T.astype(BF16), kd.astype(BF16))
        st_ref[hh] = st_new

        ms = jnp.mean(o * o, axis=-1, keepdims=True)
        o_ref[:, cols] = (o * lax.rsqrt(ms + EPS) * gn_ref[...] * _silu(og_ref[:, cols])).astype(o_ref.dtype)

    @pl.when(c == pl.num_programs(1) - 1)
    def _():
        for hh in range(heads):
            s_out_ref[hh] = st_ref[hh].T


def _hgrn_seq(proj, hg_lb, gn, chunk=HG_CHUNK, heads=HG_HEADS_PER_STEP):
    t = proj.shape[0]
    w = 128 * heads
    blk = lambda off: pl.BlockSpec((chunk, w), lambda h, c, off=off: (c, off // heads + h))
    return pl.pallas_call(
        functools.partial(_hgrn_seq_kernel, chunk=chunk, heads=heads),
        grid=(H_HG // heads, t // chunk),
        in_specs=[blk(COL_HQ), blk(COL_HF), blk(COL_HI), blk(COL_OG),
                  pl.BlockSpec((hg_lb.shape[0], w), lambda h, c: (0, h)),
                  pl.BlockSpec((1, HG_DV), lambda h, c: (0, 0))],
        out_specs=[pl.BlockSpec((chunk, w), lambda h, c: (c, h)),
                   pl.BlockSpec((heads, HG_DK, HG_DV), lambda h, c: (h, 0, 0))],
        out_shape=[jax.ShapeDtypeStruct((t, HG_VW), BF16),
                   jax.ShapeDtypeStruct((H_HG, HG_DK, HG_DV), F32)],
        scratch_shapes=[pltpu.VMEM((heads, HG_DV, HG_DK), F32)] + [pltpu.VMEM((heads, chunk, 128), F32)] * 5,
        compiler_params=_cparams("parallel", "arbitrary"),
        name="hgrn_seq",
    )(proj, proj, proj, proj, hg_lb, gn.reshape(1, HG_DV))


def _hgrn_step_kernel(hqc_ref, hfc_ref, hi_ref, og_ref, lbc_ref, gn_ref, s_ref, o_ref, s_out_ref):
    lbraw = lbc_ref[...]
    e = jnp.exp(lbraw - jnp.max(lbraw, axis=0, keepdims=True))
    lb = e[0] / jnp.sum(e, axis=0)
    q = _silu(hqc_ref[...])
    f = lb + (1.0 - lb) * jax.nn.sigmoid(hfc_ref[...])
    k = 1.0 - f
    v = hi_ref[...]
    s_new = f * s_ref[...] + k * v
    s_out_ref[...] = s_new
    o = jnp.sum(s_new * q, axis=-2, keepdims=True)
    ms = jnp.mean(o * o, axis=-1, keepdims=True)
    o_ref[...] = (o * lax.rsqrt(ms + EPS) * gn_ref[...] * _silu(og_ref[...])).astype(o_ref.dtype)


def _hgrn_step(proj, state, hg_lb, gn):
    nb = proj.shape[0]
    col = lambda a: a.reshape(nb, H_HG, HG_DK, 1)
    rowv = lambda a: a.reshape(nb, H_HG, 1, HG_DV)
    hq = col(proj[:, 0:HG_W])
    hf = col(proj[:, HG_W:2 * HG_W])
    hi = rowv(proj[:, 2 * HG_W:2 * HG_W + HG_VW])
    og = rowv(proj[:, 2 * HG_W + HG_VW:2 * HG_W + 2 * HG_VW])
    nl = hg_lb.shape[0]
    per = 4 if nb % 4 == 0 else 1
    cspec = pl.BlockSpec((per, H_HG, HG_DK, 1), lambda b: (b, 0, 0, 0))
    rspec = pl.BlockSpec((per, H_HG, 1, HG_DV), lambda b: (b, 0, 0, 0))
    sspec = pl.BlockSpec((per, H_HG, HG_DK, HG_DV), lambda b: (b, 0, 0, 0))
    return pl.pallas_call(
        _hgrn_step_kernel,
        grid=(nb // per,),
        in_specs=[cspec, cspec, rspec, rspec,
                  pl.BlockSpec((nl, H_HG, HG_DK, 1), lambda b: (0, 0, 0, 0)),
                  pl.BlockSpec((1, 1, HG_DV), lambda b: (0, 0, 0)),
                  sspec],
        out_specs=[rspec, sspec],
        out_shape=[jax.ShapeDtypeStruct((nb, H_HG, 1, HG_DV), BF16),
                   jax.ShapeDtypeStruct((nb, H_HG, HG_DK, HG_DV), F32)],
        compiler_params=_cparams("parallel"),
        name="hgrn_step",
    )(hq, hf, hi, og, hg_lb.reshape(nl, H_HG, HG_DK, 1), gn.reshape(1, 1, HG_DV), state)


BUCKET_STARTS = list(range(1, MAX_EXACT + 1)) + BUCKET_THRESHOLDS


def _bias_from_rel(rel, lo, hi, tab_ref, h):
    first = sum(1 for start in BUCKET_STARTS if start <= lo)
    out = jnp.full(rel.shape, tab_ref[first, h], F32)
    for bkt in range(first + 1, N_BUCKETS):
        if BUCKET_STARTS[bkt - 1] <= hi:
            out = jnp.where(rel >= BUCKET_STARTS[bkt - 1], tab_ref[bkt, h], out)
    return jnp.where(rel >= 0, out, NEG_INF) if lo < 0 else out


def _bias_tiles_kernel(tab_ref, o_ref, *, base, tile_step, row_step, col_step, scale):
    h = pl.program_id(0)
    n_tiles, rows, cols = o_ref.shape
    r = lax.broadcasted_iota(jnp.int32, (rows, cols), 0)
    c = lax.broadcasted_iota(jnp.int32, (rows, cols), 1)
    in_tile = r * row_step + c * col_step
    spans = [row_step * (rows - 1), col_step * (cols - 1)]
    for t in range(n_tiles):
        off = base + t * tile_step
        lo = off + sum(min(sp, 0) for sp in spans)
        hi = off + sum(max(sp, 0) for sp in spans)
        bias = _bias_from_rel(off + in_tile, lo, hi, tab_ref, h)
        o_ref[t] = bias if scale == 1.0 else bias * scale


def _bias_tiles(rel_bias, n_tiles, rows, cols, base, tile_step, row_step, col_step, scale=1.0):
    return pl.pallas_call(
        functools.partial(_bias_tiles_kernel, base=base, tile_step=tile_step, row_step=row_step,
                          col_step=col_step, scale=scale),
        grid=(H_ATT,),
        in_specs=[pl.BlockSpec(memory_space=pltpu.SMEM)],
        out_specs=pl.BlockSpec((None, n_tiles, rows, cols), lambda h: (h, 0, 0, 0)),
        out_shape=jax.ShapeDtypeStruct((H_ATT, n_tiles, rows, cols), F32),
        compiler_params=_cparams("parallel"),
        name="rel_bias_tiles",
    )(rel_bias)


def _block_mean_kernel(k_ref, o_ref):
    for blk in range(o_ref.shape[0]):
        o_ref[blk] = jnp.mean(k_ref[blk * MOBA_BLOCK:(blk + 1) * MOBA_BLOCK, :], axis=0, keepdims=True)


def _block_mean(proj, blocks_per_step=4):
    t = proj.shape[0]
    nb = t // MOBA_BLOCK
    assert nb % blocks_per_step == 0
    out = pl.pallas_call(
        _block_mean_kernel,
        grid=(nb // blocks_per_step,),
        in_specs=[pl.BlockSpec((blocks_per_step * MOBA_BLOCK, ATT_W), lambda n: (n, COLK_AK))],
        out_specs=pl.BlockSpec((blocks_per_step, 1, ATT_W), lambda n: (n, 0, 0)),
        out_shape=jax.ShapeDtypeStruct((nb, 1, ATT_W), F32),
        compiler_params=_cparams("parallel"),
        name="moba_block_mean",
    )(proj)
    return out.reshape(nb, ATT_W)


def _top_blocks(s, n_valid, axis):
    n = float(s.shape[axis])
    idx = lax.broadcasted_iota(jnp.int32, s.shape, axis).astype(F32)
    s = jnp.where(idx < n_valid, s, NEG_INF)
    sel = jnp.zeros(s.shape, F32)
    for _ in range(MOBA_TOPK):
        m = jnp.max(s, axis=axis, keepdims=True)
        first = jnp.min(jnp.where(s == m, idx, n), axis=axis, keepdims=True)
        pick = jnp.logical_and(idx == first, m > NEG_INF)
        sel = jnp.where(pick, 1.0, sel)
        s = jnp.where(pick, NEG_INF, s)
    return sel


def _moba_pen_kernel(tab_ref, q_ref, km_ref, pen_ref, penfar_ref):
    h = pl.program_id(0)
    nblk = km_ref.shape[0]
    tq = q_ref.shape[0]
    scores = _dot_nt(km_ref[...].astype(BF16), (q_ref[...] * ATT_SCALE).astype(BF16))
    pos = pl.program_id(1) * tq + lax.broadcasted_iota(jnp.int32, (1, tq), 1)
    own = pos // MOBA_BLOCK
    blk_id = lax.broadcasted_iota(jnp.int32, scores.shape, 0)
    chosen = _top_blocks(scores, own, axis=0) > 0.5
    pen_ref[0:nblk, :] = jnp.where(jnp.logical_or(chosen, blk_id == own), 0.0, NEG_INF)
    pen_ref[nblk:, :] = jnp.full((pen_ref.shape[0] - nblk, tq), NEG_INF, F32)
    far_bias = tab_ref[N_BUCKETS - 1, h] * LOG2E
    penfar_ref[...] = jnp.where(jnp.logical_and(chosen, blk_id <= own - MOBA_GROUP), far_bias, NEG_INF)


def _moba_pen(proj, kmean, rel_bias, tq=1024):
    t = proj.shape[0]
    nblk = t // MOBA_BLOCK
    return pl.pallas_call(
        _moba_pen_kernel,
        grid=(H_ATT, t // tq),
        in_specs=[pl.BlockSpec(memory_space=pltpu.SMEM),
                  pl.BlockSpec((tq, ATT_DH), lambda h, n: (n, COL_AQ + h)),
                  pl.BlockSpec((nblk, ATT_DH), lambda h, n: (0, h))],
        out_specs=[pl.BlockSpec((None, nblk + 8, tq), lambda h, n: (h, 0, n)),
                   pl.BlockSpec((None, nblk, tq), lambda h, n: (h, 0, n))],
        out_shape=[jax.ShapeDtypeStruct((H_ATT, nblk + 8, t), F32),
                   jax.ShapeDtypeStruct((H_ATT, nblk, t), F32)],
        compiler_params=_cparams("parallel", "parallel"),
        name="moba_block_masks",
    )(rel_bias, proj, kmean)


def _stream_page_sums(pt_ref, pages_hbm, sums_ref, pbuf, psem, step, n_steps, pages_per_block):
    n_pg = pbuf.shape[1]
    total = pt_ref.shape[0]
    slot = step % 2

    def copies(st, sl):
        return [pltpu.make_async_copy(pages_hbm.at[pt_ref[jnp.minimum(st * n_pg + p, total - 1)]],
                                      pbuf.at[sl, p], psem.at[sl, p]) for p in range(n_pg)]

    @pl.when(step == 0)
    def _():
        for cp in copies(step, slot):
            cp.start()

    @pl.when(step < n_steps)
    def _():
        for cp in copies(step, slot):
            cp.wait()

    @pl.when(step + 1 < n_steps)
    def _():
        for cp in copies(step + 1, 1 - slot):
            cp.start()

    def reduce_pages():
        for blk in range(n_pg // pages_per_block):
            tot = pbuf[slot, blk * pages_per_block].sum(axis=0)
            for pg in range(1, pages_per_block):
                tot = tot + pbuf[slot, blk * pages_per_block + pg].sum(axis=0)
            sums_ref[blk] = tot

    return reduce_pages


def _moba_seq_kernel(*refs, n_far, stream):
    if stream:
        n_stream_steps, pages_per_block = stream
        (pt_ref, q_ref, k_ref, v_ref, bias_ref, pen_ref, penfar_ref, pages_hbm, o_ref, sums_ref,
         kb_s, vt_s, acc_s, s_s, pbuf, psem) = refs
        reduce_pages = _stream_page_sums(
            pt_ref, pages_hbm, sums_ref, pbuf, psem,
            pl.program_id(0) * pl.num_programs(1) + pl.program_id(1), n_stream_steps, pages_per_block)
    else:
        q_ref, k_ref, v_ref, bias_ref, pen_ref, penfar_ref, o_ref, kb_s, vt_s, acc_s, s_s = refs
        reduce_pages = None
    i = pl.program_id(1)
    nblk = vt_s.shape[0]

    @pl.when(i == 0)
    def _():
        kb_s[...] = k_ref[...].astype(BF16)

        def xpose(j, carry):
            r0 = pl.multiple_of(j * MOBA_BLOCK, MOBA_BLOCK)
            vt_s[j] = v_ref[pl.ds(r0, MOBA_BLOCK), :].T.astype(BF16)
            return carry

        lax.fori_loop(0, nblk, xpose, 0)

    if reduce_pages is not None:
        reduce_pages()

    tq = q_ref.shape[0]
    grp = MOBA_GROUP
    qb = (q_ref[...] * (ATT_SCALE * LOG2E)).astype(BF16)
    acc_s[...] = jnp.zeros_like(acc_s)

    def logits(j):
        c0 = pl.multiple_of(j * MOBA_BLOCK, MOBA_BLOCK)
        return _dot_nt(kb_s[pl.ds(c0, MOBA_BLOCK), :], qb)

    def keep_max(smax, s):
        cur = jnp.max(s, axis=0, keepdims=True)
        return cur if smax is None else jnp.maximum(smax, cur)

    smax = None
    for u in range(grp):
        j = jnp.maximum(i - u, 0)
        pen_row = jnp.where(u <= i, j, nblk)
        s = logits(j) + bias_ref[min(u, n_far)] + pen_ref[pl.ds(pen_row, 1), :]
        s_s[0, u] = s
        smax = keep_max(smax, s)

    def softmax_update(slot, m_old, l_old, smax, block_of):
        mg = jnp.maximum(m_old, smax)
        alpha = jnp.exp2(m_old - mg)
        l_new = alpha * l_old
        pv = None
        for u in range(grp):
            p = jnp.exp2(s_s[slot, u] - mg)
            l_new = l_new + jnp.sum(p, axis=0, keepdims=True)
            d = _dot(vt_s[block_of(u)], p.astype(BF16))
            pv = d if pv is None else pv + d
        acc_s[...] = alpha * acc_s[...] + pv
        return mg, l_new

    def trip_with_slots(t, carry, slot_prev):
        m_old, l_old, smax_prev = carry
        smax_new = None
        for u in range(grp):
            s = logits(t * grp + u) + penfar_ref[pl.ds(t * grp + u, 1), :]
            s_s[1 - slot_prev, u] = s
            smax_new = keep_max(smax_new, s)
        prev = lambda u: jnp.where(t == 0, jnp.maximum(i - u, 0), (t - 1) * grp + u)
        m_new, l_new = softmax_update(slot_prev, m_old, l_old, smax_prev, prev)
        return m_new, l_new, smax_new

    def trip(t, carry):
        return lax.cond(t % 2 == 0, lambda c: trip_with_slots(t, c, 0), lambda c: trip_with_slots(t, c, 1), carry)

    n_far_groups = i // grp
    init = (jnp.full((1, tq), NEG_INF, F32), jnp.zeros((1, tq), F32), smax)
    m, l, smax = lax.fori_loop(0, n_far_groups, trip, init)
    last = lambda u: jnp.where(n_far_groups == 0, jnp.maximum(i - u, 0), (n_far_groups - 1) * grp + u)
    m, l = softmax_update(n_far_groups % 2, m, l, smax, last)
    o_ref[...] = (acc_s[...] / l).T.astype(o_ref.dtype)


def _moba_seq(proj, rel_bias, side_pool=None, side_page_table=None):
    t = proj.shape[0]
    nq = t // MOBA_BLOCK
    kmean = _block_mean(proj)
    n_far = -(-(BUCKET_THRESHOLDS[-1] + MOBA_BLOCK) // MOBA_BLOCK)
    assert MOBA_GROUP >= n_far and nq % MOBA_GROUP == 0
    bias = _bias_tiles(rel_bias, n_far + 1, MOBA_BLOCK, MOBA_BLOCK, 0, MOBA_BLOCK, -1, 1, scale=LOG2E)
    pen, penfar = _moba_pen(proj, kmean, rel_bias)
    n_steps = H_ATT * nq
    in_specs = [pl.BlockSpec((MOBA_BLOCK, ATT_DH), lambda h, i, *_: (i, COL_AQ + h)),
                pl.BlockSpec((t, ATT_DH), lambda h, i, *_: (0, COL_AK + h)),
                pl.BlockSpec((t, ATT_DH), lambda h, i, *_: (0, COL_AV + h)),
                pl.BlockSpec((None, n_far + 1, MOBA_BLOCK, MOBA_BLOCK), lambda h, i, *_: (h, 0, 0, 0)),
                pl.BlockSpec((None, nq + 8, MOBA_BLOCK), lambda h, i, *_: (h, 0, i)),
                pl.BlockSpec((None, nq, MOBA_BLOCK), lambda h, i, *_: (h, 0, i))]
    out_specs = [pl.BlockSpec((MOBA_BLOCK, ATT_DH), lambda h, i, *_: (i, h))]
    out_shape = [jax.ShapeDtypeStruct((t, ATT_W), BF16)]
    scratch = [pltpu.VMEM((t, ATT_DH), BF16), pltpu.VMEM((nq, ATT_DH, MOBA_BLOCK), BF16),
               pltpu.VMEM((ATT_DH, MOBA_BLOCK), F32),
               pltpu.VMEM((2, MOBA_GROUP, MOBA_BLOCK, MOBA_BLOCK), F32)]
    operands = [proj, proj, proj, bias, pen, penfar]
    stream = None
    prefetch = []
    if side_pool is not None:
        pt = side_page_table.reshape(-1)
        page_rows = side_pool.shape[1]
        ppb = MOBA_BLOCK // page_rows
        assert ppb * page_rows == MOBA_BLOCK and pt.shape[0] % ppb == 0
        n_pg = ppb * -(-pt.shape[0] // (ppb * n_steps))
        stream = (-(-pt.shape[0] // n_pg), ppb)
        in_specs.append(pl.BlockSpec(memory_space=pl.ANY))
        out_specs.append(pl.BlockSpec((None, n_pg // ppb, H_ATT, ATT_DH), lambda h, i, *_: (h * nq + i, 0, 0, 0)))
        out_shape.append(jax.ShapeDtypeStruct((n_steps, n_pg // ppb, H_ATT, ATT_DH), F32))
        scratch += [pltpu.VMEM((2, n_pg) + side_pool.shape[1:], F32), pltpu.SemaphoreType.DMA((2, n_pg))]
        operands.append(side_pool)
        prefetch = [pt]
    outs = pl.pallas_call(
        functools.partial(_moba_seq_kernel, n_far=n_far, stream=stream),
        grid_spec=pltpu.PrefetchScalarGridSpec(
            num_scalar_prefetch=len(prefetch), grid=(H_ATT, nq), in_specs=in_specs, out_specs=out_specs,
            scratch_shapes=scratch),
        out_shape=out_shape,
        compiler_params=_cparams("arbitrary", "arbitrary"),
        name="moba_seq",
    )(*prefetch, *operands)
    if side_pool is None:
        return outs[0], None
    return outs[0], outs[1].reshape(-1, H_ATT, ATT_DH)[:pt.shape[0] // ppb]


SELECT_PAGES_PER_STEP = 8
FFN_CHUNK = 256
MOBA_STEP_SLOTS = 4


def _moba_select_kernel(ksum_ref, q_ref, sel_ref):
    kmean = ksum_ref[...] / MOBA_BLOCK
    prod = kmean.astype(BF16).astype(F32) * (q_ref[...] * ATT_SCALE).astype(BF16).astype(F32)[None]
    s = jnp.sum(prod, axis=-1, keepdims=True)
    nb = s.shape[0]
    mask = _top_blocks(s, nb, axis=0)
    idx = lax.broadcasted_iota(jnp.int32, s.shape, 0).astype(F32)
    for r in range(MOBA_TOPK):
        first = jnp.min(jnp.where(mask > 0.5, idx, float(nb)), axis=0)
        sel_ref[r] = first.astype(jnp.int32)
        mask = jnp.where(idx == first[None], 0.0, mask)


def _moba_select(block_sums, q_heads):
    nb_seq, n_blocks = block_sums.shape[:2]
    assert n_blocks >= MOBA_TOPK
    return pl.pallas_call(
        _moba_select_kernel,
        grid=(nb_seq,),
        in_specs=[pl.BlockSpec((None, n_blocks, H_ATT, ATT_DH), lambda b: (b, 0, 0, 0)),
                  pl.BlockSpec((None, H_ATT, ATT_DH), lambda b: (b, 0, 0))],
        out_specs=pl.BlockSpec((None, MOBA_TOPK, H_ATT, 1), lambda b: (b, 0, 0, 0)),
        out_shape=jax.ShapeDtypeStruct((nb_seq, MOBA_TOPK, H_ATT, 1), jnp.int32),
        compiler_params=_cparams("parallel"),
        name="moba_select",
    )(block_sums, q_heads)


def _page_block_sums_kernel(pt_ref, *refs, pages_per_block):
    page_refs, o_ref = refs[:-1], refs[-1]
    for blk in range(len(page_refs) // pages_per_block):
        tot = page_refs[blk * pages_per_block][...].sum(axis=0)
        for pg in range(1, pages_per_block):
            tot = tot + page_refs[blk * pages_per_block + pg][...].sum(axis=0)
        o_ref[blk] = tot


def _page_block_sums(pool, page_table):
    nb_seq, n_pages = page_table.shape
    page_rows = pool.shape[1]
    ppb = MOBA_BLOCK // page_rows
    n_pg = SELECT_PAGES_PER_STEP
    assert ppb * page_rows == MOBA_BLOCK and n_pg % ppb == 0 and n_pages % n_pg == 0
    kspec = lambda p: pl.BlockSpec((None,) + pool.shape[1:],
                                   lambda b, n, pt, p=p: (pt[b * n_pages + n_pg * n + p], 0, 0, 0))
    return pl.pallas_call(
        functools.partial(_page_block_sums_kernel, pages_per_block=ppb),
        grid_spec=pltpu.PrefetchScalarGridSpec(
            num_scalar_prefetch=1,
            grid=(nb_seq, n_pages // n_pg),
            in_specs=[kspec(p) for p in range(n_pg)],
            out_specs=pl.BlockSpec((None, n_pg // ppb, H_ATT, ATT_DH), lambda b, n, pt: (b, n, 0, 0))),
        out_shape=jax.ShapeDtypeStruct((nb_seq, n_pages // ppb, H_ATT, ATT_DH), F32),
        compiler_params=_cparams("parallel", "arbitrary"),
        name="page_block_sums",
    )(page_table.reshape(-1), *([pool] * n_pg))


def _moba_step_kernel(pt_ref, sel_ref, q_ref, kn_ref, vn_ref, b0_ref, b1_ref, b2_ref, bown_ref, ck_hbm, cv_hbm,
                      o_ref, kbuf, vbuf, sem, *, n_pages):
    n_pg = MOBA_TOPK * 2
    b_refs = (b0_ref, b1_ref, b2_ref)
    n_slots = kbuf.shape[0]
    page_rows = kbuf.shape[2]
    b = pl.program_id(0)
    h = pl.program_id(1)
    n_heads = pl.num_programs(1)
    n_steps = pl.num_programs(0) * n_heads
    step = b * n_heads + h
    slot = step % n_slots

    def page_copies(bb, hh, sl):
        cps = []
        for r in range(MOBA_TOPK):
            blk = sel_ref[(bb * MOBA_TOPK + r) * n_heads + hh]
            for half in range(2):
                page = pt_ref[bb * n_pages + 2 * blk + half]
                idx = 2 * r + half
                cps.append(pltpu.make_async_copy(ck_hbm.at[page, :, hh, :], kbuf.at[sl, idx], sem.at[sl, idx]))
                cps.append(pltpu.make_async_copy(cv_hbm.at[page, :, hh, :], vbuf.at[sl, idx], sem.at[sl, n_pg + idx]))
        return cps

    def start_step(st):
        for cp in page_copies(st // n_heads, st % n_heads, st % n_slots):
            cp.start()

    @pl.when(step == 0)
    def _():
        for ahead in range(n_slots - 1):
            @pl.when(ahead < n_steps)
            def _():
                start_step(jnp.int32(ahead))

    @pl.when(step + n_slots - 1 < n_steps)
    def _():
        start_step(step + n_slots - 1)

    for cp in page_copies(b, h, slot):
        cp.wait()

    qs = q_ref[...] * ATT_SCALE
    eye = (lax.broadcasted_iota(jnp.int32, (page_rows, page_rows), 0)
           == lax.broadcasted_iota(jnp.int32, (page_rows, page_rows), 1))
    logits = []
    for r in range(MOBA_TOPK):
        bias = b_refs[r][...]
        for half in range(2):
            lg = jnp.sum(kbuf[slot, 2 * r + half] * qs, axis=-1, keepdims=True)
            brow = bias[:, half * page_rows:(half + 1) * page_rows]
            bcol = jnp.sum(jnp.where(eye, brow, 0.0), axis=-1, keepdims=True)
            logits.append(lg + bcol)
    l_own = jnp.sum(qs * kn_ref[...], axis=-1, keepdims=True) + bown_ref[:, 0:1]
    m = l_own
    for lg in logits:
        m = jnp.maximum(m, jnp.max(lg, axis=0, keepdims=True))
    p_own = jnp.exp(l_own - m)
    den = p_own
    acc = p_own * vn_ref[...]
    for idx, lg in enumerate(logits):
        p = jnp.exp(lg - m)
        den = den + jnp.sum(p, axis=0, keepdims=True)
        acc = acc + jnp.sum(p * vbuf[slot, idx], axis=0, keepdims=True)
    o_ref[...] = (acc / den).astype(o_ref.dtype)


def _moba_step(proj, block_sums, cache_k4, cache_v4, page_table, rel_bias):
    nb_seq, n_pages = page_table.shape
    page_rows = cache_k4.shape[1]
    ppb = MOBA_BLOCK // page_rows
    assert ppb == 2
    n_blocks = n_pages // ppb
    n_past = n_pages * page_rows
    head_rows = lambda a: a.reshape(nb_seq, H_ATT, 1, ATT_DH)
    aq = proj[:, COL_AQ * 128:COL_AQ * 128 + ATT_W]
    ak = proj[:, COL_AK * 128:COL_AK * 128 + ATT_W]
    av = proj[:, COL_AV * 128:COL_AV * 128 + ATT_W]
    sel = _moba_select(block_sums.reshape(nb_seq, n_blocks, H_ATT, ATT_DH), aq.reshape(nb_seq, H_ATT, ATT_DH))
    bias = _bias_tiles(rel_bias, n_blocks + 1, 1, MOBA_BLOCK, n_past, -MOBA_BLOCK, 0, -1)

    def bias_spec(r):
        return pl.BlockSpec((None, None, 1, MOBA_BLOCK),
                            lambda b, h, pt, sl: (h, sl[(b * MOBA_TOPK + r) * H_ATT + h], 0, 0))

    row_spec = pl.BlockSpec((None, None, 1, ATT_DH), lambda b, h, pt, sl: (b, h, 0, 0))
    hbm = pl.BlockSpec(memory_space=pl.ANY)
    n_pg = 2 * MOBA_TOPK
    out = pl.pallas_call(
        functools.partial(_moba_step_kernel, n_pages=n_pages),
        grid_spec=pltpu.PrefetchScalarGridSpec(
            num_scalar_prefetch=2,
            grid=(nb_seq, H_ATT),
            in_specs=[row_spec, row_spec, row_spec] + [bias_spec(r) for r in range(MOBA_TOPK)]
                     + [pl.BlockSpec((None, None, 1, MOBA_BLOCK), lambda b, h, pt, sl: (h, n_blocks, 0, 0)),
                        hbm, hbm],
            out_specs=row_spec,
            scratch_shapes=[pltpu.VMEM((MOBA_STEP_SLOTS, n_pg, page_rows, ATT_DH), F32),
                            pltpu.VMEM((MOBA_STEP_SLOTS, n_pg, page_rows, ATT_DH), F32),
                            pltpu.SemaphoreType.DMA((MOBA_STEP_SLOTS, 2 * n_pg))]),
        out_shape=jax.ShapeDtypeStruct((nb_seq, H_ATT, 1, ATT_DH), BF16),
        compiler_params=_cparams("arbitrary", "arbitrary"),
        name="moba_step",
    )(page_table.reshape(-1), sel.reshape(-1), head_rows(aq), head_rows(ak), head_rows(av),
      bias, bias, bias, bias, cache_k4, cache_v4)
    return out.reshape(nb_seq, ATT_W)


def _merge_out_kernel(oh_ref, oa_ref, gh0_ref, gh1_ref, ga0_ref, ga1_ref, wh_ref, wa_ref, wo_ref, x_ref, g_ref,
                      gnext_ref, o_ref, h_ref):
    half = D_MODEL // 2
    br_h = _dot(oh_ref[...], wh_ref[...])
    br_a = _dot(oa_ref[...], wa_ref[...])
    merged = jnp.concatenate(
        [(jax.nn.sigmoid(gh_ref[...]) * br_h[:, c * half:(c + 1) * half]
          + jax.nn.sigmoid(ga_ref[...]) * br_a[:, c * half:(c + 1) * half]).astype(BF16)
         for c, (gh_ref, ga_ref) in enumerate(((gh0_ref, ga0_ref), (gh1_ref, ga1_ref)))], axis=1)
    z = _dot(merged, wo_ref[...])
    ms = jnp.mean(z * z, axis=-1, keepdims=True)
    x1 = x_ref[...] + z * lax.rsqrt(ms + EPS) * g_ref[...]
    o_ref[...] = x1
    ms1 = jnp.mean(x1 * x1, axis=-1, keepdims=True)
    h_ref[...] = (x1 * lax.rsqrt(ms1 + EPS) * gnext_ref[...]).astype(h_ref.dtype)


def _merge_out(o_hg, o_att, proj, w_bh, w_bm, w_out, x, gain, gain_next, tm):
    m = x.shape[0]
    half = D_MODEL // 2
    assert half == 1024
    resident = pl.Buffered(1)
    vec = pl.BlockSpec((1, D_MODEL), lambda i: (0, 0), pipeline_mode=resident)
    row = pl.BlockSpec((tm, D_MODEL), lambda i: (i, 0))
    gate = lambda blk: pl.BlockSpec((tm, half), lambda i, blk=blk: (i, blk))
    return pl.pallas_call(
        _merge_out_kernel,
        grid=(m // tm,),
        in_specs=[pl.BlockSpec((tm, HG_VW), lambda i: (i, 0)), pl.BlockSpec((tm, ATT_W), lambda i: (i, 0)),
                  gate(COLK_GHG), gate(COLK_GHG + 1), gate(COLK_GATT), gate(COLK_GATT + 1),
                  pl.BlockSpec((HG_VW, D_MODEL), lambda i: (0, 0), pipeline_mode=resident),
                  pl.BlockSpec((ATT_W, D_MODEL), lambda i: (0, 0), pipeline_mode=resident),
                  pl.BlockSpec((D_MODEL, D_MODEL), lambda i: (0, 0), pipeline_mode=resident),
                  row, vec, vec],
        out_specs=[row, row],
        out_shape=[jax.ShapeDtypeStruct((m, D_MODEL), F32), jax.ShapeDtypeStruct((m, D_MODEL), BF16)],
        compiler_params=_cparams("parallel"),
        name="merge_out_proj_residual",
    )(o_hg, o_att, proj, proj, proj, proj, w_bh, w_bm, w_out, x, gain.reshape(1, D_MODEL),
      gain_next.reshape(1, D_MODEL))


def _ffn_up_kernel(h_ref, wg_ref, wu_ref, cw_ref, cb_ref, p0_ref, p1_ref, a_ref, g_out_ref, carry_s, *, seq):
    i = pl.program_id(0)
    j = pl.program_id(1)
    tm, tf = a_ref.shape
    if seq:
        @pl.when(i == 0)
        def _():
            carry_s[j, 0:1, :] = p0_ref[...]
            carry_s[j, 1:2, :] = p1_ref[...]

    top = lax.broadcasted_iota(jnp.int32, (8, FFN_CHUNK), 0)
    for c in range(tf // FFN_CHUNK):
        cols = slice(c * FFN_CHUNK, (c + 1) * FFN_CHUNK)
        g = _dot(h_ref[...], wg_ref[:, cols].astype(BF16))
        u = _dot(h_ref[...], wu_ref[:, cols].astype(BF16))
        if seq:
            c0 = carry_s[j, 0:1, cols]
            c1 = carry_s[j, 1:2, cols]
            r1 = pltpu.roll(g, 1, axis=0)
            r2 = pltpu.roll(g, 2, axis=0)
            prev1 = jnp.concatenate([jnp.where(top == 0, c1, r1[0:8]), r1[8:]], axis=0)
            prev2 = jnp.concatenate([jnp.where(top == 0, c0, jnp.where(top == 1, c1, r2[0:8])), r2[8:]], axis=0)
            carry_s[j, 0:2, cols] = g[tm - 2:tm, :]
            g_out_ref[:, cols] = g[tm - 2:tm, :]
        else:
            prev2 = p0_ref[:, cols]
            prev1 = p1_ref[:, cols]
            g_out_ref[:, cols] = g
        gc = cb_ref[:, cols] + cw_ref[0:1, cols] * prev2 + cw_ref[1:2, cols] * prev1 + cw_ref[2:3, cols] * g
        a_ref[:, cols] = (jax.nn.gelu(gc, approximate=True) * u).astype(a_ref.dtype)


def _ffn_down_kernel(a_ref, wd_ref, x_ref, g_ref, o_ref):
    y = _dot(a_ref[...], wd_ref[...])
    ms = jnp.mean(y * y, axis=-1, keepdims=True)
    o_ref[...] = x_ref[...] + y * lax.rsqrt(ms + EPS) * g_ref[...]


def _ffn(x, h, w_up, conv_w, conv_b, prev0, prev1, w_down, gpost, *, seq, tm_up, tm_down, tf=512):
    m = x.shape[0]
    nf = D_FF // tf
    assert CONV_W == 3 and nf * tf == D_FF
    prow = 1 if seq else tm_up
    pspec = pl.BlockSpec((prow, tf), (lambda i, j: (0, j)) if seq else (lambda i, j: (i, j)))
    if seq:
        gspec = pl.BlockSpec((None, CONV_W - 1, tf), lambda i, j: (i, 0, j))
        gshape = (m // tm_up, CONV_W - 1, D_FF)
    else:
        gspec = pl.BlockSpec((tm_up, tf), lambda i, j: (i, j))
        gshape = (m, D_FF)
    act, g_out = pl.pallas_call(
        functools.partial(_ffn_up_kernel, seq=seq),
        grid=(m // tm_up, nf),
        in_specs=[pl.BlockSpec((tm_up, D_MODEL), lambda i, j: (i, 0)),
                  pl.BlockSpec((D_MODEL, tf), lambda i, j: (0, j)),
                  pl.BlockSpec((D_MODEL, tf), lambda i, j: (0, nf + j)),
                  pl.BlockSpec((CONV_W, tf), lambda i, j: (0, j)),
                  pl.BlockSpec((1, tf), lambda i, j: (0, j)),
                  pspec, pspec],
        out_specs=[pl.BlockSpec((tm_up, tf), lambda i, j: (i, j)), gspec],
        out_shape=[jax.ShapeDtypeStruct((m, D_FF), BF16), jax.ShapeDtypeStruct(gshape, F32)],
        scratch_shapes=[pltpu.VMEM((nf, 8, tf), F32)],
        compiler_params=_cparams("arbitrary", "arbitrary"),
        name="conv_ffn_up",
    )(h, w_up, w_up, conv_w, conv_b.reshape(1, D_FF), prev0, prev1)
    resident = pl.Buffered(1)
    y = pl.pallas_call(
        _ffn_down_kernel,
        grid=(m // tm_down,),
        in_specs=[pl.BlockSpec((tm_down, D_FF), lambda i: (i, 0)),
                  pl.BlockSpec((D_FF, D_MODEL), lambda i: (0, 0), pipeline_mode=resident),
                  pl.BlockSpec((tm_down, D_MODEL), lambda i: (i, 0)),
                  pl.BlockSpec((1, D_MODEL), lambda i: (0, 0), pipeline_mode=resident)],
        out_specs=pl.BlockSpec((tm_down, D_MODEL), lambda i: (i, 0)),
        out_shape=jax.ShapeDtypeStruct((m, D_MODEL), F32),
        compiler_params=_cparams("parallel"),
        name="ffn_down_residual",
    )(act, w_down, x, gpost.reshape(1, D_MODEL))
    return y, (g_out[-1] if seq else g_out)


def kernel(x_prompt, x_sample, cache_k, cache_v, state_hgrn, state_ffn_conv, page_table, rel_bias, hg_lb,
           norm_mix_pre, norm_mix_post, norm_ffn_pre, norm_ffn_post, w_in, hg_out_norm, w_branch_hgrn,
           w_branch_moba, w_out, w_ffn_up, ffn_conv_w, ffn_conv_b, w_ffn_down):
    nb, t, _ = x_prompt.shape
    db = x_sample.shape[0]
    depth = w_in.shape[0]
    assert nb == 1 and depth == 1 and x_sample.shape[1] == 1
    l = 0
    w_in_b = w_in[l]
    w_bh = w_branch_hgrn[l].astype(BF16)
    w_bm = w_branch_moba[l].astype(BF16)
    w_o = w_out[l].astype(BF16)
    w_up = w_ffn_up[l]
    w_dn = w_ffn_down[l].astype(BF16)

    n_pool, page_rows = cache_k.shape[1], cache_k.shape[2]
    ck = cache_k.reshape(depth * n_pool, page_rows, H_ATT, ATT_DH)
    cv = cache_v.reshape(depth * n_pool, page_rows, H_ATT, ATT_DH)

    xp = x_prompt.reshape(t, D_MODEL)
    proj = _norm_matmul(xp, norm_mix_pre[l], w_in_b, tm=1024, tn=1024)
    o_hg, s_new = _hgrn_seq(proj, hg_lb, hg_out_norm[l])
    o_att, cache_block_sums = _moba_seq(proj, rel_bias, side_pool=ck, side_page_table=page_table)
    x1, h2 = _merge_out(o_hg, o_att, proj, w_bh, w_bm, w_o, xp, norm_mix_post[l], norm_ffn_pre[l], tm=256)
    zero_row = jnp.zeros((1, D_FF), F32)
    yp, conv_p = _ffn(x1, h2, w_up, ffn_conv_w[l], ffn_conv_b[l], zero_row, zero_row, w_dn,
                      norm_ffn_post[l], seq=True, tm_up=1024, tm_down=256)
    kp = proj[:, COL_AK * 128:COL_AK * 128 + ATT_W].reshape(1, 1, t, H_ATT, ATT_DH)
    vp = proj[:, COL_AV * 128:COL_AV * 128 + ATT_W].reshape(1, 1, t, H_ATT, ATT_DH)

    xs = x_sample.reshape(db, D_MODEL)
    projs = _norm_matmul(xs, norm_mix_pre[l], w_in_b, tm=db, tn=1024)
    o_hg_s, s_new_s = _hgrn_step(projs, state_hgrn[l], hg_lb, hg_out_norm[l])
    o_att_s = _moba_step(projs, cache_block_sums, ck, cv, page_table, rel_bias)
    x1s, h2s = _merge_out(o_hg_s.reshape(db, HG_VW), o_att_s, projs, w_bh, w_bm, w_o, xs, norm_mix_post[l],
                          norm_ffn_pre[l], tm=db)
    buf = state_ffn_conv[l]
    ys, g_s = _ffn(x1s, h2s, w_up, ffn_conv_w[l], ffn_conv_b[l], buf[:, 0], buf[:, 1], w_dn,
                   norm_ffn_post[l], seq=False, tm_up=db, tm_down=db)
    ks = projs[:, COL_AK * 128:COL_AK * 128 + ATT_W].reshape(1, db, 1, H_ATT, ATT_DH)
    vs = projs[:, COL_AV * 128:COL_AV * 128 + ATT_W].reshape(1, db, 1, H_ATT, ATT_DH)
    conv_s = jnp.stack([buf[:, 1], g_s], axis=1)

    return (yp.reshape(1, t, D_MODEL), ys.reshape(db, 1, D_MODEL), kp, vp,
            s_new.reshape(1, 1, H_HG, HG_DK, HG_DV), conv_p.reshape(1, 1, CONV_W - 1, D_FF),
            ks, vs, s_new_s.reshape(1, db, H_HG, HG_DK, HG_DV), conv_s.reshape(1, db, CONV_W - 1, D_FF))
```

```python
import functools
import math

import numpy as np
import jax
import jax.numpy as jnp
from jax import lax
from jax.experimental import pallas as pl
from jax.experimental.pallas import tpu as pltpu

F32 = jnp.float32
BF16 = jnp.bfloat16

D_MODEL = 2048
H_HG = 8
HG_DK = 128
HG_DV = 128
HG_W = H_HG * HG_DK
HG_VW = H_HG * HG_DV
H_ATT = 8
ATT_DH = 128
ATT_W = H_ATT * ATT_DH
MOBA_BLOCK = 256
MOBA_TOPK = 3
N_BUCKETS = 32
MAX_EXACT = N_BUCKETS // 2
REL_MAX_DIST = 1024
D_FF = 5632
CONV_W = 3
EPS = 1e-6
N_IN = 2 * HG_W + 2 * HG_VW + 3 * ATT_W + 2 * D_MODEL
ATT_SCALE = ATT_DH ** -0.5
LOG2E = math.log2(math.e)

COL_HQ, COL_HF, COL_HI, COL_OG = 0, 8, 16, 24
COL_AQ = 32
COLK_AK, COLK_AV = 5, 6
COLK_GHG, COLK_GATT = 5, 7

HG_CHUNK = 128
HG_SUB = 8
HG_HEADS_PER_STEP = 8
MOBA_GROUP = 8
VMEM_LIMIT = 56 * 1024 * 1024
NEG_INF = float("-inf")


def _bucket_thresholds():
    n = np.arange(MAX_EXACT, 4 * REL_MAX_DIST, dtype=np.float64)
    large = MAX_EXACT + (np.log(n / MAX_EXACT) / math.log(REL_MAX_DIST / MAX_EXACT)
                         * (N_BUCKETS - MAX_EXACT)).astype(np.int64)
    large = np.minimum(large, N_BUCKETS - 1)
    return [int(n[np.argmax(large >= b)]) for b in range(MAX_EXACT + 1, N_BUCKETS)]


BUCKET_THRESHOLDS = _bucket_thresholds()


def _cparams(*sem):
    return pltpu.CompilerParams(dimension_semantics=sem, vmem_limit_bytes=VMEM_LIMIT)


def _silu(x):
    return x * jax.nn.sigmoid(x)


def _dot(a, b):
    return jnp.dot(a, b, preferred_element_type=F32)


def _dot_nt(a, b, precision=None):
    return lax.dot_general(a, b, (((1,), (1,)), ((), ())), precision=precision,
                           preferred_element_type=F32)


def _norm_matmul_kernel(x_ref, g_ref, w_ref, o_ref, k_ref, v_ref, h_ref):
    j = pl.program_id(1)

    @pl.when(j == 0)
    def _():
        x = x_ref[...]
        ms = jnp.mean(x * x, axis=-1, keepdims=True)
        h_ref[...] = (x * lax.rsqrt(ms + EPS) * g_ref[...]).astype(BF16)

    @pl.when(j == COLK_AK)
    def _():
        k_ref[...] = _dot(h_ref[...], w_ref[...])

    @pl.when(j == COLK_AV)
    def _():
        v_ref[...] = _dot(h_ref[...], w_ref[...])

    @pl.when(jnp.logical_and(j != COLK_AK, j != COLK_AV))
    def _():
        o_ref[...] = _dot(h_ref[...], w_ref[...])


def _norm_matmul(x, gain, w_bf16, tm):
    m, k = x.shape
    n = w_bf16.shape[1]
    tn = ATT_W
    assert COLK_AV == COLK_AK + 1
    single = pl.Buffered(1)
    return pl.pallas_call(
        _norm_matmul_kernel,
        grid=(m // tm, n // tn),
        in_specs=[pl.BlockSpec((tm, k), lambda i, j: (i, 0)),
                  pl.BlockSpec((1, k), lambda i, j: (0, 0)),
                  pl.BlockSpec((k, tn), lambda i, j: (0, j))],
        out_specs=[pl.BlockSpec((tm, tn), lambda i, j: (i, jnp.where(j < COLK_AK, j, jnp.maximum(j - 2, COLK_AK - 1)))),
                   pl.BlockSpec((tm, tn), lambda i, j: (i, 0), pipeline_mode=single),
                   pl.BlockSpec((tm, tn), lambda i, j: (i, 0), pipeline_mode=single)],
        out_shape=[jax.ShapeDtypeStruct((m, n - 2 * tn), F32), jax.ShapeDtypeStruct((m, tn), F32),
                   jax.ShapeDtypeStruct((m, tn), F32)],
        scratch_shapes=[pltpu.VMEM((tm, k), BF16)],
        compiler_params=_cparams("parallel", "arbitrary"),
        name="norm_in_proj",
    )(x, gain.reshape(1, k), w_bf16)


def _forget_lower_bound(lbraw):
    e = jnp.exp(lbraw - jnp.max(lbraw, axis=0, keepdims=True))
    return e[0:1] / jnp.sum(e, axis=0, keepdims=True)


def _hgrn_seq_kernel(hq_ref, hf_ref, hi_ref, og_ref, lbraw_ref, gn_ref, o_ref, s_out_ref,
                     st_ref, q_s, k_s, b_s, v_s, acc_s, *, chunk, heads):
    c = pl.program_id(1)

    @pl.when(c == 0)
    def _():
        st_ref[...] = jnp.zeros_like(st_ref)

    row = lax.broadcasted_iota(jnp.int32, (chunk, HG_DK), 0)
    rr = lax.broadcasted_iota(jnp.int32, (chunk, chunk), 0)
    cc = lax.broadcasted_iota(jnp.int32, (chunk, chunk), 1)
    sub = lax.broadcasted_iota(jnp.int32, (HG_SUB, HG_DK), 0)

    for hh in range(heads):
        cols = slice(hh * 128, (hh + 1) * 128)
        lb = _forget_lower_bound(lbraw_ref[:, cols])
        q = _silu(hq_ref[:, cols])
        f = lb + (1.0 - lb) * jax.nn.sigmoid(hf_ref[:, cols])
        k = 1.0 - f
        v = hi_ref[:, cols]

        b = jnp.log2(f)
        shift = 1
        while shift < chunk:
            b = b + jnp.where(row >= shift, pltpu.roll(b, shift, axis=0), 0.0)
            shift *= 2

        q_s[hh] = q
        k_s[hh] = k
        b_s[hh] = b
        v_s[hh] = v

        st = st_ref[hh]
        o = _dot_nt((q * jnp.exp2(b)).astype(BF16), st.astype(BF16))

        a = jnp.zeros((chunk, chunk), F32)
        hs = HG_SUB
        while hs < chunk:
            blk = 2 * hs
            ref_rows = jnp.concatenate(
                [jnp.broadcast_to(b_s[hh, m0 + hs - 1:m0 + hs, :], (blk, HG_DK)) for m0 in range(0, chunk, blk)],
                axis=0)
            second = (row & (blk - 1)) >= hs
            qd = jnp.where(second, q * jnp.exp2(jnp.minimum(b - ref_rows, 0.0)), 0.0)
            kd = jnp.where(second, 0.0, k * jnp.exp2(jnp.minimum(ref_rows - b, 0.0)))
            a_l = _dot_nt(qd.astype(BF16), kd.astype(BF16))
            sh = int(math.log2(blk))
            a = a + jnp.where((rr >> sh) == (cc >> sh), a_l, 0.0)
            hs = blk
        o = o + _dot(a.astype(BF16), v.astype(BF16))

        for r0 in range(0, chunk, HG_SUB):
            qi = q_s[hh, r0:r0 + HG_SUB, :]
            bi = b_s[hh, r0:r0 + HG_SUB, :]
            acc = jnp.zeros((HG_SUB, HG_DV), F32)
            for s in range(HG_SUB):
                ks = k_s[hh, r0 + s:r0 + s + 1, :]
                bs = b_s[hh, r0 + s:r0 + s + 1, :]
                vs = v_s[hh, r0 + s:r0 + s + 1, :]
                e = jnp.where(sub >= s, jnp.exp2(bi - bs), 0.0)
                w = jnp.sum(qi * ks * e, axis=-1, keepdims=True)
                acc = acc + w * vs
            acc_s[hh, r0:r0 + HG_SUB, :] = acc
        o = o + acc_s[hh]

        b_last = b_s[hh, chunk - 1:chunk, :]
        kd = k * jnp.exp2(b_last - b)
        st_new = st * jnp.exp2(b_last) + _dot(v.T.astype(BF16), kd.astype(BF16))
        st_ref[hh] = st_new

        ms = jnp.mean(o * o, axis=-1, keepdims=True)
        o_ref[:, cols] = (o * lax.rsqrt(ms + EPS) * gn_ref[...] * _silu(og_ref[:, cols])).astype(o_ref.dtype)

    @pl.when(c == pl.num_programs(1) - 1)
    def _():
        for hh in range(heads):
            s_out_ref[hh] = st_ref[hh].T


def _hgrn_seq(proj, hg_lb, gn, chunk=HG_CHUNK, heads=HG_HEADS_PER_STEP):
    t = proj.shape[0]
    w = 128 * heads
    blk = lambda off: pl.BlockSpec((chunk, w), lambda h, c, off=off: (c, off // heads + h))
    return pl.pallas_call(
        functools.partial(_hgrn_seq_kernel, chunk=chunk, heads=heads),
        grid=(H_HG // heads, t // chunk),
        in_specs=[blk(COL_HQ), blk(COL_HF), blk(COL_HI), blk(COL_OG),
                  pl.BlockSpec((hg_lb.shape[0], w), lambda h, c: (0, h)),
                  pl.BlockSpec((1, HG_DV), lambda h, c: (0, 0))],
        out_specs=[pl.BlockSpec((chunk, w), lambda h, c: (c, h)),
                   pl.BlockSpec((heads, HG_DK, HG_DV), lambda h, c: (h, 0, 0))],
        out_shape=[jax.ShapeDtypeStruct((t, HG_VW), BF16),
                   jax.ShapeDtypeStruct((H_HG, HG_DK, HG_DV), F32)],
        scratch_shapes=[pltpu.VMEM((heads, HG_DV, HG_DK), F32)] + [pltpu.VMEM((heads, chunk, 128), F32)] * 5,
        compiler_params=_cparams("parallel", "arbitrary"),
        name="hgrn_seq",
    )(proj, proj, proj, proj, hg_lb, gn.reshape(1, HG_DV))


def _hgrn_step_kernel(hqc_ref, hfc_ref, hi_ref, og_ref, lbc_ref, gn_ref, s_ref, o_ref, s_out_ref):
    lbraw = lbc_ref[...]
    e = jnp.exp(lbraw - jnp.max(lbraw, axis=0, keepdims=True))
    lb = e[0] / jnp.sum(e, axis=0)
    q = _silu(hqc_ref[...])
    f = lb + (1.0 - lb) * jax.nn.sigmoid(hfc_ref[...])
    k = 1.0 - f
    v = hi_ref[...]
    s_new = f * s_ref[...] + k * v
    s_out_ref[...] = s_new
    o = jnp.sum(s_new * q, axis=-2, keepdims=True)
    ms = jnp.mean(o * o, axis=-1, keepdims=True)
    o_ref[...] = (o * lax.rsqrt(ms + EPS) * gn_ref[...] * _silu(og_ref[...])).astype(o_ref.dtype)


def _hgrn_step(proj, state, hg_lb, gn):
    nb = proj.shape[0]
    col = lambda a: a.reshape(nb, H_HG, HG_DK, 1)
    rowv = lambda a: a.reshape(nb, H_HG, 1, HG_DV)
    hq = col(proj[:, 0:HG_W])
    hf = col(proj[:, HG_W:2 * HG_W])
    hi = rowv(proj[:, 2 * HG_W:2 * HG_W + HG_VW])
    og = rowv(proj[:, 2 * HG_W + HG_VW:2 * HG_W + 2 * HG_VW])
    nl = hg_lb.shape[0]
    per = 4 if nb % 4 == 0 else 1
    cspec = pl.BlockSpec((per, H_HG, HG_DK, 1), lambda b: (b, 0, 0, 0))
    rspec = pl.BlockSpec((per, H_HG, 1, HG_DV), lambda b: (b, 0, 0, 0))
    sspec = pl.BlockSpec((per, H_HG, HG_DK, HG_DV), lambda b: (b, 0, 0, 0))
    return pl.pallas_call(
        _hgrn_step_kernel,
        grid=(nb // per,),
        in_specs=[cspec, cspec, rspec, rspec,
                  pl.BlockSpec((nl, H_HG, HG_DK, 1), lambda b: (0, 0, 0, 0)),
                  pl.BlockSpec((1, 1, HG_DV), lambda b: (0, 0, 0)),
                  sspec],
        out_specs=[rspec, sspec],
        out_shape=[jax.ShapeDtypeStruct((nb, H_HG, 1, HG_DV), BF16),
                   jax.ShapeDtypeStruct((nb, H_HG, HG_DK, HG_DV), F32)],
        compiler_params=_cparams("parallel"),
        name="hgrn_step",
    )(hq, hf, hi, og, hg_lb.reshape(nl, H_HG, HG_DK, 1), gn.reshape(1, 1, HG_DV), state)


BUCKET_STARTS = list(range(1, MAX_EXACT + 1)) + BUCKET_THRESHOLDS


def _bias_from_rel(rel, lo, hi, tab_ref, h):
    first = sum(1 for start in BUCKET_STARTS if start <= lo)
    out = jnp.full(rel.shape, tab_ref[first, h], F32)
    for bkt in range(first + 1, N_BUCKETS):
        if BUCKET_STARTS[bkt - 1] <= hi:
            out = jnp.where(rel >= BUCKET_STARTS[bkt - 1], tab_ref[bkt, h], out)
    return jnp.where(rel >= 0, out, NEG_INF) if lo < 0 else out


def _bias_tiles_kernel(tab_ref, o_ref, *, base, tile_step, row_step, col_step, scale):
    h = pl.program_id(0)
    n_tiles, rows, cols = o_ref.shape
    r = lax.broadcasted_iota(jnp.int32, (rows, cols), 0)
    c = lax.broadcasted_iota(jnp.int32, (rows, cols), 1)
    in_tile = r * row_step + c * col_step
    spans = [row_step * (rows - 1), col_step * (cols - 1)]
    for t in range(n_tiles):
        off = base + t * tile_step
        lo = off + sum(min(sp, 0) for sp in spans)
        hi = off + sum(max(sp, 0) for sp in spans)
        bias = _bias_from_rel(off + in_tile, lo, hi, tab_ref, h)
        o_ref[t] = bias if scale == 1.0 else bias * scale


def _bias_tiles(rel_bias, n_tiles, rows, cols, base, tile_step, row_step, col_step, scale=1.0):
    return pl.pallas_call(
        functools.partial(_bias_tiles_kernel, base=base, tile_step=tile_step, row_step=row_step,
                          col_step=col_step, scale=scale),
        grid=(H_ATT,),
        in_specs=[pl.BlockSpec(memory_space=pltpu.SMEM)],
        out_specs=pl.BlockSpec((None, n_tiles, rows, cols), lambda h: (h, 0, 0, 0)),
        out_shape=jax.ShapeDtypeStruct((H_ATT, n_tiles, rows, cols), F32),
        compiler_params=_cparams("parallel"),
        name="rel_bias_tiles",
    )(rel_bias)


def _block_mean_kernel(k_ref, o_ref):
    for blk in range(o_ref.shape[0]):
        o_ref[blk] = jnp.mean(k_ref[blk * MOBA_BLOCK:(blk + 1) * MOBA_BLOCK, :], axis=0, keepdims=True)


def _block_mean(keys, blocks_per_step=4):
    t = keys.shape[0]
    nb = t // MOBA_BLOCK
    assert nb % blocks_per_step == 0
    out = pl.pallas_call(
        _block_mean_kernel,
        grid=(nb // blocks_per_step,),
        in_specs=[pl.BlockSpec((blocks_per_step * MOBA_BLOCK, ATT_W), lambda n: (n, 0))],
        out_specs=pl.BlockSpec((blocks_per_step, 1, ATT_W), lambda n: (n, 0, 0)),
        out_shape=jax.ShapeDtypeStruct((nb, 1, ATT_W), F32),
        compiler_params=_cparams("parallel"),
        name="moba_block_mean",
    )(keys)
    return out.reshape(nb, ATT_W)


def _top_blocks(s, n_valid, axis):
    n = float(s.shape[axis])
    idx = lax.broadcasted_iota(jnp.int32, s.shape, axis).astype(F32)
    s = jnp.where(idx < n_valid, s, NEG_INF)
    sel = jnp.zeros(s.shape, F32)
    for _ in range(MOBA_TOPK):
        m = jnp.max(s, axis=axis, keepdims=True)
        first = jnp.min(jnp.where(s == m, idx, n), axis=axis, keepdims=True)
        pick = jnp.logical_and(idx == first, m > NEG_INF)
        sel = jnp.where(pick, 1.0, sel)
        s = jnp.where(pick, NEG_INF, s)
    return sel


def _moba_pen_kernel(tab_ref, q_ref, km_ref, pen_ref, penfar_ref):
    h = pl.program_id(0)
    nblk = km_ref.shape[0]
    tq = q_ref.shape[0]
    scores = _dot_nt(km_ref[...].astype(BF16), (q_ref[...] * ATT_SCALE).astype(BF16))
    pos = pl.program_id(1) * tq + lax.broadcasted_iota(jnp.int32, (1, tq), 1)
    own = pos // MOBA_BLOCK
    blk_id = lax.broadcasted_iota(jnp.int32, scores.shape, 0)
    chosen = _top_blocks(scores, own, axis=0) > 0.5
    pen_ref[0:nblk, :] = jnp.where(jnp.logical_or(chosen, blk_id == own), 0.0, NEG_INF)
    pen_ref[nblk:, :] = jnp.full((pen_ref.shape[0] - nblk, tq), NEG_INF, F32)
    far_bias = tab_ref[N_BUCKETS - 1, h] * LOG2E
    penfar_ref[...] = jnp.where(jnp.logical_and(chosen, blk_id <= own - MOBA_GROUP), far_bias, NEG_INF)


def _moba_pen(proj, kmean, rel_bias, tq=1024):
    t = proj.shape[0]
    nblk = t // MOBA_BLOCK
    return pl.pallas_call(
        _moba_pen_kernel,
        grid=(H_ATT, t // tq),
        in_specs=[pl.BlockSpec(memory_space=pltpu.SMEM),
                  pl.BlockSpec((tq, ATT_DH), lambda h, n: (n, COL_AQ + h)),
                  pl.BlockSpec((nblk, ATT_DH), lambda h, n: (0, h))],
        out_specs=[pl.BlockSpec((None, nblk + 8, tq), lambda h, n: (h, 0, n)),
                   pl.BlockSpec((None, nblk, tq), lambda h, n: (h, 0, n))],
        out_shape=[jax.ShapeDtypeStruct((H_ATT, nblk + 8, t), F32),
                   jax.ShapeDtypeStruct((H_ATT, nblk, t), F32)],
        compiler_params=_cparams("parallel", "parallel"),
        name="moba_block_masks",
    )(rel_bias, proj, kmean)


def _stream_page_sums(pt_ref, pages_hbm, sums_ref, pbuf, psem, step, n_steps, pages_per_block):
    n_pg = pbuf.shape[1]
    total = pt_ref.shape[0]
    slot = step % 2

    def copies(st, sl):
        return [pltpu.make_async_copy(pages_hbm.at[pt_ref[jnp.minimum(st * n_pg + p, total - 1)]],
                                      pbuf.at[sl, p], psem.at[sl, p]) for p in range(n_pg)]

    @pl.when(step == 0)
    def _():
        for cp in copies(step, slot):
            cp.start()

    @pl.when(step < n_steps)
    def _():
        for cp in copies(step, slot):
            cp.wait()

    @pl.when(step + 1 < n_steps)
    def _():
        for cp in copies(step + 1, 1 - slot):
            cp.start()

    def reduce_pages():
        for blk in range(n_pg // pages_per_block):
            tot = pbuf[slot, blk * pages_per_block].sum(axis=0)
            for pg in range(1, pages_per_block):
                tot = tot + pbuf[slot, blk * pages_per_block + pg].sum(axis=0)
            sums_ref[blk] = tot

    return reduce_pages


def _moba_seq_kernel(*refs, n_far, stream):
    if stream:
        n_stream_steps, pages_per_block = stream
        (pt_ref, q_ref, k_ref, v_ref, bias_ref, pen_ref, penfar_ref, pages_hbm, o_ref, sums_ref,
         kb_s, vt_s, acc_s, s_s, pbuf, psem) = refs
        reduce_pages = _stream_page_sums(
            pt_ref, pages_hbm, sums_ref, pbuf, psem,
            pl.program_id(0) * pl.num_programs(1) + pl.program_id(1), n_stream_steps, pages_per_block)
    else:
        q_ref, k_ref, v_ref, bias_ref, pen_ref, penfar_ref, o_ref, kb_s, vt_s, acc_s, s_s = refs
        reduce_pages = None
    i = pl.program_id(1)
    nblk = vt_s.shape[0]

    @pl.when(i == 0)
    def _():
        kb_s[...] = k_ref[...].astype(BF16)

        def xpose(j, carry):
            r0 = pl.multiple_of(j * MOBA_BLOCK, MOBA_BLOCK)
            vt_s[j] = v_ref[pl.ds(r0, MOBA_BLOCK), :].T.astype(BF16)
            return carry

        lax.fori_loop(0, nblk, xpose, 0)

    if reduce_pages is not None:
        reduce_pages()

    tq = q_ref.shape[0]
    grp = MOBA_GROUP
    qb = (q_ref[...] * (ATT_SCALE * LOG2E)).astype(BF16)
    acc_s[...] = jnp.zeros_like(acc_s)

    def logits(j):
        c0 = pl.multiple_of(j * MOBA_BLOCK, MOBA_BLOCK)
        return _dot_nt(kb_s[pl.ds(c0, MOBA_BLOCK), :], qb)

    def keep_max(smax, s):
        cur = jnp.max(s, axis=0, keepdims=True)
        return cur if smax is None else jnp.maximum(smax, cur)

    smax = None
    for u in range(grp):
        j = jnp.maximum(i - u, 0)
        pen_row = jnp.where(u <= i, j, nblk)
        s = logits(j) + bias_ref[min(u, n_far)] + pen_ref[pl.ds(pen_row, 1), :]
        s_s[0, u] = s
        smax = keep_max(smax, s)

    def softmax_update(slot, m_old, l_old, smax, block_of):
        mg = jnp.maximum(m_old, smax)
        alpha = jnp.exp2(m_old - mg)
        l_new = alpha * l_old
        pv = None
        for u in range(grp):
            p = jnp.exp2(s_s[slot, u] - mg)
            l_new = l_new + jnp.sum(p, axis=0, keepdims=True)
            d = _dot(vt_s[block_of(u)], p.astype(BF16))
            pv = d if pv is None else pv + d
        acc_s[...] = alpha * acc_s[...] + pv
        return mg, l_new

    def trip_with_slots(t, carry, slot_prev):
        m_old, l_old, smax_prev = carry
        smax_new = None
        for u in range(grp):
            s = logits(t * grp + u) + penfar_ref[pl.ds(t * grp + u, 1), :]
            s_s[1 - slot_prev, u] = s
            smax_new = keep_max(smax_new, s)
        prev = lambda u: jnp.where(t == 0, jnp.maximum(i - u, 0), (t - 1) * grp + u)
        m_new, l_new = softmax_update(slot_prev, m_old, l_old, smax_prev, prev)
        return m_new, l_new, smax_new

    def trip(t, carry):
        return lax.cond(t % 2 == 0, lambda c: trip_with_slots(t, c, 0), lambda c: trip_with_slots(t, c, 1), carry)

    n_far_groups = i // grp
    init = (jnp.full((1, tq), NEG_INF, F32), jnp.zeros((1, tq), F32), smax)
    m, l, smax = lax.fori_loop(0, n_far_groups, trip, init)
    last = lambda u: jnp.where(n_far_groups == 0, jnp.maximum(i - u, 0), (n_far_groups - 1) * grp + u)
    m, l = softmax_update(n_far_groups % 2, m, l, smax, last)
    o_ref[...] = (acc_s[...] / l).T.astype(o_ref.dtype)


def _moba_seq(proj, keys, values, rel_bias, side_pool=None, side_page_table=None):
    t = proj.shape[0]
    nq = t // MOBA_BLOCK
    kmean = _block_mean(keys)
    n_far = -(-(BUCKET_THRESHOLDS[-1] + MOBA_BLOCK) // MOBA_BLOCK)
    assert MOBA_GROUP >= n_far and nq % MOBA_GROUP == 0
    bias = _bias_tiles(rel_bias, n_far + 1, MOBA_BLOCK, MOBA_BLOCK, 0, MOBA_BLOCK, -1, 1, scale=LOG2E)
    pen, penfar = _moba_pen(proj, kmean, rel_bias)
    n_steps = H_ATT * nq
    in_specs = [pl.BlockSpec((MOBA_BLOCK, ATT_DH), lambda h, i, *_: (i, COL_AQ + h)),
                pl.BlockSpec((t, ATT_DH), lambda h, i, *_: (0, h)),
                pl.BlockSpec((t, ATT_DH), lambda h, i, *_: (0, h)),
                pl.BlockSpec((None, n_far + 1, MOBA_BLOCK, MOBA_BLOCK), lambda h, i, *_: (h, 0, 0, 0)),
                pl.BlockSpec((None, nq + 8, MOBA_BLOCK), lambda h, i, *_: (h, 0, i)),
                pl.BlockSpec((None, nq, MOBA_BLOCK), lambda h, i, *_: (h, 0, i))]
    out_specs = [pl.BlockSpec((MOBA_BLOCK, ATT_DH), lambda h, i, *_: (i, h))]
    out_shape = [jax.ShapeDtypeStruct((t, ATT_W), BF16)]
    scratch = [pltpu.VMEM((t, ATT_DH), BF16), pltpu.VMEM((nq, ATT_DH, MOBA_BLOCK), BF16),
               pltpu.VMEM((ATT_DH, MOBA_BLOCK), F32),
               pltpu.VMEM((2, MOBA_GROUP, MOBA_BLOCK, MOBA_BLOCK), F32)]
    operands = [proj, keys, values, bias, pen, penfar]
    stream = None
    prefetch = []
    if side_pool is not None:
        pt = side_page_table.reshape(-1)
        page_rows = side_pool.shape[1]
        ppb = MOBA_BLOCK // page_rows
        assert ppb * page_rows == MOBA_BLOCK and pt.shape[0] % ppb == 0
        n_pg = ppb * -(-pt.shape[0] // (ppb * n_steps))
        stream = (-(-pt.shape[0] // n_pg), ppb)
        in_specs.append(pl.BlockSpec(memory_space=pl.ANY))
        out_specs.append(pl.BlockSpec((None, n_pg // ppb, H_ATT, ATT_DH), lambda h, i, *_: (h * nq + i, 0, 0, 0)))
        out_shape.append(jax.ShapeDtypeStruct((n_steps, n_pg // ppb, H_ATT, ATT_DH), F32))
        scratch += [pltpu.VMEM((2, n_pg) + side_pool.shape[1:], F32), pltpu.SemaphoreType.DMA((2, n_pg))]
        operands.append(side_pool)
        prefetch = [pt]
    outs = pl.pallas_call(
        functools.partial(_moba_seq_kernel, n_far=n_far, stream=stream),
        grid_spec=pltpu.PrefetchScalarGridSpec(
            num_scalar_prefetch=len(prefetch), grid=(H_ATT, nq), in_specs=in_specs, out_specs=out_specs,
            scratch_shapes=scratch),
        out_shape=out_shape,
        compiler_params=_cparams("arbitrary", "arbitrary"),
        name="moba_seq",
    )(*prefetch, *operands)
    if side_pool is None:
        return outs[0], None
    return outs[0], outs[1].reshape(-1, H_ATT, ATT_DH)[:pt.shape[0] // ppb]


SELECT_PAGES_PER_STEP = 8
FFN_CHUNK = 256
MOBA_STEP_SLOTS = 4


def _moba_select_kernel(ksum_ref, q_ref, sel_ref):
    kmean = ksum_ref[...] / MOBA_BLOCK
    prod = kmean.astype(BF16).astype(F32) * (q_ref[...] * ATT_SCALE).astype(BF16).astype(F32)[None]
    s = jnp.sum(prod, axis=-1, keepdims=True)
    nb = s.shape[0]
    mask = _top_blocks(s, nb, axis=0)
    idx = lax.broadcasted_iota(jnp.int32, s.shape, 0).astype(F32)
    for r in range(MOBA_TOPK):
        first = jnp.min(jnp.where(mask > 0.5, idx, float(nb)), axis=0)
        sel_ref[r] = first.astype(jnp.int32)
        mask = jnp.where(idx == first[None], 0.0, mask)


def _moba_select(block_sums, q_heads):
    nb_seq, n_blocks = block_sums.shape[:2]
    assert n_blocks >= MOBA_TOPK
    return pl.pallas_call(
        _moba_select_kernel,
        grid=(nb_seq,),
        in_specs=[pl.BlockSpec((None, n_blocks, H_ATT, ATT_DH), lambda b: (b, 0, 0, 0)),
                  pl.BlockSpec((None, H_ATT, ATT_DH), lambda b: (b, 0, 0))],
        out_specs=pl.BlockSpec((None, MOBA_TOPK, H_ATT, 1), lambda b: (b, 0, 0, 0)),
        out_shape=jax.ShapeDtypeStruct((nb_seq, MOBA_TOPK, H_ATT, 1), jnp.int32),
        compiler_params=_cparams("parallel"),
        name="moba_select",
    )(block_sums, q_heads)


def _page_block_sums_kernel(pt_ref, *refs, pages_per_block):
    page_refs, o_ref = refs[:-1], refs[-1]
    for blk in range(len(page_refs) // pages_per_block):
        tot = page_refs[blk * pages_per_block][...].sum(axis=0)
        for pg in range(1, pages_per_block):
            tot = tot + page_refs[blk * pages_per_block + pg][...].sum(axis=0)
        o_ref[blk] = tot


def _page_block_sums(pool, page_table):
    nb_seq, n_pages = page_table.shape
    page_rows = pool.shape[1]
    ppb = MOBA_BLOCK // page_rows
    n_pg = SELECT_PAGES_PER_STEP
    assert ppb * page_rows == MOBA_BLOCK and n_pg % ppb == 0 and n_pages % n_pg == 0
    kspec = lambda p: pl.BlockSpec((None,) + pool.shape[1:],
                                   lambda b, n, pt, p=p: (pt[b * n_pages + n_pg * n + p], 0, 0, 0))
    return pl.pallas_call(
        functools.partial(_page_block_sums_kernel, pages_per_block=ppb),
        grid_spec=pltpu.PrefetchScalarGridSpec(
            num_scalar_prefetch=1,
            grid=(nb_seq, n_pages // n_pg),
            in_specs=[kspec(p) for p in range(n_pg)],
            out_specs=pl.BlockSpec((None, n_pg // ppb, H_ATT, ATT_DH), lambda b, n, pt: (b, n, 0, 0))),
        out_shape=jax.ShapeDtypeStruct((nb_seq, n_pages // ppb, H_ATT, ATT_DH), F32),
        compiler_params=_cparams("parallel", "arbitrary"),
        name="page_block_sums",
    )(page_table.reshape(-1), *([pool] * n_pg))


def _moba_step_kernel(pt_ref, sel_ref, q_ref, kn_ref, vn_ref, b0_ref, b1_ref, b2_ref, bown_ref, ck_hbm, cv_hbm,
                      o_ref, kbuf, vbuf, sem, *, n_pages):
    n_pg = MOBA_TOPK * 2
    b_refs = (b0_ref, b1_ref, b2_ref)
    n_slots = kbuf.shape[0]
    page_rows = kbuf.shape[2]
    b = pl.program_id(0)
    h = pl.program_id(1)
    n_heads = pl.num_programs(1)
    n_steps = pl.num_programs(0) * n_heads
    step = b * n_heads + h
    slot = step % n_slots

    def page_copies(bb, hh, sl):
        cps = []
        for r in range(MOBA_TOPK):
            blk = sel_ref[(bb * MOBA_TOPK + r) * n_heads + hh]
            for half in range(2):
                page = pt_ref[bb * n_pages + 2 * blk + half]
                idx = 2 * r + half
                cps.append(pltpu.make_async_copy(ck_hbm.at[page, :, hh, :], kbuf.at[sl, idx], sem.at[sl, idx]))
                cps.append(pltpu.make_async_copy(cv_hbm.at[page, :, hh, :], vbuf.at[sl, idx], sem.at[sl, n_pg + idx]))
        return cps

    def start_step(st):
        for cp in page_copies(st // n_heads, st % n_heads, st % n_slots):
            cp.start()

    @pl.when(step == 0)
    def _():
        for ahead in range(n_slots - 1):
            @pl.when(ahead < n_steps)
            def _():
                start_step(jnp.int32(ahead))

    @pl.when(step + n_slots - 1 < n_steps)
    def _():
        start_step(step + n_slots - 1)

    for cp in page_copies(b, h, slot):
        cp.wait()

    qs = q_ref[...] * ATT_SCALE
    eye = (lax.broadcasted_iota(jnp.int32, (page_rows, page_rows), 0)
           == lax.broadcasted_iota(jnp.int32, (page_rows, page_rows), 1))
    logits = []
    for r in range(MOBA_TOPK):
        bias = b_refs[r][...]
        for half in range(2):
            lg = jnp.sum(kbuf[slot, 2 * r + half] * qs, axis=-1, keepdims=True)
            brow = bias[:, half * page_rows:(half + 1) * page_rows]
            bcol = jnp.sum(jnp.where(eye, brow, 0.0), axis=-1, keepdims=True)
            logits.append(lg + bcol)
    l_own = jnp.sum(qs * kn_ref[...], axis=-1, keepdims=True) + bown_ref[:, 0:1]
    m = l_own
    for lg in logits:
        m = jnp.maximum(m, jnp.max(lg, axis=0, keepdims=True))
    p_own = jnp.exp(l_own - m)
    den = p_own
    acc = p_own * vn_ref[...]
    for idx, lg in enumerate(logits):
        p = jnp.exp(lg - m)
        den = den + jnp.sum(p, axis=0, keepdims=True)
        acc = acc + jnp.sum(p * vbuf[slot, idx], axis=0, keepdims=True)
    o_ref[...] = (acc / den).astype(o_ref.dtype)


def _moba_step(proj, ak, av, block_sums, cache_k4, cache_v4, page_table, rel_bias):
    nb_seq, n_pages = page_table.shape
    page_rows = cache_k4.shape[1]
    ppb = MOBA_BLOCK // page_rows
    assert ppb == 2
    n_blocks = n_pages // ppb
    n_past = n_pages * page_rows
    head_rows = lambda a: a.reshape(nb_seq, H_ATT, 1, ATT_DH)
    aq = proj[:, COL_AQ * 128:COL_AQ * 128 + ATT_W]
    sel = _moba_select(block_sums.reshape(nb_seq, n_blocks, H_ATT, ATT_DH), aq.reshape(nb_seq, H_ATT, ATT_DH))
    bias = _bias_tiles(rel_bias, n_blocks + 1, 1, MOBA_BLOCK, n_past, -MOBA_BLOCK, 0, -1)

    def bias_spec(r):
        return pl.BlockSpec((None, None, 1, MOBA_BLOCK),
                            lambda b, h, pt, sl: (h, sl[(b * MOBA_TOPK + r) * H_ATT + h], 0, 0))

    row_spec = pl.BlockSpec((None, None, 1, ATT_DH), lambda b, h, pt, sl: (b, h, 0, 0))
    hbm = pl.BlockSpec(memory_space=pl.ANY)
    n_pg = 2 * MOBA_TOPK
    out = pl.pallas_call(
        functools.partial(_moba_step_kernel, n_pages=n_pages),
        grid_spec=pltpu.PrefetchScalarGridSpec(
            num_scalar_prefetch=2,
            grid=(nb_seq, H_ATT),
            in_specs=[row_spec, row_spec, row_spec] + [bias_spec(r) for r in range(MOBA_TOPK)]
                     + [pl.BlockSpec((None, None, 1, MOBA_BLOCK), lambda b, h, pt, sl: (h, n_blocks, 0, 0)),
                        hbm, hbm],
            out_specs=row_spec,
            scratch_shapes=[pltpu.VMEM((MOBA_STEP_SLOTS, n_pg, page_rows, ATT_DH), F32),
                            pltpu.VMEM((MOBA_STEP_SLOTS, n_pg, page_rows, ATT_DH), F32),
                            pltpu.SemaphoreType.DMA((MOBA_STEP_SLOTS, 2 * n_pg))]),
        out_shape=jax.ShapeDtypeStruct((nb_seq, H_ATT, 1, ATT_DH), BF16),
        compiler_params=_cparams("arbitrary", "arbitrary"),
        name="moba_step",
    )(page_table.reshape(-1), sel.reshape(-1), head_rows(aq), head_rows(ak), head_rows(av),
      bias, bias, bias, bias, cache_k4, cache_v4)
    return out.reshape(nb_seq, ATT_W)


def _merge_out_kernel(oh_ref, oa_ref, gh0_ref, gh1_ref, ga0_ref, ga1_ref, wh_ref, wa_ref, wo_ref, x_ref, g_ref,
                      gnext_ref, o_ref, h_ref):
    half = D_MODEL // 2
    br_h = _dot(oh_ref[...], wh_ref[...])
    br_a = _dot(oa_ref[...], wa_ref[...])
    merged = jnp.concatenate(
        [(jax.nn.sigmoid(gh_ref[...]) * br_h[:, c * half:(c + 1) * half]
          + jax.nn.sigmoid(ga_ref[...]) * br_a[:, c * half:(c + 1) * half]).astype(BF16)
         for c, (gh_ref, ga_ref) in enumerate(((gh0_ref, ga0_ref), (gh1_ref, ga1_ref)))], axis=1)
    z = _dot(merged, wo_ref[...])
    ms = jnp.mean(z * z, axis=-1, keepdims=True)
    x1 = x_ref[...] + z * lax.rsqrt(ms + EPS) * g_ref[...]
    o_ref[...] = x1
    ms1 = jnp.mean(x1 * x1, axis=-1, keepdims=True)
    h_ref[...] = (x1 * lax.rsqrt(ms1 + EPS) * gnext_ref[...]).astype(h_ref.dtype)


def _merge_out(o_hg, o_att, proj, w_bh, w_bm, w_out, x, gain, gain_next, tm):
    m = x.shape[0]
    half = D_MODEL // 2
    assert half == 1024
    resident = pl.Buffered(1)
    vec = pl.BlockSpec((1, D_MODEL), lambda i: (0, 0), pipeline_mode=resident)
    row = pl.BlockSpec((tm, D_MODEL), lambda i: (i, 0))
    gate = lambda blk: pl.BlockSpec((tm, half), lambda i, blk=blk: (i, blk))
    return pl.pallas_call(
        _merge_out_kernel,
        grid=(m // tm,),
        in_specs=[pl.BlockSpec((tm, HG_VW), lambda i: (i, 0)), pl.BlockSpec((tm, ATT_W), lambda i: (i, 0)),
                  gate(COLK_GHG), gate(COLK_GHG + 1), gate(COLK_GATT), gate(COLK_GATT + 1),
                  pl.BlockSpec((HG_VW, D_MODEL), lambda i: (0, 0), pipeline_mode=resident),
                  pl.BlockSpec((ATT_W, D_MODEL), lambda i: (0, 0), pipeline_mode=resident),
                  pl.BlockSpec((D_MODEL, D_MODEL), lambda i: (0, 0), pipeline_mode=resident),
                  row, vec, vec],
        out_specs=[row, row],
        out_shape=[jax.ShapeDtypeStruct((m, D_MODEL), F32), jax.ShapeDtypeStruct((m, D_MODEL), BF16)],
        compiler_params=_cparams("parallel"),
        name="merge_out_proj_residual",
    )(o_hg, o_att, proj, proj, proj, proj, w_bh, w_bm, w_out, x, gain.reshape(1, D_MODEL),
      gain_next.reshape(1, D_MODEL))


def _ffn_up_kernel(h_ref, wg_ref, wu_ref, cw_ref, cb_ref, p0_ref, p1_ref, a_ref, g_out_ref, carry_s, *, seq):
    i = pl.program_id(0)
    j = pl.program_id(1)
    tm, tf = a_ref.shape
    if seq:
        @pl.when(i == 0)
        def _():
            carry_s[j, 0:1, :] = p0_ref[...]
            carry_s[j, 1:2, :] = p1_ref[...]

    top = lax.broadcasted_iota(jnp.int32, (8, FFN_CHUNK), 0)
    for c in range(tf // FFN_CHUNK):
        cols = slice(c * FFN_CHUNK, (c + 1) * FFN_CHUNK)
        g = _dot(h_ref[...], wg_ref[:, cols].astype(BF16))
        u = _dot(h_ref[...], wu_ref[:, cols].astype(BF16))
        if seq:
            c0 = carry_s[j, 0:1, cols]
            c1 = carry_s[j, 1:2, cols]
            r1 = pltpu.roll(g, 1, axis=0)
            r2 = pltpu.roll(g, 2, axis=0)
            prev1 = jnp.concatenate([jnp.where(top == 0, c1, r1[0:8]), r1[8:]], axis=0)
            prev2 = jnp.concatenate([jnp.where(top == 0, c0, jnp.where(top == 1, c1, r2[0:8])), r2[8:]], axis=0)
            carry_s[j, 0:2, cols] = g[tm - 2:tm, :]
            g_out_ref[:, cols] = g[tm - 2:tm, :]
        else:
            prev2 = p0_ref[:, cols]
            prev1 = p1_ref[:, cols]
            g_out_ref[:, cols] = g
        gc = cb_ref[:, cols] + cw_ref[0:1, cols] * prev2 + cw_ref[1:2, cols] * prev1 + cw_ref[2:3, cols] * g
        a_ref[:, cols] = (jax.nn.gelu(gc, approximate=True) * u).astype(a_ref.dtype)


def _ffn_down_kernel(a_ref, wd_ref, x_ref, g_ref, o_ref):
    y = _dot(a_ref[...], wd_ref[...])
    ms = jnp.mean(y * y, axis=-1, keepdims=True)
    o_ref[...] = x_ref[...] + y * lax.rsqrt(ms + EPS) * g_ref[...]


def _ffn(x, h, w_up, conv_w, conv_b, prev0, prev1, w_down, gpost, *, seq, tm_up, tm_down, tf=512):
    m = x.shape[0]
    nf = D_FF // tf
    assert CONV_W == 3 and nf * tf == D_FF
    prow = 1 if seq else tm_up
    pspec = pl.BlockSpec((prow, tf), (lambda i, j: (0, j)) if seq else (lambda i, j: (i, j)))
    if seq:
        gspec = pl.BlockSpec((None, CONV_W - 1, tf), lambda i, j: (i, 0, j))
        gshape = (m // tm_up, CONV_W - 1, D_FF)
    else:
        gspec = pl.BlockSpec((tm_up, tf), lambda i, j: (i, j))
        gshape = (m, D_FF)
    act, g_out = pl.pallas_call(
        functools.partial(_ffn_up_kernel, seq=seq),
        grid=(m // tm_up, nf),
        in_specs=[pl.BlockSpec((tm_up, D_MODEL), lambda i, j: (i, 0)),
                  pl.BlockSpec((D_MODEL, tf), lambda i, j: (0, j)),
                  pl.BlockSpec((D_MODEL, tf), lambda i, j: (0, nf + j)),
                  pl.BlockSpec((CONV_W, tf), lambda i, j: (0, j)),
                  pl.BlockSpec((1, tf), lambda i, j: (0, j)),
                  pspec, pspec],
        out_specs=[pl.BlockSpec((tm_up, tf), lambda i, j: (i, j)), gspec],
        out_shape=[jax.ShapeDtypeStruct((m, D_FF), BF16), jax.ShapeDtypeStruct(gshape, F32)],
        scratch_shapes=[pltpu.VMEM((nf, 8, tf), F32)],
        compiler_params=_cparams("arbitrary", "arbitrary"),
        name="conv_ffn_up",
    )(h, w_up, w_up, conv_w, conv_b.reshape(1, D_FF), prev0, prev1)
    resident = pl.Buffered(1)
    y = pl.pallas_call(
        _ffn_down_kernel,
        grid=(m // tm_down,),
        in_specs=[pl.BlockSpec((tm_down, D_FF), lambda i: (i, 0)),
                  pl.BlockSpec((D_FF, D_MODEL), lambda i: (0, 0), pipeline_mode=resident),
                  pl.BlockSpec((tm_down, D_MODEL), lambda i: (i, 0)),
                  pl.BlockSpec((1, D_MODEL), lambda i: (0, 0), pipeline_mode=resident)],
        out_specs=pl.BlockSpec((tm_down, D_MODEL), lambda i: (i, 0)),
        out_shape=jax.ShapeDtypeStruct((m, D_MODEL), F32),
        compiler_params=_cparams("parallel"),
        name="ffn_down_residual",
    )(act, w_down, x, gpost.reshape(1, D_MODEL))
    return y, (g_out[-1] if seq else g_out)


def kernel(x_prompt, x_sample, cache_k, cache_v, state_hgrn, state_ffn_conv, page_table, rel_bias, hg_lb,
           norm_mix_pre, norm_mix_post, norm_ffn_pre, norm_ffn_post, w_in, hg_out_norm, w_branch_hgrn,
           w_branch_moba, w_out, w_ffn_up, ffn_conv_w, ffn_conv_b, w_ffn_down):
    nb, t, _ = x_prompt.shape
    db = x_sample.shape[0]
    depth = w_in.shape[0]
    assert nb == 1 and depth == 1 and x_sample.shape[1] == 1
    l = 0
    w_in_b = w_in[l].astype(BF16)
    w_bh = w_branch_hgrn[l].astype(BF16)
    w_bm = w_branch_moba[l].astype(BF16)
    w_o = w_out[l].astype(BF16)
    w_up = w_ffn_up[l]
    w_dn = w_ffn_down[l].astype(BF16)

    n_pool, page_rows = cache_k.shape[1], cache_k.shape[2]
    ck = cache_k.reshape(depth * n_pool, page_rows, H_ATT, ATT_DH)
    cv = cache_v.reshape(depth * n_pool, page_rows, H_ATT, ATT_DH)

    xp = x_prompt.reshape(t, D_MODEL)
    proj, k_new, v_new = _norm_matmul(xp, norm_mix_pre[l], w_in_b, tm=1024)
    o_hg, s_new = _hgrn_seq(proj, hg_lb, hg_out_norm[l])
    o_att, cache_block_sums = _moba_seq(proj, k_new, v_new, rel_bias, side_pool=ck, side_page_table=page_table)
    x1, h2 = _merge_out(o_hg, o_att, proj, w_bh, w_bm, w_o, xp, norm_mix_post[l], norm_ffn_pre[l], tm=256)
    zero_row = jnp.zeros((1, D_FF), F32)
    yp, conv_p = _ffn(x1, h2, w_up, ffn_conv_w[l], ffn_conv_b[l], zero_row, zero_row, w_dn,
                      norm_ffn_post[l], seq=True, tm_up=1024, tm_down=256)
    kp = k_new.reshape(1, 1, t, H_ATT, ATT_DH)
    vp = v_new.reshape(1, 1, t, H_ATT, ATT_DH)

    xs = x_sample.reshape(db, D_MODEL)
    projs, k_new_s, v_new_s = _norm_matmul(xs, norm_mix_pre[l], w_in_b, tm=db)
    o_hg_s, s_new_s = _hgrn_step(projs, state_hgrn[l], hg_lb, hg_out_norm[l])
    o_att_s = _moba_step(projs, k_new_s, v_new_s, cache_block_sums, ck, cv, page_table, rel_bias)
    x1s, h2s = _merge_out(o_hg_s.reshape(db, HG_VW), o_att_s, projs, w_bh, w_bm, w_o, xs, norm_mix_post[l],
                          norm_ffn_pre[l], tm=db)
    buf = state_ffn_conv[l]
    ys, g_s = _ffn(x1s, h2s, w_up, ffn_conv_w[l], ffn_conv_b[l], buf[:, 0], buf[:, 1], w_dn,
                   norm_ffn_post[l], seq=False, tm_up=db, tm_down=db)
    ks = k_new_s.reshape(1, db, 1, H_ATT, ATT_DH)
    vs = v_new_s.reshape(1, db, 1, H_ATT, ATT_DH)
    conv_s = jnp.stack([buf[:, 1], g_s], axis=1)

    return (yp.reshape(1, t, D_MODEL), ys.reshape(db, 1, D_MODEL), kp, vp,
            s_new.reshape(1, 1, H_HG, HG_DK, HG_DV), conv_p.reshape(1, 1, CONV_W - 1, D_FF),
            ks, vs, s_new_s.reshape(1, db, H_HG, HG_DK, HG_DV), conv_s.reshape(1, db, CONV_W - 1, D_FF))
```

```python
import functools
import math

import numpy as np
import jax
import jax.numpy as jnp
from jax import lax
from jax.experimental import pallas as pl
from jax.experimental.pallas import tpu as pltpu

F32 = jnp.float32
BF16 = jnp.bfloat16

D_MODEL = 2048
H_HG = 8
HG_DK = 128
HG_DV = 128
HG_W = H_HG * HG_DK
HG_VW = H_HG * HG_DV
H_ATT = 8
ATT_DH = 128
ATT_W = H_ATT * ATT_DH
MOBA_BLOCK = 256
MOBA_TOPK = 3
N_BUCKETS = 32
MAX_EXACT = N_BUCKETS // 2
REL_MAX_DIST = 1024
D_FF = 5632
CONV_W = 3
EPS = 1e-6
N_IN = 2 * HG_W + 2 * HG_VW + 3 * ATT_W + 2 * D_MODEL
ATT_SCALE = ATT_DH ** -0.5
LOG2E = math.log2(math.e)

COL_HQ, COL_HF, COL_HI, COL_OG = 0, 8, 16, 24
COL_AQ = 32
COLK_AK, COLK_AV = 5, 6
COLK_GHG, COLK_GATT = 5, 7

HG_CHUNK = 128
HG_SUB = 8
HG_HEADS_PER_STEP = 8
MOBA_GROUP = 8
VMEM_LIMIT = 56 * 1024 * 1024
NEG_INF = float("-inf")


def _bucket_thresholds():
    n = np.arange(MAX_EXACT, 4 * REL_MAX_DIST, dtype=np.float64)
    large = MAX_EXACT + (np.log(n / MAX_EXACT) / math.log(REL_MAX_DIST / MAX_EXACT)
                         * (N_BUCKETS - MAX_EXACT)).astype(np.int64)
    large = np.minimum(large, N_BUCKETS - 1)
    return [int(n[np.argmax(large >= b)]) for b in range(MAX_EXACT + 1, N_BUCKETS)]


BUCKET_THRESHOLDS = _bucket_thresholds()


def _cparams(*sem):
    return pltpu.CompilerParams(dimension_semantics=sem, vmem_limit_bytes=VMEM_LIMIT)


def _silu(x):
    return x * jax.nn.sigmoid(x)


def _dot(a, b):
    return jnp.dot(a, b, preferred_element_type=F32)


def _dot_nt(a, b, precision=None):
    return lax.dot_general(a, b, (((1,), (1,)), ((), ())), precision=precision,
                           preferred_element_type=F32)


def _norm_matmul_kernel(x_ref, g_ref, w_ref, o_ref, k_ref, v_ref, h_ref):
    j = pl.program_id(1)

    @pl.when(j == 0)
    def _():
        x = x_ref[...]
        ms = jnp.mean(x * x, axis=-1, keepdims=True)
        h_ref[...] = (x * lax.rsqrt(ms + EPS) * g_ref[...]).astype(BF16)

    @pl.when(j == COLK_AK)
    def _():
        k_ref[...] = _dot(h_ref[...], w_ref[...])

    @pl.when(j == COLK_AV)
    def _():
        v_ref[...] = _dot(h_ref[...], w_ref[...])

    @pl.when(jnp.logical_and(j != COLK_AK, j != COLK_AV))
    def _():
        o_ref[...] = _dot(h_ref[...], w_ref[...])


def _norm_matmul(x, gain, w_bf16, tm):
    m, k = x.shape
    n = w_bf16.shape[1]
    tn = ATT_W
    assert COLK_AV == COLK_AK + 1
    return pl.pallas_call(
        _norm_matmul_kernel,
        grid=(m // tm, n // tn),
        in_specs=[pl.BlockSpec((tm, k), lambda i, j: (i, 0)),
                  pl.BlockSpec((1, k), lambda i, j: (0, 0)),
                  pl.BlockSpec((k, tn), lambda i, j: (0, j))],
        out_specs=[pl.BlockSpec((tm, tn), lambda i, j: (i, jnp.where(j < COLK_AK, j, jnp.maximum(j - 2, COLK_AK - 1)))),
                   pl.BlockSpec((tm, tn), lambda i, j: (i, 0)),
                   pl.BlockSpec((tm, tn), lambda i, j: (i, 0))],
        out_shape=[jax.ShapeDtypeStruct((m, n - 2 * tn), F32), jax.ShapeDtypeStruct((m, tn), F32),
                   jax.ShapeDtypeStruct((m, tn), F32)],
        scratch_shapes=[pltpu.VMEM((tm, k), BF16)],
        compiler_params=_cparams("parallel", "arbitrary"),
        name="norm_in_proj",
    )(x, gain.reshape(1, k), w_bf16)


def _forget_lower_bound(lbraw):
    e = jnp.exp(lbraw - jnp.max(lbraw, axis=0, keepdims=True))
    return e[0:1] / jnp.sum(e, axis=0, keepdims=True)


def _hgrn_seq_kernel(hq_ref, hf_ref, hi_ref, og_ref, lbraw_ref, gn_ref, o_ref, s_out_ref,
                     st_ref, q_s, k_s, b_s, v_s, acc_s, *, chunk, heads):
    c = pl.program_id(1)

    @pl.when(c == 0)
    def _():
        st_ref[...] = jnp.zeros_like(st_ref)

    row = lax.broadcasted_iota(jnp.int32, (chunk, HG_DK), 0)
    rr = lax.broadcasted_iota(jnp.int32, (chunk, chunk), 0)
    cc = lax.broadcasted_iota(jnp.int32, (chunk, chunk), 1)
    sub = lax.broadcasted_iota(jnp.int32, (HG_SUB, HG_DK), 0)

    for hh in range(heads):
        cols = slice(hh * 128, (hh + 1) * 128)
        lb = _forget_lower_bound(lbraw_ref[:, cols])
        q = _silu(hq_ref[:, cols])
        f = lb + (1.0 - lb) * jax.nn.sigmoid(hf_ref[:, cols])
        k = 1.0 - f
        v = hi_ref[:, cols]

        b = jnp.log2(f)
        shift = 1
        while shift < chunk:
            b = b + jnp.where(row >= shift, pltpu.roll(b, shift, axis=0), 0.0)
            shift *= 2

        q_s[hh] = q
        k_s[hh] = k
        b_s[hh] = b
        v_s[hh] = v

        st = st_ref[hh]
        o = _dot_nt((q * jnp.exp2(b)).astype(BF16), st.astype(BF16))

        a = jnp.zeros((chunk, chunk), F32)
        hs = HG_SUB
        while hs < chunk:
            blk = 2 * hs
            ref_rows = jnp.concatenate(
                [jnp.broadcast_to(b_s[hh, m0 + hs - 1:m0 + hs, :], (blk, HG_DK)) for m0 in range(0, chunk, blk)],
                axis=0)
            second = (row & (blk - 1)) >= hs
            qd = jnp.where(second, q * jnp.exp2(jnp.minimum(b - ref_rows, 0.0)), 0.0)
            kd = jnp.where(second, 0.0, k * jnp.exp2(jnp.minimum(ref_rows - b, 0.0)))
            a_l = _dot_nt(qd.astype(BF16), kd.astype(BF16))
            sh = int(math.log2(blk))
            a = a + jnp.where((rr >> sh) == (cc >> sh), a_l, 0.0)
            hs = blk
        o = o + _dot(a.astype(BF16), v.astype(BF16))

        for r0 in range(0, chunk, HG_SUB):
            qi = q_s[hh, r0:r0 + HG_SUB, :]
            bi = b_s[hh, r0:r0 + HG_SUB, :]
            acc = jnp.zeros((HG_SUB, HG_DV), F32)
            for s in range(HG_SUB):
                ks = k_s[hh, r0 + s:r0 + s + 1, :]
                bs = b_s[hh, r0 + s:r0 + s + 1, :]
                vs = v_s[hh, r0 + s:r0 + s + 1, :]
                e = jnp.where(sub >= s, jnp.exp2(bi - bs), 0.0)
                w = jnp.sum(qi * ks * e, axis=-1, keepdims=True)
                acc = acc + w * vs
            acc_s[hh, r0:r0 + HG_SUB, :] = acc
        o = o + acc_s[hh]

        b_last = b_s[hh, chunk - 1:chunk, :]
        kd = k * jnp.exp2(b_last - b)
        st_new = st * jnp.exp2(b_last) + _dot(v.T.astype(BF16), kd.astype(BF16))
        st_ref[hh] = st_new

        ms = jnp.mean(o * o, axis=-1, keepdims=True)
        o_ref[:, cols] = (o * lax.rsqrt(ms + EPS) * gn_ref[...] * _silu(og_ref[:, cols])).astype(o_ref.dtype)

    @pl.when(c == pl.num_programs(1) - 1)
    def _():
        for hh in range(heads):
            s_out_ref[hh] = st_ref[hh].T


def _hgrn_seq(proj, hg_lb, gn, chunk=HG_CHUNK, heads=HG_HEADS_PER_STEP):
    t = proj.shape[0]
    w = 128 * heads
    blk = lambda off: pl.BlockSpec((chunk, w), lambda h, c, off=off: (c, off // heads + h))
    return pl.pallas_call(
        functools.partial(_hgrn_seq_kernel, chunk=chunk, heads=heads),
        grid=(H_HG // heads, t // chunk),
        in_specs=[blk(COL_HQ), blk(COL_HF), blk(COL_HI), blk(COL_OG),
                  pl.BlockSpec((hg_lb.shape[0], w), lambda h, c: (0, h)),
                  pl.BlockSpec((1, HG_DV), lambda h, c: (0, 0))],
        out_specs=[pl.BlockSpec((chunk, w), lambda h, c: (c, h)),
                   pl.BlockSpec((heads, HG_DK, HG_DV), lambda h, c: (h, 0, 0))],
        out_shape=[jax.ShapeDtypeStruct((t, HG_VW), BF16),
                   jax.ShapeDtypeStruct((H_HG, HG_DK, HG_DV), F32)],
        scratch_shapes=[pltpu.VMEM((heads, HG_DV, HG_DK), F32)] + [pltpu.VMEM((heads, chunk, 128), F32)] * 5,
        compiler_params=_cparams("parallel", "arbitrary"),
        name="hgrn_seq",
    )(proj, proj, proj, proj, hg_lb, gn.reshape(1, HG_DV))


def _hgrn_step_kernel(hqc_ref, hfc_ref, hi_ref, og_ref, lbc_ref, gn_ref, s_ref, o_ref, s_out_ref):
    lbraw = lbc_ref[...]
    e = jnp.exp(lbraw - jnp.max(lbraw, axis=0, keepdims=True))
    lb = e[0] / jnp.sum(e, axis=0)
    q = _silu(hqc_ref[...])
    f = lb + (1.0 - lb) * jax.nn.sigmoid(hfc_ref[...])
    k = 1.0 - f
    v = hi_ref[...]
    s_new = f * s_ref[...] + k * v
    s_out_ref[...] = s_new
    o = jnp.sum(s_new * q, axis=-2, keepdims=True)
    ms = jnp.mean(o * o, axis=-1, keepdims=True)
    o_ref[...] = (o * lax.rsqrt(ms + EPS) * gn_ref[...] * _silu(og_ref[...])).astype(o_ref.dtype)


def _hgrn_step(proj, state, hg_lb, gn):
    nb = proj.shape[0]
    col = lambda a: a.reshape(nb, H_HG, HG_DK, 1)
    rowv = lambda a: a.reshape(nb, H_HG, 1, HG_DV)
    hq = col(proj[:, 0:HG_W])
    hf = col(proj[:, HG_W:2 * HG_W])
    hi = rowv(proj[:, 2 * HG_W:2 * HG_W + HG_VW])
    og = rowv(proj[:, 2 * HG_W + HG_VW:2 * HG_W + 2 * HG_VW])
    nl = hg_lb.shape[0]
    per = 4 if nb % 4 == 0 else 1
    cspec = pl.BlockSpec((per, H_HG, HG_DK, 1), lambda b: (b, 0, 0, 0))
    rspec = pl.BlockSpec((per, H_HG, 1, HG_DV), lambda b: (b, 0, 0, 0))
    sspec = pl.BlockSpec((per, H_HG, HG_DK, HG_DV), lambda b: (b, 0, 0, 0))
    return pl.pallas_call(
        _hgrn_step_kernel,
        grid=(nb // per,),
        in_specs=[cspec, cspec, rspec, rspec,
                  pl.BlockSpec((nl, H_HG, HG_DK, 1), lambda b: (0, 0, 0, 0)),
                  pl.BlockSpec((1, 1, HG_DV), lambda b: (0, 0, 0)),
                  sspec],
        out_specs=[rspec, sspec],
        out_shape=[jax.ShapeDtypeStruct((nb, H_HG, 1, HG_DV), BF16),
                   jax.ShapeDtypeStruct((nb, H_HG, HG_DK, HG_DV), F32)],
        compiler_params=_cparams("parallel"),
        name="hgrn_step",
    )(hq, hf, hi, og, hg_lb.reshape(nl, H_HG, HG_DK, 1), gn.reshape(1, 1, HG_DV), state)


BUCKET_STARTS = list(range(1, MAX_EXACT + 1)) + BUCKET_THRESHOLDS


def _bias_from_rel(rel, lo, hi, tab_ref, h):
    first = sum(1 for start in BUCKET_STARTS if start <= lo)
    out = jnp.full(rel.shape, tab_ref[first, h], F32)
    for bkt in range(first + 1, N_BUCKETS):
        if BUCKET_STARTS[bkt - 1] <= hi:
            out = jnp.where(rel >= BUCKET_STARTS[bkt - 1], tab_ref[bkt, h], out)
    return jnp.where(rel >= 0, out, NEG_INF) if lo < 0 else out


def _bias_tiles_kernel(tab_ref, o_ref, *, base, tile_step, row_step, col_step, scale):
    h = pl.program_id(0)
    n_tiles, rows, cols = o_ref.shape
    r = lax.broadcasted_iota(jnp.int32, (rows, cols), 0)
    c = lax.broadcasted_iota(jnp.int32, (rows, cols), 1)
    in_tile = r * row_step + c * col_step
    spans = [row_step * (rows - 1), col_step * (cols - 1)]
    for t in range(n_tiles):
        off = base + t * tile_step
        lo = off + sum(min(sp, 0) for sp in spans)
        hi = off + sum(max(sp, 0) for sp in spans)
        bias = _bias_from_rel(off + in_tile, lo, hi, tab_ref, h)
        o_ref[t] = bias if scale == 1.0 else bias * scale


def _bias_tiles(rel_bias, n_tiles, rows, cols, base, tile_step, row_step, col_step, scale=1.0):
    return pl.pallas_call(
        functools.partial(_bias_tiles_kernel, base=base, tile_step=tile_step, row_step=row_step,
                          col_step=col_step, scale=scale),
        grid=(H_ATT,),
        in_specs=[pl.BlockSpec(memory_space=pltpu.SMEM)],
        out_specs=pl.BlockSpec((None, n_tiles, rows, cols), lambda h: (h, 0, 0, 0)),
        out_shape=jax.ShapeDtypeStruct((H_ATT, n_tiles, rows, cols), F32),
        compiler_params=_cparams("parallel"),
        name="rel_bias_tiles",
    )(rel_bias)


def _block_mean_kernel(k_ref, o_ref):
    for blk in range(o_ref.shape[0]):
        o_ref[blk] = jnp.mean(k_ref[blk * MOBA_BLOCK:(blk + 1) * MOBA_BLOCK, :], axis=0, keepdims=True)


def _block_mean(keys, blocks_per_step=8):
    t = keys.shape[0]
    nb = t // MOBA_BLOCK
    assert nb % blocks_per_step == 0
    out = pl.pallas_call(
        _block_mean_kernel,
        grid=(nb // blocks_per_step,),
        in_specs=[pl.BlockSpec((blocks_per_step * MOBA_BLOCK, ATT_W), lambda n: (n, 0))],
        out_specs=pl.BlockSpec((blocks_per_step, 1, ATT_W), lambda n: (n, 0, 0)),
        out_shape=jax.ShapeDtypeStruct((nb, 1, ATT_W), F32),
        compiler_params=_cparams("parallel"),
        name="moba_block_mean",
    )(keys)
    return out.reshape(nb, ATT_W)


def _top_blocks(s, n_valid, axis):
    n = float(s.shape[axis])
    idx = lax.broadcasted_iota(jnp.int32, s.shape, axis).astype(F32)
    s = jnp.where(idx < n_valid, s, NEG_INF)
    sel = jnp.zeros(s.shape, F32)
    for _ in range(MOBA_TOPK):
        m = jnp.max(s, axis=axis, keepdims=True)
        first = jnp.min(jnp.where(s == m, idx, n), axis=axis, keepdims=True)
        pick = jnp.logical_and(idx == first, m > NEG_INF)
        sel = jnp.where(pick, 1.0, sel)
        s = jnp.where(pick, NEG_INF, s)
    return sel


def _moba_pen_kernel(tab_ref, q_ref, km_ref, pen_ref, penfar_ref):
    h = pl.program_id(0)
    nblk = km_ref.shape[0]
    tq = q_ref.shape[0]
    scores = _dot_nt(km_ref[...].astype(BF16), (q_ref[...] * ATT_SCALE).astype(BF16))
    pos = pl.program_id(1) * tq + lax.broadcasted_iota(jnp.int32, (1, tq), 1)
    own = pos // MOBA_BLOCK
    blk_id = lax.broadcasted_iota(jnp.int32, scores.shape, 0)
    chosen = _top_blocks(scores, own, axis=0) > 0.5
    pen_ref[0:nblk, :] = jnp.where(jnp.logical_or(chosen, blk_id == own), 0.0, NEG_INF)
    pen_ref[nblk:, :] = jnp.full((pen_ref.shape[0] - nblk, tq), NEG_INF, F32)
    far_bias = tab_ref[N_BUCKETS - 1, h] * LOG2E
    penfar_ref[...] = jnp.where(jnp.logical_and(chosen, blk_id <= own - MOBA_GROUP), far_bias, NEG_INF)


def _moba_pen(proj, kmean, rel_bias, tq=1024):
    t = proj.shape[0]
    nblk = t // MOBA_BLOCK
    return pl.pallas_call(
        _moba_pen_kernel,
        grid=(H_ATT, t // tq),
        in_specs=[pl.BlockSpec(memory_space=pltpu.SMEM),
                  pl.BlockSpec((tq, ATT_DH), lambda h, n: (n, COL_AQ + h)),
                  pl.BlockSpec((nblk, ATT_DH), lambda h, n: (0, h))],
        out_specs=[pl.BlockSpec((None, nblk + 8, tq), lambda h, n: (h, 0, n)),
                   pl.BlockSpec((None, nblk, tq), lambda h, n: (h, 0, n))],
        out_shape=[jax.ShapeDtypeStruct((H_ATT, nblk + 8, t), F32),
                   jax.ShapeDtypeStruct((H_ATT, nblk, t), F32)],
        compiler_params=_cparams("parallel", "parallel"),
        name="moba_block_masks",
    )(rel_bias, proj, kmean)


def _stream_page_sums(pt_ref, pages_hbm, sums_ref, pbuf, psem, step, n_steps, pages_per_block):
    n_pg = pbuf.shape[1]
    total = pt_ref.shape[0]
    slot = step % 2

    def copies(st, sl):
        return [pltpu.make_async_copy(pages_hbm.at[pt_ref[jnp.minimum(st * n_pg + p, total - 1)]],
                                      pbuf.at[sl, p], psem.at[sl, p]) for p in range(n_pg)]

    @pl.when(step == 0)
    def _():
        for cp in copies(step, slot):
            cp.start()

    @pl.when(step < n_steps)
    def _():
        for cp in copies(step, slot):
            cp.wait()

    @pl.when(step + 1 < n_steps)
    def _():
        for cp in copies(step + 1, 1 - slot):
            cp.start()

    def reduce_pages():
        for blk in range(n_pg // pages_per_block):
            tot = pbuf[slot, blk * pages_per_block].sum(axis=0)
            for pg in range(1, pages_per_block):
                tot = tot + pbuf[slot, blk * pages_per_block + pg].sum(axis=0)
            sums_ref[blk] = tot

    return reduce_pages


def _moba_seq_kernel(*refs, n_far, stream):
    if stream:
        n_stream_steps, pages_per_block = stream
        (pt_ref, q_ref, k_ref, v_ref, bias_ref, pen_ref, penfar_ref, pages_hbm, o_ref, sums_ref,
         kb_s, vt_s, acc_s, s_s, pbuf, psem) = refs
        reduce_pages = _stream_page_sums(
            pt_ref, pages_hbm, sums_ref, pbuf, psem,
            pl.program_id(0) * pl.num_programs(1) + pl.program_id(1), n_stream_steps, pages_per_block)
    else:
        q_ref, k_ref, v_ref, bias_ref, pen_ref, penfar_ref, o_ref, kb_s, vt_s, acc_s, s_s = refs
        reduce_pages = None
    i = pl.program_id(1)
    nblk = vt_s.shape[0]

    @pl.when(i == 0)
    def _():
        kb_s[...] = k_ref[...].astype(BF16)

        def xpose(j, carry):
            r0 = pl.multiple_of(j * MOBA_BLOCK, MOBA_BLOCK)
            vt_s[j] = v_ref[pl.ds(r0, MOBA_BLOCK), :].T.astype(BF16)
            return carry

        lax.fori_loop(0, nblk, xpose, 0)

    if reduce_pages is not None:
        reduce_pages()

    tq = q_ref.shape[0]
    grp = MOBA_GROUP
    qb = (q_ref[...] * (ATT_SCALE * LOG2E)).astype(BF16)
    acc_s[...] = jnp.zeros_like(acc_s)

    def logits(j):
        c0 = pl.multiple_of(j * MOBA_BLOCK, MOBA_BLOCK)
        return _dot_nt(kb_s[pl.ds(c0, MOBA_BLOCK), :], qb)

    def keep_max(smax, s):
        cur = jnp.max(s, axis=0, keepdims=True)
        return cur if smax is None else jnp.maximum(smax, cur)

    smax = None
    for u in range(grp):
        j = jnp.maximum(i - u, 0)
        pen_row = jnp.where(u <= i, j, nblk)
        s = logits(j) + bias_ref[min(u, n_far)] + pen_ref[pl.ds(pen_row, 1), :]
        s_s[0, u] = s
        smax = keep_max(smax, s)

    def softmax_update(slot, m_old, l_old, smax, block_of):
        mg = jnp.maximum(m_old, smax)
        alpha = jnp.exp2(m_old - mg)
        l_new = alpha * l_old
        pv = None
        for u in range(grp):
            p = jnp.exp2(s_s[slot, u] - mg)
            l_new = l_new + jnp.sum(p, axis=0, keepdims=True)
            d = _dot(vt_s[block_of(u)], p.astype(BF16))
            pv = d if pv is None else pv + d
        acc_s[...] = alpha * acc_s[...] + pv
        return mg, l_new

    def trip_with_slots(t, carry, slot_prev):
        m_old, l_old, smax_prev = carry
        smax_new = None
        for u in range(grp):
            s = logits(t * grp + u) + penfar_ref[pl.ds(t * grp + u, 1), :]
            s_s[1 - slot_prev, u] = s
            smax_new = keep_max(smax_new, s)
        prev = lambda u: jnp.where(t == 0, jnp.maximum(i - u, 0), (t - 1) * grp + u)
        m_new, l_new = softmax_update(slot_prev, m_old, l_old, smax_prev, prev)
        return m_new, l_new, smax_new

    def trip(t, carry):
        return lax.cond(t % 2 == 0, lambda c: trip_with_slots(t, c, 0), lambda c: trip_with_slots(t, c, 1), carry)

    n_far_groups = i // grp
    init = (jnp.full((1, tq), NEG_INF, F32), jnp.zeros((1, tq), F32), smax)
    m, l, smax = lax.fori_loop(0, n_far_groups, trip, init)
    last = lambda u: jnp.where(n_far_groups == 0, jnp.maximum(i - u, 0), (n_far_groups - 1) * grp + u)
    m, l = softmax_update(n_far_groups % 2, m, l, smax, last)
    o_ref[...] = (acc_s[...] / l).T.astype(o_ref.dtype)


def _moba_seq(proj, keys, values, rel_bias, side_pool=None, side_page_table=None):
    t = proj.shape[0]
    nq = t // MOBA_BLOCK
    kmean = _block_mean(keys)
    n_far = -(-(BUCKET_THRESHOLDS[-1] + MOBA_BLOCK) // MOBA_BLOCK)
    assert MOBA_GROUP >= n_far and nq % MOBA_GROUP == 0
    bias = _bias_tiles(rel_bias, n_far + 1, MOBA_BLOCK, MOBA_BLOCK, 0, MOBA_BLOCK, -1, 1, scale=LOG2E)
    pen, penfar = _moba_pen(proj, kmean, rel_bias)
    n_steps = H_ATT * nq
    in_specs = [pl.BlockSpec((MOBA_BLOCK, ATT_DH), lambda h, i, *_: (i, COL_AQ + h)),
                pl.BlockSpec((t, ATT_DH), lambda h, i, *_: (0, h)),
                pl.BlockSpec((t, ATT_DH), lambda h, i, *_: (0, h)),
                pl.BlockSpec((None, n_far + 1, MOBA_BLOCK, MOBA_BLOCK), lambda h, i, *_: (h, 0, 0, 0)),
                pl.BlockSpec((None, nq + 8, MOBA_BLOCK), lambda h, i, *_: (h, 0, i)),
                pl.BlockSpec((None, nq, MOBA_BLOCK), lambda h, i, *_: (h, 0, i))]
    out_specs = [pl.BlockSpec((MOBA_BLOCK, ATT_DH), lambda h, i, *_: (i, h))]
    out_shape = [jax.ShapeDtypeStruct((t, ATT_W), BF16)]
    scratch = [pltpu.VMEM((t, ATT_DH), BF16), pltpu.VMEM((nq, ATT_DH, MOBA_BLOCK), BF16),
               pltpu.VMEM((ATT_DH, MOBA_BLOCK), F32),
               pltpu.VMEM((2, MOBA_GROUP, MOBA_BLOCK, MOBA_BLOCK), F32)]
    operands = [proj, keys, values, bias, pen, penfar]
    stream = None
    prefetch = []
    if side_pool is not None:
        pt = side_page_table.reshape(-1)
        page_rows = side_pool.shape[1]
        ppb = MOBA_BLOCK // page_rows
        assert ppb * page_rows == MOBA_BLOCK and pt.shape[0] % ppb == 0
        n_pg = ppb * -(-pt.shape[0] // (ppb * n_steps))
        stream = (-(-pt.shape[0] // n_pg), ppb)
        in_specs.append(pl.BlockSpec(memory_space=pl.ANY))
        out_specs.append(pl.BlockSpec((None, n_pg // ppb, H_ATT, ATT_DH), lambda h, i, *_: (h * nq + i, 0, 0, 0)))
        out_shape.append(jax.ShapeDtypeStruct((n_steps, n_pg // ppb, H_ATT, ATT_DH), F32))
        scratch += [pltpu.VMEM((2, n_pg) + side_pool.shape[1:], F32), pltpu.SemaphoreType.DMA((2, n_pg))]
        operands.append(side_pool)
        prefetch = [pt]
    outs = pl.pallas_call(
        functools.partial(_moba_seq_kernel, n_far=n_far, stream=stream),
        grid_spec=pltpu.PrefetchScalarGridSpec(
            num_scalar_prefetch=len(prefetch), grid=(H_ATT, nq), in_specs=in_specs, out_specs=out_specs,
            scratch_shapes=scratch),
        out_shape=out_shape,
        compiler_params=_cparams("arbitrary", "arbitrary"),
        name="moba_seq",
    )(*prefetch, *operands)
    if side_pool is None:
        return outs[0], None
    return outs[0], outs[1].reshape(-1, H_ATT, ATT_DH)[:pt.shape[0] // ppb]


FFN_CHUNK = 256
MOBA_STEP_SLOTS = 4


def _moba_select_kernel(ksum_ref, q_ref, sel_ref):
    kmean = ksum_ref[...] / MOBA_BLOCK
    prod = kmean.astype(BF16).astype(F32) * (q_ref[...] * ATT_SCALE).astype(BF16).astype(F32)[None]
    s = jnp.sum(prod, axis=-1, keepdims=True)
    nb = s.shape[0]
    mask = _top_blocks(s, nb, axis=0)
    idx = lax.broadcasted_iota(jnp.int32, s.shape, 0).astype(F32)
    for r in range(MOBA_TOPK):
        first = jnp.min(jnp.where(mask > 0.5, idx, float(nb)), axis=0)
        sel_ref[r] = first.astype(jnp.int32)
        mask = jnp.where(idx == first[None], 0.0, mask)


def _moba_select(block_sums, q_heads):
    nb_seq, n_blocks = block_sums.shape[:2]
    assert n_blocks >= MOBA_TOPK
    return pl.pallas_call(
        _moba_select_kernel,
        grid=(nb_seq,),
        in_specs=[pl.BlockSpec((None, n_blocks, H_ATT, ATT_DH), lambda b: (b, 0, 0, 0)),
                  pl.BlockSpec((None, H_ATT, ATT_DH), lambda b: (b, 0, 0))],
        out_specs=pl.BlockSpec((None, MOBA_TOPK, H_ATT, 1), lambda b: (b, 0, 0, 0)),
        out_shape=jax.ShapeDtypeStruct((nb_seq, MOBA_TOPK, H_ATT, 1), jnp.int32),
        compiler_params=_cparams("parallel"),
        name="moba_select",
    )(block_sums, q_heads)


def _moba_step_kernel(pt_ref, sel_ref, q_ref, kn_ref, vn_ref, b0_ref, b1_ref, b2_ref, bown_ref, ck_hbm, cv_hbm,
                      o_ref, kbuf, vbuf, sem, *, n_pages):
    n_pg = MOBA_TOPK * 2
    b_refs = (b0_ref, b1_ref, b2_ref)
    n_slots = kbuf.shape[0]
    page_rows = kbuf.shape[2]
    b = pl.program_id(0)
    h = pl.program_id(1)
    n_heads = pl.num_programs(1)
    n_steps = pl.num_programs(0) * n_heads
    step = b * n_heads + h
    slot = step % n_slots

    def page_copies(bb, hh, sl):
        cps = []
        for r in range(MOBA_TOPK):
            blk = sel_ref[(bb * MOBA_TOPK + r) * n_heads + hh]
            for half in range(2):
                page = pt_ref[bb * n_pages + 2 * blk + half]
                idx = 2 * r + half
                cps.append(pltpu.make_async_copy(ck_hbm.at[page, :, hh, :], kbuf.at[sl, idx], sem.at[sl, idx]))
                cps.append(pltpu.make_async_copy(cv_hbm.at[page, :, hh, :], vbuf.at[sl, idx], sem.at[sl, n_pg + idx]))
        return cps

    def start_step(st):
        for cp in page_copies(st // n_heads, st % n_heads, st % n_slots):
            cp.start()

    @pl.when(step == 0)
    def _():
        for ahead in range(n_slots - 1):
            @pl.when(ahead < n_steps)
            def _():
                start_step(jnp.int32(ahead))

    @pl.when(step + n_slots - 1 < n_steps)
    def _():
        start_step(step + n_slots - 1)

    for cp in page_copies(b, h, slot):
        cp.wait()

    qs = q_ref[...] * ATT_SCALE
    eye = (lax.broadcasted_iota(jnp.int32, (page_rows, page_rows), 0)
           == lax.broadcasted_iota(jnp.int32, (page_rows, page_rows), 1))
    logits = []
    for r in range(MOBA_TOPK):
        bias = b_refs[r][...]
        for half in range(2):
            lg = jnp.sum(kbuf[slot, 2 * r + half] * qs, axis=-1, keepdims=True)
            brow = bias[:, half * page_rows:(half + 1) * page_rows]
            bcol = jnp.sum(jnp.where(eye, brow, 0.0), axis=-1, keepdims=True)
            logits.append(lg + bcol)
    l_own = jnp.sum(qs * kn_ref[...], axis=-1, keepdims=True) + bown_ref[:, 0:1]
    m = l_own
    for lg in logits:
        m = jnp.maximum(m, jnp.max(lg, axis=0, keepdims=True))
    p_own = jnp.exp(l_own - m)
    den = p_own
    acc = p_own * vn_ref[...]
    for idx, lg in enumerate(logits):
        p = jnp.exp(lg - m)
        den = den + jnp.sum(p, axis=0, keepdims=True)
        acc = acc + jnp.sum(p * vbuf[slot, idx], axis=0, keepdims=True)
    o_ref[...] = (acc / den).astype(o_ref.dtype)


def _moba_step(proj, ak, av, block_sums, cache_k4, cache_v4, page_table, rel_bias):
    nb_seq, n_pages = page_table.shape
    page_rows = cache_k4.shape[1]
    ppb = MOBA_BLOCK // page_rows
    assert ppb == 2
    n_blocks = n_pages // ppb
    n_past = n_pages * page_rows
    head_rows = lambda a: a.reshape(nb_seq, H_ATT, 1, ATT_DH)
    aq = proj[:, COL_AQ * 128:COL_AQ * 128 + ATT_W]
    sel = _moba_select(block_sums.reshape(nb_seq, n_blocks, H_ATT, ATT_DH), aq.reshape(nb_seq, H_ATT, ATT_DH))
    bias = _bias_tiles(rel_bias, n_blocks + 1, 1, MOBA_BLOCK, n_past, -MOBA_BLOCK, 0, -1)

    def bias_spec(r):
        return pl.BlockSpec((None, None, 1, MOBA_BLOCK),
                            lambda b, h, pt, sl: (h, sl[(b * MOBA_TOPK + r) * H_ATT + h], 0, 0))

    row_spec = pl.BlockSpec((None, None, 1, ATT_DH), lambda b, h, pt, sl: (b, h, 0, 0))
    hbm = pl.BlockSpec(memory_space=pl.ANY)
    n_pg = 2 * MOBA_TOPK
    out = pl.pallas_call(
        functools.partial(_moba_step_kernel, n_pages=n_pages),
        grid_spec=pltpu.PrefetchScalarGridSpec(
            num_scalar_prefetch=2,
            grid=(nb_seq, H_ATT),
            in_specs=[row_spec, row_spec, row_spec] + [bias_spec(r) for r in range(MOBA_TOPK)]
                     + [pl.BlockSpec((None, None, 1, MOBA_BLOCK), lambda b, h, pt, sl: (h, n_blocks, 0, 0)),
                        hbm, hbm],
            out_specs=row_spec,
            scratch_shapes=[pltpu.VMEM((MOBA_STEP_SLOTS, n_pg, page_rows, ATT_DH), F32),
                            pltpu.VMEM((MOBA_STEP_SLOTS, n_pg, page_rows, ATT_DH), F32),
                            pltpu.SemaphoreType.DMA((MOBA_STEP_SLOTS, 2 * n_pg))]),
        out_shape=jax.ShapeDtypeStruct((nb_seq, H_ATT, 1, ATT_DH), BF16),
        compiler_params=_cparams("arbitrary", "arbitrary"),
        name="moba_step",
    )(page_table.reshape(-1), sel.reshape(-1), head_rows(aq), head_rows(ak), head_rows(av),
      bias, bias, bias, bias, cache_k4, cache_v4)
    return out.reshape(nb_seq, ATT_W)


def _merge_out_kernel(oh_ref, oa_ref, gh0_ref, gh1_ref, ga0_ref, ga1_ref, wh_ref, wa_ref, wo_ref, x_ref, g_ref,
                      gnext_ref, o_ref, h_ref):
    half = D_MODEL // 2
    br_h = _dot(oh_ref[...], wh_ref[...])
    br_a = _dot(oa_ref[...], wa_ref[...])
    merged = jnp.concatenate(
        [(jax.nn.sigmoid(gh_ref[...]) * br_h[:, c * half:(c + 1) * half]
          + jax.nn.sigmoid(ga_ref[...]) * br_a[:, c * half:(c + 1) * half]).astype(BF16)
         for c, (gh_ref, ga_ref) in enumerate(((gh0_ref, ga0_ref), (gh1_ref, ga1_ref)))], axis=1)
    z = _dot(merged, wo_ref[...])
    ms = jnp.mean(z * z, axis=-1, keepdims=True)
    x1 = x_ref[...] + z * lax.rsqrt(ms + EPS) * g_ref[...]
    o_ref[...] = x1
    ms1 = jnp.mean(x1 * x1, axis=-1, keepdims=True)
    h_ref[...] = (x1 * lax.rsqrt(ms1 + EPS) * gnext_ref[...]).astype(h_ref.dtype)


def _merge_out(o_hg, o_att, proj, w_bh, w_bm, w_out, x, gain, gain_next, tm):
    m = x.shape[0]
    half = D_MODEL // 2
    assert half == 1024
    resident = pl.Buffered(1)
    vec = pl.BlockSpec((1, D_MODEL), lambda i: (0, 0), pipeline_mode=resident)
    row = pl.BlockSpec((tm, D_MODEL), lambda i: (i, 0))
    gate = lambda blk: pl.BlockSpec((tm, half), lambda i, blk=blk: (i, blk))
    return pl.pallas_call(
        _merge_out_kernel,
        grid=(m // tm,),
        in_specs=[pl.BlockSpec((tm, HG_VW), lambda i: (i, 0)), pl.BlockSpec((tm, ATT_W), lambda i: (i, 0)),
                  gate(COLK_GHG), gate(COLK_GHG + 1), gate(COLK_GATT), gate(COLK_GATT + 1),
                  pl.BlockSpec((HG_VW, D_MODEL), lambda i: (0, 0), pipeline_mode=resident),
                  pl.BlockSpec((ATT_W, D_MODEL), lambda i: (0, 0), pipeline_mode=resident),
                  pl.BlockSpec((D_MODEL, D_MODEL), lambda i: (0, 0), pipeline_mode=resident),
                  row, vec, vec],
        out_specs=[row, row],
        out_shape=[jax.ShapeDtypeStruct((m, D_MODEL), F32), jax.ShapeDtypeStruct((m, D_MODEL), BF16)],
        compiler_params=_cparams("parallel"),
        name="merge_out_proj_residual",
    )(o_hg, o_att, proj, proj, proj, proj, w_bh, w_bm, w_out, x, gain.reshape(1, D_MODEL),
      gain_next.reshape(1, D_MODEL))


def _ffn_up_kernel(h_ref, wg_ref, wu_ref, cw_ref, cb_ref, p0_ref, p1_ref, a_ref, g_out_ref, carry_s, *, seq):
    i = pl.program_id(0)
    j = pl.program_id(1)
    tm, tf = a_ref.shape
    if seq:
        @pl.when(i == 0)
        def _():
            carry_s[j, 0:1, :] = p0_ref[...]
            carry_s[j, 1:2, :] = p1_ref[...]

    top = lax.broadcasted_iota(jnp.int32, (8, FFN_CHUNK), 0)
    for c in range(tf // FFN_CHUNK):
        cols = slice(c * FFN_CHUNK, (c + 1) * FFN_CHUNK)
        g = _dot(h_ref[...], wg_ref[:, cols].astype(BF16))
        u = _dot(h_ref[...], wu_ref[:, cols].astype(BF16))
        if seq:
            c0 = carry_s[j, 0:1, cols]
            c1 = carry_s[j, 1:2, cols]
            r1 = pltpu.roll(g, 1, axis=0)
            r2 = pltpu.roll(g, 2, axis=0)
            prev1 = jnp.concatenate([jnp.where(top == 0, c1, r1[0:8]), r1[8:]], axis=0)
            prev2 = jnp.concatenate([jnp.where(top == 0, c0, jnp.where(top == 1, c1, r2[0:8])), r2[8:]], axis=0)
            carry_s[j, 0:2, cols] = g[tm - 2:tm, :]
            g_out_ref[:, cols] = g[tm - 2:tm, :]
        else:
            prev2 = p0_ref[:, cols]
            prev1 = p1_ref[:, cols]
            g_out_ref[:, cols] = g
        gc = cb_ref[:, cols] + cw_ref[0:1, cols] * prev2 + cw_ref[1:2, cols] * prev1 + cw_ref[2:3, cols] * g
        a_ref[:, cols] = (jax.nn.gelu(gc, approximate=True) * u).astype(a_ref.dtype)


def _ffn_down_kernel(a_ref, wd_ref, x_ref, g_ref, o_ref):
    y = _dot(a_ref[...], wd_ref[...])
    ms = jnp.mean(y * y, axis=-1, keepdims=True)
    o_ref[...] = x_ref[...] + y * lax.rsqrt(ms + EPS) * g_ref[...]


def _ffn(x, h, w_up, conv_w, conv_b, prev0, prev1, w_down, gpost, *, seq, tm_up, tm_down, tf=512):
    m = x.shape[0]
    nf = D_FF // tf
    assert CONV_W == 3 and nf * tf == D_FF
    prow = 1 if seq else tm_up
    pspec = pl.BlockSpec((prow, tf), (lambda i, j: (0, j)) if seq else (lambda i, j: (i, j)))
    if seq:
        gspec = pl.BlockSpec((None, CONV_W - 1, tf), lambda i, j: (i, 0, j))
        gshape = (m // tm_up, CONV_W - 1, D_FF)
    else:
        gspec = pl.BlockSpec((tm_up, tf), lambda i, j: (i, j))
        gshape = (m, D_FF)
    act, g_out = pl.pallas_call(
        functools.partial(_ffn_up_kernel, seq=seq),
        grid=(m // tm_up, nf),
        in_specs=[pl.BlockSpec((tm_up, D_MODEL), lambda i, j: (i, 0)),
                  pl.BlockSpec((D_MODEL, tf), lambda i, j: (0, j)),
                  pl.BlockSpec((D_MODEL, tf), lambda i, j: (0, nf + j)),
                  pl.BlockSpec((CONV_W, tf), lambda i, j: (0, j)),
                  pl.BlockSpec((1, tf), lambda i, j: (0, j)),
                  pspec, pspec],
        out_specs=[pl.BlockSpec((tm_up, tf), lambda i, j: (i, j)), gspec],
        out_shape=[jax.ShapeDtypeStruct((m, D_FF), BF16), jax.ShapeDtypeStruct(gshape, F32)],
        scratch_shapes=[pltpu.VMEM((nf, 8, tf), F32)],
        compiler_params=_cparams("arbitrary", "arbitrary"),
        name="conv_ffn_up",
    )(h, w_up, w_up, conv_w, conv_b.reshape(1, D_FF), prev0, prev1)
    resident = pl.Buffered(1)
    y = pl.pallas_call(
        _ffn_down_kernel,
        grid=(m // tm_down,),
        in_specs=[pl.BlockSpec((tm_down, D_FF), lambda i: (i, 0)),
                  pl.BlockSpec((D_FF, D_MODEL), lambda i: (0, 0), pipeline_mode=resident),
                  pl.BlockSpec((tm_down, D_MODEL), lambda i: (i, 0)),
                  pl.BlockSpec((1, D_MODEL), lambda i: (0, 0), pipeline_mode=resident)],
        out_specs=pl.BlockSpec((tm_down, D_MODEL), lambda i: (i, 0)),
        out_shape=jax.ShapeDtypeStruct((m, D_MODEL), F32),
        compiler_params=_cparams("parallel"),
        name="ffn_down_residual",
    )(act, w_down, x, gpost.reshape(1, D_MODEL))
    return y, (g_out[-1] if seq else g_out)


def kernel(x_prompt, x_sample, cache_k, cache_v, state_hgrn, state_ffn_conv, page_table, rel_bias, hg_lb,
           norm_mix_pre, norm_mix_post, norm_ffn_pre, norm_ffn_post, w_in, hg_out_norm, w_branch_hgrn,
           w_branch_moba, w_out, w_ffn_up, ffn_conv_w, ffn_conv_b, w_ffn_down):
    nb, t, _ = x_prompt.shape
    db = x_sample.shape[0]
    depth = w_in.shape[0]
    assert nb == 1 and depth == 1 and x_sample.shape[1] == 1
    l = 0
    w_in_b = w_in[l].astype(BF16)
    w_bh = w_branch_hgrn[l].astype(BF16)
    w_bm = w_branch_moba[l].astype(BF16)
    w_o = w_out[l].astype(BF16)
    w_up = w_ffn_up[l]
    w_dn = w_ffn_down[l].astype(BF16)

    n_pool, page_rows = cache_k.shape[1], cache_k.shape[2]
    ck = cache_k.reshape(depth * n_pool, page_rows, H_ATT, ATT_DH)
    cv = cache_v.reshape(depth * n_pool, page_rows, H_ATT, ATT_DH)

    xp = x_prompt.reshape(t, D_MODEL)
    proj, k_new, v_new = _norm_matmul(xp, norm_mix_pre[l], w_in_b, tm=1024)
    o_hg, s_new = _hgrn_seq(proj, hg_lb, hg_out_norm[l])
    o_att, cache_block_sums = _moba_seq(proj, k_new, v_new, rel_bias, side_pool=ck, side_page_table=page_table)
    x1, h2 = _merge_out(o_hg, o_att, proj, w_bh, w_bm, w_o, xp, norm_mix_post[l], norm_ffn_pre[l], tm=256)
    zero_row = jnp.zeros((1, D_FF), F32)
    yp, conv_p = _ffn(x1, h2, w_up, ffn_conv_w[l], ffn_conv_b[l], zero_row, zero_row, w_dn,
                      norm_ffn_post[l], seq=True, tm_up=1024, tm_down=256)
    kp = k_new.reshape(1, 1, t, H_ATT, ATT_DH)
    vp = v_new.reshape(1, 1, t, H_ATT, ATT_DH)

    xs = x_sample.reshape(db, D_MODEL)
    projs, k_new_s, v_new_s = _norm_matmul(xs, norm_mix_pre[l], w_in_b, tm=db)
    o_hg_s, s_new_s = _hgrn_step(projs, state_hgrn[l], hg_lb, hg_out_norm[l])
    o_att_s = _moba_step(projs, k_new_s, v_new_s, cache_block_sums, ck, cv, page_table, rel_bias)
    x1s, h2s = _merge_out(o_hg_s.reshape(db, HG_VW), o_att_s, projs, w_bh, w_bm, w_o, xs, norm_mix_post[l],
                          norm_ffn_pre[l], tm=db)
    buf = state_ffn_conv[l]
    ys, g_s = _ffn(x1s, h2s, w_up, ffn_conv_w[l], ffn_conv_b[l], buf[:, 0], buf[:, 1], w_dn,
                   norm_ffn_post[l], seq=False, tm_up=db, tm_down=db)
    ks = k_new_s.reshape(1, db, 1, H_ATT, ATT_DH)
    vs = v_new_s.reshape(1, db, 1, H_ATT, ATT_DH)
    conv_s = jnp.stack([buf[:, 1], g_s], axis=1)

    return (yp.reshape(1, t, D_MODEL), ys.reshape(db, 1, D_MODEL), kp, vp,
            s_new.reshape(1, 1, H_HG, HG_DK, HG_DV), conv_p.reshape(1, 1, CONV_W - 1, D_FF),
            ks, vs, s_new_s.reshape(1, db, H_HG, HG_DK, HG_DV), conv_s.reshape(1, db, CONV_W - 1, D_FF))
```

```python
import functools
import math

import numpy as np
import jax
import jax.numpy as jnp
from jax import lax
from jax.experimental import pallas as pl
from jax.experimental.pallas import tpu as pltpu

F32 = jnp.float32
BF16 = jnp.bfloat16

D_MODEL = 2048
H_HG = 8
HG_DK = 128
HG_DV = 128
HG_W = H_HG * HG_DK
HG_VW = H_HG * HG_DV
H_ATT = 8
ATT_DH = 128
ATT_W = H_ATT * ATT_DH
MOBA_BLOCK = 256
MOBA_TOPK = 3
N_BUCKETS = 32
MAX_EXACT = N_BUCKETS // 2
REL_MAX_DIST = 1024
D_FF = 5632
CONV_W = 3
EPS = 1e-6
N_IN = 2 * HG_W + 2 * HG_VW + 3 * ATT_W + 2 * D_MODEL
ATT_SCALE = ATT_DH ** -0.5
LOG2E = math.log2(math.e)

COL_HQ, COL_HF, COL_HI, COL_OG = 0, 8, 16, 24
COL_AQ = 32
COLK_AK, COLK_AV = 5, 6
COLK_GHG, COLK_GATT = 5, 7

HG_CHUNK = 128
HG_SUB = 8
HG_HEADS_PER_STEP = 8
MOBA_GROUP = 8
VMEM_LIMIT = 56 * 1024 * 1024
NEG_INF = float("-inf")


def _bucket_thresholds():
    n = np.arange(MAX_EXACT, 4 * REL_MAX_DIST, dtype=np.float64)
    large = MAX_EXACT + (np.log(n / MAX_EXACT) / math.log(REL_MAX_DIST / MAX_EXACT)
                         * (N_BUCKETS - MAX_EXACT)).astype(np.int64)
    large = np.minimum(large, N_BUCKETS - 1)
    return [int(n[np.argmax(large >= b)]) for b in range(MAX_EXACT + 1, N_BUCKETS)]


BUCKET_THRESHOLDS = _bucket_thresholds()


def _cparams(*sem):
    return pltpu.CompilerParams(dimension_semantics=sem, vmem_limit_bytes=VMEM_LIMIT)


def _silu(x):
    return x * jax.nn.sigmoid(x)


def _dot(a, b):
    return jnp.dot(a, b, preferred_element_type=F32)


def _dot_nt(a, b, precision=None):
    return lax.dot_general(a, b, (((1,), (1,)), ((), ())), precision=precision,
                           preferred_element_type=F32)


def _norm_matmul_kernel(x_ref, g_ref, w_ref, o_ref, k_ref, v_ref, h_ref):
    j = pl.program_id(1)

    @pl.when(j == 0)
    def _():
        x = x_ref[...]
        ms = jnp.mean(x * x, axis=-1, keepdims=True)
        h_ref[...] = (x * lax.rsqrt(ms + EPS) * g_ref[...]).astype(BF16)

    @pl.when(j == COLK_AK)
    def _():
        k_ref[...] = _dot(h_ref[...], w_ref[...])

    @pl.when(j == COLK_AV)
    def _():
        v_ref[...] = _dot(h_ref[...], w_ref[...])

    @pl.when(jnp.logical_and(j != COLK_AK, j != COLK_AV))
    def _():
        o_ref[...] = _dot(h_ref[...], w_ref[...])


def _norm_matmul(x, gain, w_bf16, tm):
    m, k = x.shape
    n = w_bf16.shape[1]
    tn = ATT_W
    assert COLK_AV == COLK_AK + 1
    return pl.pallas_call(
        _norm_matmul_kernel,
        grid=(m // tm, n // tn),
        in_specs=[pl.BlockSpec((tm, k), lambda i, j: (i, 0)),
                  pl.BlockSpec((1, k), lambda i, j: (0, 0)),
                  pl.BlockSpec((k, tn), lambda i, j: (0, j))],
        out_specs=[pl.BlockSpec((tm, tn), lambda i, j: (i, jnp.where(j < COLK_AK, j, jnp.maximum(j - 2, COLK_AK - 1)))),
                   pl.BlockSpec((tm, tn), lambda i, j: (i, 0)),
                   pl.BlockSpec((tm, tn), lambda i, j: (i, 0))],
        out_shape=[jax.ShapeDtypeStruct((m, n - 2 * tn), F32), jax.ShapeDtypeStruct((m, tn), F32),
                   jax.ShapeDtypeStruct((m, tn), F32)],
        scratch_shapes=[pltpu.VMEM((tm, k), BF16)],
        compiler_params=_cparams("parallel", "arbitrary"),
        name="norm_in_proj",
    )(x, gain.reshape(1, k), w_bf16)


def _forget_lower_bound(lbraw):
    e = jnp.exp(lbraw - jnp.max(lbraw, axis=0, keepdims=True))
    return e[0:1] / jnp.sum(e, axis=0, keepdims=True)


def _hgrn_seq_kernel(hq_ref, hf_ref, hi_ref, og_ref, lbraw_ref, gn_ref, o_ref, s_out_ref,
                     st_ref, q_s, k_s, b_s, v_s, acc_s, *, chunk, heads):
    c = pl.program_id(1)

    @pl.when(c == 0)
    def _():
        st_ref[...] = jnp.zeros_like(st_ref)

    row = lax.broadcasted_iota(jnp.int32, (chunk, HG_DK), 0)
    rr = lax.broadcasted_iota(jnp.int32, (chunk, chunk), 0)
    cc = lax.broadcasted_iota(jnp.int32, (chunk, chunk), 1)
    sub = lax.broadcasted_iota(jnp.int32, (HG_SUB, HG_DK), 0)

    for hh in range(heads):
        cols = slice(hh * 128, (hh + 1) * 128)
        lb = _forget_lower_bound(lbraw_ref[:, cols])
        q = _silu(hq_ref[:, cols])
        f = lb + (1.0 - lb) * jax.nn.sigmoid(hf_ref[:, cols])
        k = 1.0 - f
        v = hi_ref[:, cols]

        b = jnp.log2(f)
        shift = 1
        while shift < chunk:
            b = b + jnp.where(row >= shift, pltpu.roll(b, shift, axis=0), 0.0)
            shift *= 2

        q_s[hh] = q
        k_s[hh] = k
        b_s[hh] = b
        v_s[hh] = v

        st = st_ref[hh]
        o = _dot_nt((q * jnp.exp2(b)).astype(BF16), st.astype(BF16))

        a = jnp.zeros((chunk, chunk), F32)
        hs = HG_SUB
        while hs < chunk:
            blk = 2 * hs
            ref_rows = jnp.concatenate(
                [jnp.broadcast_to(b_s[hh, m0 + hs - 1:m0 + hs, :], (blk, HG_DK)) for m0 in range(0, chunk, blk)],
                axis=0)
            second = (row & (blk - 1)) >= hs
            qd = jnp.where(second, q * jnp.exp2(b - ref_rows), 0.0)
            kd = jnp.where(second, 0.0, k * jnp.exp2(ref_rows - b))
            a_l = _dot_nt(qd.astype(BF16), kd.astype(BF16))
            sh = int(math.log2(blk))
            a = a + jnp.where((rr >> sh) == (cc >> sh), a_l, 0.0)
            hs = blk
        o = o + _dot(a.astype(BF16), v.astype(BF16))

        for r0 in range(0, chunk, HG_SUB):
            qi = q_s[hh, r0:r0 + HG_SUB, :]
            bi = b_s[hh, r0:r0 + HG_SUB, :]
            acc = jnp.zeros((HG_SUB, HG_DV), F32)
            for s in range(HG_SUB):
                ks = k_s[hh, r0 + s:r0 + s + 1, :]
                bs = b_s[hh, r0 + s:r0 + s + 1, :]
                vs = v_s[hh, r0 + s:r0 + s + 1, :]
                e = jnp.where(sub >= s, jnp.exp2(bi - bs), 0.0)
                w = jnp.sum(qi * ks * e, axis=-1, keepdims=True)
                acc = acc + w * vs
            acc_s[hh, r0:r0 + HG_SUB, :] = acc
        o = o + acc_s[hh]

        b_last = b_s[hh, chunk - 1:chunk, :]
        kd = k * jnp.exp2(b_last - b)
        st_new = st * jnp.exp2(b_last) + _dot(v.T.astype(BF16), kd.astype(BF16))
        st_ref[hh] = st_new

        ms = jnp.mean(o * o, axis=-1, keepdims=True)
        o_ref[:, cols] = (o * lax.rsqrt(ms + EPS) * gn_ref[...] * _silu(og_ref[:, cols])).astype(o_ref.dtype)

    @pl.when(c == pl.num_programs(1) - 1)
    def _():
        for hh in range(heads):
            s_out_ref[hh] = st_ref[hh].T


def _hgrn_seq(proj, hg_lb, gn, chunk=HG_CHUNK, heads=HG_HEADS_PER_STEP):
    t = proj.shape[0]
    w = 128 * heads
    blk = lambda off: pl.BlockSpec((chunk, w), lambda h, c, off=off: (c, off // heads + h))
    return pl.pallas_call(
        functools.partial(_hgrn_seq_kernel, chunk=chunk, heads=heads),
        grid=(H_HG // heads, t // chunk),
        in_specs=[blk(COL_HQ), blk(COL_HF), blk(COL_HI), blk(COL_OG),
                  pl.BlockSpec((hg_lb.shape[0], w), lambda h, c: (0, h)),
                  pl.BlockSpec((1, HG_DV), lambda h, c: (0, 0))],
        out_specs=[pl.BlockSpec((chunk, w), lambda h, c: (c, h)),
                   pl.BlockSpec((heads, HG_DK, HG_DV), lambda h, c: (h, 0, 0))],
        out_shape=[jax.ShapeDtypeStruct((t, HG_VW), BF16),
                   jax.ShapeDtypeStruct((H_HG, HG_DK, HG_DV), F32)],
        scratch_shapes=[pltpu.VMEM((heads, HG_DV, HG_DK), F32)] + [pltpu.VMEM((heads, chunk, 128), F32)] * 5,
        compiler_params=_cparams("parallel", "arbitrary"),
        name="hgrn_seq",
    )(proj, proj, proj, proj, hg_lb, gn.reshape(1, HG_DV))


def _hgrn_step_kernel(hqc_ref, hfc_ref, hi_ref, og_ref, lbc_ref, gn_ref, s_ref, o_ref, s_out_ref):
    lbraw = lbc_ref[...]
    e = jnp.exp(lbraw - jnp.max(lbraw, axis=0, keepdims=True))
    lb = e[0] / jnp.sum(e, axis=0)
    q = _silu(hqc_ref[...])
    f = lb + (1.0 - lb) * jax.nn.sigmoid(hfc_ref[...])
    k = 1.0 - f
    v = hi_ref[...]
    s_new = f * s_ref[...] + k * v
    s_out_ref[...] = s_new
    o = jnp.sum(s_new * q, axis=-2, keepdims=True)
    ms = jnp.mean(o * o, axis=-1, keepdims=True)
    o_ref[...] = (o * lax.rsqrt(ms + EPS) * gn_ref[...] * _silu(og_ref[...])).astype(o_ref.dtype)


def _hgrn_step(proj, state, hg_lb, gn):
    nb = proj.shape[0]
    col = lambda a: a.reshape(nb, H_HG, HG_DK, 1)
    rowv = lambda a: a.reshape(nb, H_HG, 1, HG_DV)
    hq = col(proj[:, 0:HG_W])
    hf = col(proj[:, HG_W:2 * HG_W])
    hi = rowv(proj[:, 2 * HG_W:2 * HG_W + HG_VW])
    og = rowv(proj[:, 2 * HG_W + HG_VW:2 * HG_W + 2 * HG_VW])
    nl = hg_lb.shape[0]
    per = 4 if nb % 4 == 0 else 1
    cspec = pl.BlockSpec((per, H_HG, HG_DK, 1), lambda b: (b, 0, 0, 0))
    rspec = pl.BlockSpec((per, H_HG, 1, HG_DV), lambda b: (b, 0, 0, 0))
    sspec = pl.BlockSpec((per, H_HG, HG_DK, HG_DV), lambda b: (b, 0, 0, 0))
    return pl.pallas_call(
        _hgrn_step_kernel,
        grid=(nb // per,),
        in_specs=[cspec, cspec, rspec, rspec,
                  pl.BlockSpec((nl, H_HG, HG_DK, 1), lambda b: (0, 0, 0, 0)),
                  pl.BlockSpec((1, 1, HG_DV), lambda b: (0, 0, 0)),
                  sspec],
        out_specs=[rspec, sspec],
        out_shape=[jax.ShapeDtypeStruct((nb, H_HG, 1, HG_DV), BF16),
                   jax.ShapeDtypeStruct((nb, H_HG, HG_DK, HG_DV), F32)],
        compiler_params=_cparams("parallel"),
        name="hgrn_step",
    )(hq, hf, hi, og, hg_lb.reshape(nl, H_HG, HG_DK, 1), gn.reshape(1, 1, HG_DV), state)


BUCKET_STARTS = list(range(1, MAX_EXACT + 1)) + BUCKET_THRESHOLDS


def _bias_from_rel(rel, lo, hi, tab_ref, h):
    first = sum(1 for start in BUCKET_STARTS if start <= lo)
    out = jnp.full(rel.shape, tab_ref[first, h], F32)
    for bkt in range(first + 1, N_BUCKETS):
        if BUCKET_STARTS[bkt - 1] <= hi:
            out = jnp.where(rel >= BUCKET_STARTS[bkt - 1], tab_ref[bkt, h], out)
    return jnp.where(rel >= 0, out, NEG_INF) if lo < 0 else out


def _bias_tiles_kernel(tab_ref, o_ref, *, base, tile_step, row_step, col_step, scale):
    h = pl.program_id(0)
    n_tiles, rows, cols = o_ref.shape
    r = lax.broadcasted_iota(jnp.int32, (rows, cols), 0)
    c = lax.broadcasted_iota(jnp.int32, (rows, cols), 1)
    in_tile = r * row_step + c * col_step
    spans = [row_step * (rows - 1), col_step * (cols - 1)]
    for t in range(n_tiles):
        off = base + t * tile_step
        lo = off + sum(min(sp, 0) for sp in spans)
        hi = off + sum(max(sp, 0) for sp in spans)
        bias = _bias_from_rel(off + in_tile, lo, hi, tab_ref, h)
        o_ref[t] = bias if scale == 1.0 else bias * scale


def _bias_tiles(rel_bias, n_tiles, rows, cols, base, tile_step, row_step, col_step, scale=1.0):
    return pl.pallas_call(
        functools.partial(_bias_tiles_kernel, base=base, tile_step=tile_step, row_step=row_step,
                          col_step=col_step, scale=scale),
        grid=(H_ATT,),
        in_specs=[pl.BlockSpec(memory_space=pltpu.SMEM)],
        out_specs=pl.BlockSpec((None, n_tiles, rows, cols), lambda h: (h, 0, 0, 0)),
        out_shape=jax.ShapeDtypeStruct((H_ATT, n_tiles, rows, cols), F32),
        compiler_params=_cparams("parallel"),
        name="rel_bias_tiles",
    )(rel_bias)


def _block_mean_kernel(k_ref, o_ref):
    for blk in range(o_ref.shape[0]):
        o_ref[blk] = jnp.mean(k_ref[blk * MOBA_BLOCK:(blk + 1) * MOBA_BLOCK, :], axis=0, keepdims=True)


def _block_mean(keys, blocks_per_step=8):
    t = keys.shape[0]
    nb = t // MOBA_BLOCK
    assert nb % blocks_per_step == 0
    out = pl.pallas_call(
        _block_mean_kernel,
        grid=(nb // blocks_per_step,),
        in_specs=[pl.BlockSpec((blocks_per_step * MOBA_BLOCK, ATT_W), lambda n: (n, 0))],
        out_specs=pl.BlockSpec((blocks_per_step, 1, ATT_W), lambda n: (n, 0, 0)),
        out_shape=jax.ShapeDtypeStruct((nb, 1, ATT_W), F32),
        compiler_params=_cparams("parallel"),
        name="moba_block_mean",
    )(keys)
    return out.reshape(nb, ATT_W)


def _top_blocks(s, n_valid, axis):
    n = float(s.shape[axis])
    idx = lax.broadcasted_iota(jnp.int32, s.shape, axis).astype(F32)
    s = jnp.where(idx < n_valid, s, NEG_INF)
    sel = jnp.zeros(s.shape, F32)
    for _ in range(MOBA_TOPK):
        m = jnp.max(s, axis=axis, keepdims=True)
        first = jnp.min(jnp.where(s == m, idx, n), axis=axis, keepdims=True)
        pick = jnp.logical_and(idx == first, m > NEG_INF)
        sel = jnp.where(pick, 1.0, sel)
        s = jnp.where(pick, NEG_INF, s)
    return sel


def _moba_pen_kernel(tab_ref, q_ref, km_ref, pen_ref, penfar_ref):
    h = pl.program_id(0)
    nblk = km_ref.shape[0]
    tq = q_ref.shape[0]
    scores = _dot_nt(km_ref[...].astype(BF16), (q_ref[...] * ATT_SCALE).astype(BF16))
    pos = pl.program_id(1) * tq + lax.broadcasted_iota(jnp.int32, (1, tq), 1)
    own = pos // MOBA_BLOCK
    blk_id = lax.broadcasted_iota(jnp.int32, scores.shape, 0)
    chosen = _top_blocks(scores, own, axis=0) > 0.5
    pen_ref[0:nblk, :] = jnp.where(jnp.logical_or(chosen, blk_id == own), 0.0, NEG_INF)
    pen_ref[nblk:, :] = jnp.full((pen_ref.shape[0] - nblk, tq), NEG_INF, F32)
    far_bias = tab_ref[N_BUCKETS - 1, h] * LOG2E
    penfar_ref[...] = jnp.where(jnp.logical_and(chosen, blk_id <= own - MOBA_GROUP), far_bias, NEG_INF)


def _moba_pen(proj, kmean, rel_bias, tq=2048):
    t = proj.shape[0]
    nblk = t // MOBA_BLOCK
    return pl.pallas_call(
        _moba_pen_kernel,
        grid=(H_ATT, t // tq),
        in_specs=[pl.BlockSpec(memory_space=pltpu.SMEM),
                  pl.BlockSpec((tq, ATT_DH), lambda h, n: (n, COL_AQ + h)),
                  pl.BlockSpec((nblk, ATT_DH), lambda h, n: (0, h))],
        out_specs=[pl.BlockSpec((None, nblk + 8, tq), lambda h, n: (h, 0, n)),
                   pl.BlockSpec((None, nblk, tq), lambda h, n: (h, 0, n))],
        out_shape=[jax.ShapeDtypeStruct((H_ATT, nblk + 8, t), F32),
                   jax.ShapeDtypeStruct((H_ATT, nblk, t), F32)],
        compiler_params=_cparams("parallel", "parallel"),
        name="moba_block_masks",
    )(rel_bias, proj, kmean)


def _stream_page_sums(pt_ref, pages_hbm, sums_ref, pbuf, psem, step, n_steps, pages_per_block):
    n_pg = pbuf.shape[1]
    total = pt_ref.shape[0]
    slot = step % 2

    def copies(st, sl):
        return [pltpu.make_async_copy(pages_hbm.at[pt_ref[jnp.minimum(st * n_pg + p, total - 1)]],
                                      pbuf.at[sl, p], psem.at[sl, p]) for p in range(n_pg)]

    @pl.when(step == 0)
    def _():
        for cp in copies(step, slot):
            cp.start()

    @pl.when(step < n_steps)
    def _():
        for cp in copies(step, slot):
            cp.wait()

    @pl.when(step + 1 < n_steps)
    def _():
        for cp in copies(step + 1, 1 - slot):
            cp.start()

    def reduce_pages():
        for blk in range(n_pg // pages_per_block):
            tot = pbuf[slot, blk * pages_per_block].sum(axis=0)
            for pg in range(1, pages_per_block):
                tot = tot + pbuf[slot, blk * pages_per_block + pg].sum(axis=0)
            sums_ref[blk] = tot

    return reduce_pages


def _moba_seq_kernel(*refs, n_far, stream):
    if stream:
        n_stream_steps, pages_per_block = stream
        (pt_ref, q_ref, k_ref, v_ref, bias_ref, pen_ref, penfar_ref, pages_hbm, o_ref, sums_ref,
         kb_s, vt_s, acc_s, s_s, pbuf, psem) = refs
        reduce_pages = _stream_page_sums(
            pt_ref, pages_hbm, sums_ref, pbuf, psem,
            pl.program_id(0) * pl.num_programs(1) + pl.program_id(1), n_stream_steps, pages_per_block)
    else:
        q_ref, k_ref, v_ref, bias_ref, pen_ref, penfar_ref, o_ref, kb_s, vt_s, acc_s, s_s = refs
        reduce_pages = None
    i = pl.program_id(1)
    nblk = vt_s.shape[0]

    @pl.when(i == 0)
    def _():
        kb_s[...] = k_ref[...].astype(BF16)

        def xpose(j, carry):
            r0 = pl.multiple_of(j * MOBA_BLOCK, MOBA_BLOCK)
            vt_s[j] = v_ref[pl.ds(r0, MOBA_BLOCK), :].T.astype(BF16)
            return carry

        lax.fori_loop(0, nblk, xpose, 0)

    if reduce_pages is not None:
        reduce_pages()

    tq = q_ref.shape[0]
    grp = MOBA_GROUP
    qb = (q_ref[...] * (ATT_SCALE * LOG2E)).astype(BF16)
    acc_s[...] = jnp.zeros_like(acc_s)

    def logits(j):
        c0 = pl.multiple_of(j * MOBA_BLOCK, MOBA_BLOCK)
        return _dot_nt(kb_s[pl.ds(c0, MOBA_BLOCK), :], qb)

    def keep_max(smax, s):
        cur = jnp.max(s, axis=0, keepdims=True)
        return cur if smax is None else jnp.maximum(smax, cur)

    smax = None
    for u in range(grp):
        j = jnp.maximum(i - u, 0)
        pen_row = jnp.where(u <= i, j, nblk)
        s = logits(j) + bias_ref[min(u, n_far)] + pen_ref[pl.ds(pen_row, 1), :]
        s_s[0, u] = s
        smax = keep_max(smax, s)

    def softmax_update(slot, m_old, l_old, smax, block_of):
        mg = jnp.maximum(m_old, smax)
        alpha = jnp.exp2(m_old - mg)
        l_new = alpha * l_old
        pv = None
        for u in range(grp):
            p = jnp.exp2(s_s[slot, u] - mg)
            l_new = l_new + jnp.sum(p, axis=0, keepdims=True)
            d = _dot(vt_s[block_of(u)], p.astype(BF16))
            pv = d if pv is None else pv + d
        acc_s[...] = alpha * acc_s[...] + pv
        return mg, l_new

    def trip_with_slots(t, carry, slot_prev):
        m_old, l_old, smax_prev = carry
        smax_new = None
        for u in range(grp):
            s = logits(t * grp + u) + penfar_ref[pl.ds(t * grp + u, 1), :]
            s_s[1 - slot_prev, u] = s
            smax_new = keep_max(smax_new, s)
        prev = lambda u: jnp.where(t == 0, jnp.maximum(i - u, 0), (t - 1) * grp + u)
        m_new, l_new = softmax_update(slot_prev, m_old, l_old, smax_prev, prev)
        return m_new, l_new, smax_new

    def trip(t, carry):
        return lax.cond(t % 2 == 0, lambda c: trip_with_slots(t, c, 0), lambda c: trip_with_slots(t, c, 1), carry)

    n_far_groups = i // grp
    init = (jnp.full((1, tq), NEG_INF, F32), jnp.zeros((1, tq), F32), smax)
    m, l, smax = lax.fori_loop(0, n_far_groups, trip, init)
    last = lambda u: jnp.where(n_far_groups == 0, jnp.maximum(i - u, 0), (n_far_groups - 1) * grp + u)
    m, l = softmax_update(n_far_groups % 2, m, l, smax, last)
    o_ref[...] = (acc_s[...] / l).T.astype(o_ref.dtype)


def _moba_seq(proj, keys, values, rel_bias, side_pool=None, side_page_table=None):
    t = proj.shape[0]
    nq = t // MOBA_BLOCK
    kmean = _block_mean(keys)
    n_far = -(-(BUCKET_THRESHOLDS[-1] + MOBA_BLOCK) // MOBA_BLOCK)
    assert MOBA_GROUP >= n_far and nq % MOBA_GROUP == 0
    bias = _bias_tiles(rel_bias, n_far + 1, MOBA_BLOCK, MOBA_BLOCK, 0, MOBA_BLOCK, -1, 1, scale=LOG2E)
    pen, penfar = _moba_pen(proj, kmean, rel_bias)
    n_steps = H_ATT * nq
    in_specs = [pl.BlockSpec((MOBA_BLOCK, ATT_DH), lambda h, i, *_: (i, COL_AQ + h)),
                pl.BlockSpec((t, ATT_DH), lambda h, i, *_: (0, h)),
                pl.BlockSpec((t, ATT_DH), lambda h, i, *_: (0, h)),
                pl.BlockSpec((None, n_far + 1, MOBA_BLOCK, MOBA_BLOCK), lambda h, i, *_: (h, 0, 0, 0)),
                pl.BlockSpec((None, nq + 8, MOBA_BLOCK), lambda h, i, *_: (h, 0, i)),
                pl.BlockSpec((None, nq, MOBA_BLOCK), lambda h, i, *_: (h, 0, i))]
    out_specs = [pl.BlockSpec((MOBA_BLOCK, ATT_DH), lambda h, i, *_: (i, h))]
    out_shape = [jax.ShapeDtypeStruct((t, ATT_W), BF16)]
    scratch = [pltpu.VMEM((t, ATT_DH), BF16), pltpu.VMEM((nq, ATT_DH, MOBA_BLOCK), BF16),
               pltpu.VMEM((ATT_DH, MOBA_BLOCK), F32),
               pltpu.VMEM((2, MOBA_GROUP, MOBA_BLOCK, MOBA_BLOCK), F32)]
    operands = [proj, keys, values, bias, pen, penfar]
    stream = None
    prefetch = []
    if side_pool is not None:
        pt = side_page_table.reshape(-1)
        page_rows = side_pool.shape[1]
        ppb = MOBA_BLOCK // page_rows
        assert ppb * page_rows == MOBA_BLOCK and pt.shape[0] % ppb == 0
        n_pg = ppb * -(-pt.shape[0] // (ppb * n_steps))
        stream = (-(-pt.shape[0] // n_pg), ppb)
        in_specs.append(pl.BlockSpec(memory_space=pl.ANY))
        out_specs.append(pl.BlockSpec((None, n_pg // ppb, H_ATT, ATT_DH), lambda h, i, *_: (h * nq + i, 0, 0, 0)))
        out_shape.append(jax.ShapeDtypeStruct((n_steps, n_pg // ppb, H_ATT, ATT_DH), F32))
        scratch += [pltpu.VMEM((2, n_pg) + side_pool.shape[1:], F32), pltpu.SemaphoreType.DMA((2, n_pg))]
        operands.append(side_pool)
        prefetch = [pt]
    outs = pl.pallas_call(
        functools.partial(_moba_seq_kernel, n_far=n_far, stream=stream),
        grid_spec=pltpu.PrefetchScalarGridSpec(
            num_scalar_prefetch=len(prefetch), grid=(H_ATT, nq), in_specs=in_specs, out_specs=out_specs,
            scratch_shapes=scratch),
        out_shape=out_shape,
        compiler_params=_cparams("arbitrary", "arbitrary"),
        name="moba_seq",
    )(*prefetch, *operands)
    if side_pool is None:
        return outs[0], None
    return outs[0], outs[1].reshape(-1, H_ATT, ATT_DH)[:pt.shape[0] // ppb]


FFN_CHUNK = 256
MOBA_STEP_SLOTS = 4


def _moba_select_kernel(ksum_ref, q_ref, sel_ref):
    kmean = ksum_ref[...] / MOBA_BLOCK
    prod = kmean.astype(BF16).astype(F32) * (q_ref[...] * ATT_SCALE).astype(BF16).astype(F32)[None]
    s = jnp.sum(prod, axis=-1, keepdims=True)
    nb = s.shape[0]
    mask = _top_blocks(s, nb, axis=0)
    idx = lax.broadcasted_iota(jnp.int32, s.shape, 0).astype(F32)
    for r in range(MOBA_TOPK):
        first = jnp.min(jnp.where(mask > 0.5, idx, float(nb)), axis=0)
        sel_ref[r] = first.astype(jnp.int32)
        mask = jnp.where(idx == first[None], 0.0, mask)


def _moba_select(block_sums, q_heads):
    nb_seq, n_blocks = block_sums.shape[:2]
    assert n_blocks >= MOBA_TOPK
    return pl.pallas_call(
        _moba_select_kernel,
        grid=(nb_seq,),
        in_specs=[pl.BlockSpec((None, n_blocks, H_ATT, ATT_DH), lambda b: (b, 0, 0, 0)),
                  pl.BlockSpec((None, H_ATT, ATT_DH), lambda b: (b, 0, 0))],
        out_specs=pl.BlockSpec((None, MOBA_TOPK, H_ATT, 1), lambda b: (b, 0, 0, 0)),
        out_shape=jax.ShapeDtypeStruct((nb_seq, MOBA_TOPK, H_ATT, 1), jnp.int32),
        compiler_params=_cparams("parallel"),
        name="moba_select",
    )(block_sums, q_heads)


def _moba_step_kernel(pt_ref, sel_ref, q_ref, kn_ref, vn_ref, b0_ref, b1_ref, b2_ref, bown_ref, ck_hbm, cv_hbm,
                      o_ref, kbuf, vbuf, sem, *, n_pages):
    n_pg = MOBA_TOPK * 2
    b_refs = (b0_ref, b1_ref, b2_ref)
    n_slots = kbuf.shape[0]
    page_rows = kbuf.shape[2]
    b = pl.program_id(0)
    h = pl.program_id(1)
    n_heads = pl.num_programs(1)
    n_steps = pl.num_programs(0) * n_heads
    step = b * n_heads + h
    slot = step % n_slots

    def page_copies(bb, hh, sl):
        cps = []
        for r in range(MOBA_TOPK):
            blk = sel_ref[(bb * MOBA_TOPK + r) * n_heads + hh]
            for half in range(2):
                page = pt_ref[bb * n_pages + 2 * blk + half]
                idx = 2 * r + half
                cps.append(pltpu.make_async_copy(ck_hbm.at[page, :, hh, :], kbuf.at[sl, idx], sem.at[sl, idx]))
                cps.append(pltpu.make_async_copy(cv_hbm.at[page, :, hh, :], vbuf.at[sl, idx], sem.at[sl, n_pg + idx]))
        return cps

    def start_step(st):
        for cp in page_copies(st // n_heads, st % n_heads, st % n_slots):
            cp.start()

    @pl.when(step == 0)
    def _():
        for ahead in range(n_slots - 1):
            @pl.when(ahead < n_steps)
            def _():
                start_step(jnp.int32(ahead))

    @pl.when(step + n_slots - 1 < n_steps)
    def _():
        start_step(step + n_slots - 1)

    for cp in page_copies(b, h, slot):
        cp.wait()

    qs = q_ref[...] * ATT_SCALE
    eye = (lax.broadcasted_iota(jnp.int32, (page_rows, page_rows), 0)
           == lax.broadcasted_iota(jnp.int32, (page_rows, page_rows), 1))
    logits = []
    for r in range(MOBA_TOPK):
        bias = b_refs[r][...]
        for half in range(2):
            lg = jnp.sum(kbuf[slot, 2 * r + half] * qs, axis=-1, keepdims=True)
            brow = bias[:, half * page_rows:(half + 1) * page_rows]
            bcol = jnp.sum(jnp.where(eye, brow, 0.0), axis=-1, keepdims=True)
            logits.append(lg + bcol)
    l_own = jnp.sum(qs * kn_ref[...], axis=-1, keepdims=True) + bown_ref[:, 0:1]
    m = l_own
    for lg in logits:
        m = jnp.maximum(m, jnp.max(lg, axis=0, keepdims=True))
    p_own = jnp.exp(l_own - m)
    den = p_own
    acc = p_own * vn_ref[...]
    for idx, lg in enumerate(logits):
        p = jnp.exp(lg - m)
        den = den + jnp.sum(p, axis=0, keepdims=True)
        acc = acc + jnp.sum(p * vbuf[slot, idx], axis=0, keepdims=True)
    o_ref[...] = (acc / den).astype(o_ref.dtype)


def _moba_step(proj, ak, av, block_sums, cache_k4, cache_v4, page_table, rel_bias):
    nb_seq, n_pages = page_table.shape
    page_rows = cache_k4.shape[1]
    ppb = MOBA_BLOCK // page_rows
    assert ppb == 2
    n_blocks = n_pages // ppb
    n_past = n_pages * page_rows
    head_rows = lambda a: a.reshape(nb_seq, H_ATT, 1, ATT_DH)
    aq = proj[:, COL_AQ * 128:COL_AQ * 128 + ATT_W]
    sel = _moba_select(block_sums.reshape(nb_seq, n_blocks, H_ATT, ATT_DH), aq.reshape(nb_seq, H_ATT, ATT_DH))
    bias = _bias_tiles(rel_bias, n_blocks + 1, 1, MOBA_BLOCK, n_past, -MOBA_BLOCK, 0, -1)

    def bias_spec(r):
        return pl.BlockSpec((None, None, 1, MOBA_BLOCK),
                            lambda b, h, pt, sl: (h, sl[(b * MOBA_TOPK + r) * H_ATT + h], 0, 0))

    row_spec = pl.BlockSpec((None, None, 1, ATT_DH), lambda b, h, pt, sl: (b, h, 0, 0))
    hbm = pl.BlockSpec(memory_space=pl.ANY)
    n_pg = 2 * MOBA_TOPK
    out = pl.pallas_call(
        functools.partial(_moba_step_kernel, n_pages=n_pages),
        grid_spec=pltpu.PrefetchScalarGridSpec(
            num_scalar_prefetch=2,
            grid=(nb_seq, H_ATT),
            in_specs=[row_spec, row_spec, row_spec] + [bias_spec(r) for r in range(MOBA_TOPK)]
                     + [pl.BlockSpec((None, None, 1, MOBA_BLOCK), lambda b, h, pt, sl: (h, n_blocks, 0, 0)),
                        hbm, hbm],
            out_specs=row_spec,
            scratch_shapes=[pltpu.VMEM((MOBA_STEP_SLOTS, n_pg, page_rows, ATT_DH), F32),
                            pltpu.VMEM((MOBA_STEP_SLOTS, n_pg, page_rows, ATT_DH), F32),
                            pltpu.SemaphoreType.DMA((MOBA_STEP_SLOTS, 2 * n_pg))]),
        out_shape=jax.ShapeDtypeStruct((nb_seq, H_ATT, 1, ATT_DH), BF16),
        compiler_params=_cparams("arbitrary", "arbitrary"),
        name="moba_step",
    )(page_table.reshape(-1), sel.reshape(-1), head_rows(aq), head_rows(ak), head_rows(av),
      bias, bias, bias, bias, cache_k4, cache_v4)
    return out.reshape(nb_seq, ATT_W)


def _merge_out_kernel(oh_ref, oa_ref, gh0_ref, gh1_ref, ga0_ref, ga1_ref, wh_ref, wa_ref, wo_ref, x_ref, g_ref,
                      gnext_ref, o_ref, h_ref):
    half = D_MODEL // 2
    br_h = _dot(oh_ref[...], wh_ref[...])
    br_a = _dot(oa_ref[...], wa_ref[...])
    merged = jnp.concatenate(
        [(jax.nn.sigmoid(gh_ref[...]) * br_h[:, c * half:(c + 1) * half]
          + jax.nn.sigmoid(ga_ref[...]) * br_a[:, c * half:(c + 1) * half]).astype(BF16)
         for c, (gh_ref, ga_ref) in enumerate(((gh0_ref, ga0_ref), (gh1_ref, ga1_ref)))], axis=1)
    z = _dot(merged, wo_ref[...])
    ms = jnp.mean(z * z, axis=-1, keepdims=True)
    x1 = x_ref[...] + z * lax.rsqrt(ms + EPS) * g_ref[...]
    o_ref[...] = x1
    ms1 = jnp.mean(x1 * x1, axis=-1, keepdims=True)
    h_ref[...] = (x1 * lax.rsqrt(ms1 + EPS) * gnext_ref[...]).astype(h_ref.dtype)


def _merge_out(o_hg, o_att, proj, w_bh, w_bm, w_out, x, gain, gain_next, tm):
    m = x.shape[0]
    half = D_MODEL // 2
    assert half == 1024
    resident = pl.Buffered(1)
    vec = pl.BlockSpec((1, D_MODEL), lambda i: (0, 0), pipeline_mode=resident)
    row = pl.BlockSpec((tm, D_MODEL), lambda i: (i, 0))
    gate = lambda blk: pl.BlockSpec((tm, half), lambda i, blk=blk: (i, blk))
    return pl.pallas_call(
        _merge_out_kernel,
        grid=(m // tm,),
        in_specs=[pl.BlockSpec((tm, HG_VW), lambda i: (i, 0)), pl.BlockSpec((tm, ATT_W), lambda i: (i, 0)),
                  gate(COLK_GHG), gate(COLK_GHG + 1), gate(COLK_GATT), gate(COLK_GATT + 1),
                  pl.BlockSpec((HG_VW, D_MODEL), lambda i: (0, 0), pipeline_mode=resident),
                  pl.BlockSpec((ATT_W, D_MODEL), lambda i: (0, 0), pipeline_mode=resident),
                  pl.BlockSpec((D_MODEL, D_MODEL), lambda i: (0, 0), pipeline_mode=resident),
                  row, vec, vec],
        out_specs=[row, row],
        out_shape=[jax.ShapeDtypeStruct((m, D_MODEL), F32), jax.ShapeDtypeStruct((m, D_MODEL), BF16)],
        compiler_params=_cparams("parallel"),
        name="merge_out_proj_residual",
    )(o_hg, o_att, proj, proj, proj, proj, w_bh, w_bm, w_out, x, gain.reshape(1, D_MODEL),
      gain_next.reshape(1, D_MODEL))


def _ffn_up_kernel(h_ref, wg_ref, wu_ref, cw_ref, cb_ref, p0_ref, p1_ref, a_ref, g_out_ref, carry_s, *, seq):
    i = pl.program_id(0)
    j = pl.program_id(1)
    tm, tf = a_ref.shape
    if seq:
        @pl.when(i == 0)
        def _():
            carry_s[j, 0:1, :] = p0_ref[...]
            carry_s[j, 1:2, :] = p1_ref[...]

    top = lax.broadcasted_iota(jnp.int32, (8, FFN_CHUNK), 0)
    for c in range(tf // FFN_CHUNK):
        cols = slice(c * FFN_CHUNK, (c + 1) * FFN_CHUNK)
        g = _dot(h_ref[...], wg_ref[:, cols].astype(BF16))
        u = _dot(h_ref[...], wu_ref[:, cols].astype(BF16))
        if seq:
            c0 = carry_s[j, 0:1, cols]
            c1 = carry_s[j, 1:2, cols]
            r1 = pltpu.roll(g, 1, axis=0)
            r2 = pltpu.roll(g, 2, axis=0)
            prev1 = jnp.concatenate([jnp.where(top == 0, c1, r1[0:8]), r1[8:]], axis=0)
            prev2 = jnp.concatenate([jnp.where(top == 0, c0, jnp.where(top == 1, c1, r2[0:8])), r2[8:]], axis=0)
            carry_s[j, 0:2, cols] = g[tm - 2:tm, :]
            g_out_ref[:, cols] = g[tm - 2:tm, :]
        else:
            prev2 = p0_ref[:, cols]
            prev1 = p1_ref[:, cols]
            g_out_ref[:, cols] = g
        gc = cb_ref[:, cols] + cw_ref[0:1, cols] * prev2 + cw_ref[1:2, cols] * prev1 + cw_ref[2:3, cols] * g
        a_ref[:, cols] = (jax.nn.gelu(gc, approximate=True) * u).astype(a_ref.dtype)


def _ffn_down_kernel(a_ref, wd_ref, x_ref, g_ref, o_ref):
    y = _dot(a_ref[...], wd_ref[...])
    ms = jnp.mean(y * y, axis=-1, keepdims=True)
    o_ref[...] = x_ref[...] + y * lax.rsqrt(ms + EPS) * g_ref[...]


def _ffn(x, h, w_up, conv_w, conv_b, prev0, prev1, w_down, gpost, *, seq, tm_up, tm_down, tf=512):
    m = x.shape[0]
    nf = D_FF // tf
    assert CONV_W == 3 and nf * tf == D_FF
    prow = 1 if seq else tm_up
    pspec = pl.BlockSpec((prow, tf), (lambda i, j: (0, j)) if seq else (lambda i, j: (i, j)))
    if seq:
        gspec = pl.BlockSpec((None, CONV_W - 1, tf), lambda i, j: (i, 0, j))
        gshape = (m // tm_up, CONV_W - 1, D_FF)
    else:
        gspec = pl.BlockSpec((tm_up, tf), lambda i, j: (i, j))
        gshape = (m, D_FF)
    act, g_out = pl.pallas_call(
        functools.partial(_ffn_up_kernel, seq=seq),
        grid=(m // tm_up, nf),
        in_specs=[pl.BlockSpec((tm_up, D_MODEL), lambda i, j: (i, 0)),
                  pl.BlockSpec((D_MODEL, tf), lambda i, j: (0, j)),
                  pl.BlockSpec((D_MODEL, tf), lambda i, j: (0, nf + j)),
                  pl.BlockSpec((CONV_W, tf), lambda i, j: (0, j)),
                  pl.BlockSpec((1, tf), lambda i, j: (0, j)),
                  pspec, pspec],
        out_specs=[pl.BlockSpec((tm_up, tf), lambda i, j: (i, j)), gspec],
        out_shape=[jax.ShapeDtypeStruct((m, D_FF), BF16), jax.ShapeDtypeStruct(gshape, F32)],
        scratch_shapes=[pltpu.VMEM((nf, 8, tf), F32)],
        compiler_params=_cparams("arbitrary", "arbitrary"),
        name="conv_ffn_up",
    )(h, w_up, w_up, conv_w, conv_b.reshape(1, D_FF), prev0, prev1)
    resident = pl.Buffered(1)
    y = pl.pallas_call(
        _ffn_down_kernel,
        grid=(m // tm_down,),
        in_specs=[pl.BlockSpec((tm_down, D_FF), lambda i: (i, 0)),
                  pl.BlockSpec((D_FF, D_MODEL), lambda i: (0, 0), pipeline_mode=resident),
                  pl.BlockSpec((tm_down, D_MODEL), lambda i: (i, 0)),
                  pl.BlockSpec((1, D_MODEL), lambda i: (0, 0), pipeline_mode=resident)],
        out_specs=pl.BlockSpec((tm_down, D_MODEL), lambda i: (i, 0)),
        out_shape=jax.ShapeDtypeStruct((m, D_MODEL), F32),
        compiler_params=_cparams("parallel"),
        name="ffn_down_residual",
    )(act, w_down, x, gpost.reshape(1, D_MODEL))
    return y, (g_out[-1] if seq else g_out)


def kernel(x_prompt, x_sample, cache_k, cache_v, state_hgrn, state_ffn_conv, page_table, rel_bias, hg_lb,
           norm_mix_pre, norm_mix_post, norm_ffn_pre, norm_ffn_post, w_in, hg_out_norm, w_branch_hgrn,
           w_branch_moba, w_out, w_ffn_up, ffn_conv_w, ffn_conv_b, w_ffn_down):
    nb, t, _ = x_prompt.shape
    db = x_sample.shape[0]
    depth = w_in.shape[0]
    assert nb == 1 and depth == 1 and x_sample.shape[1] == 1
    l = 0
    w_in_b = w_in[l].astype(BF16)
    w_bh = w_branch_hgrn[l].astype(BF16)
    w_bm = w_branch_moba[l].astype(BF16)
    w_o = w_out[l].astype(BF16)
    w_up = w_ffn_up[l]
    w_dn = w_ffn_down[l].astype(BF16)

    n_pool, page_rows = cache_k.shape[1], cache_k.shape[2]
    ck = cache_k.reshape(depth * n_pool, page_rows, H_ATT, ATT_DH)
    cv = cache_v.reshape(depth * n_pool, page_rows, H_ATT, ATT_DH)

    xp = x_prompt.reshape(t, D_MODEL)
    proj, k_new, v_new = _norm_matmul(xp, norm_mix_pre[l], w_in_b, tm=1024)
    o_hg, s_new = _hgrn_seq(proj, hg_lb, hg_out_norm[l])
    o_att, cache_block_sums = _moba_seq(proj, k_new, v_new, rel_bias, side_pool=ck, side_page_table=page_table)
    x1, h2 = _merge_out(o_hg, o_att, proj, w_bh, w_bm, w_o, xp, norm_mix_post[l], norm_ffn_pre[l], tm=256)
    zero_row = jnp.zeros((1, D_FF), F32)
    yp, conv_p = _ffn(x1, h2, w_up, ffn_conv_w[l], ffn_conv_b[l], zero_row, zero_row, w_dn,
                      norm_ffn_post[l], seq=True, tm_up=1024, tm_down=256)
    kp = k_new.reshape(1, 1, t, H_ATT, ATT_DH)
    vp = v_new.reshape(1, 1, t, H_ATT, ATT_DH)

    xs = x_sample.reshape(db, D_MODEL)
    projs, k_new_s, v_new_s = _norm_matmul(xs, norm_mix_pre[l], w_in_b, tm=db)
    o_hg_s, s_new_s = _hgrn_step(projs, state_hgrn[l], hg_lb, hg_out_norm[l])
    o_att_s = _moba_step(projs, k_new_s, v_new_s, cache_block_sums, ck, cv, page_table, rel_bias)
    x1s, h2s = _merge_out(o_hg_s.reshape(db, HG_VW), o_att_s, projs, w_bh, w_bm, w_o, xs, norm_mix_post[l],
                          norm_ffn_pre[l], tm=db)
    buf = state_ffn_conv[l]
    ys, g_s = _ffn(x1s, h2s, w_up, ffn_conv_w[l], ffn_conv_b[l], buf[:, 0], buf[:, 1], w_dn,
                   norm_ffn_post[l], seq=False, tm_up=db, tm_down=db)
    ks = k_new_s.reshape(1, db, 1, H_ATT, ATT_DH)
    vs = v_new_s.reshape(1, db, 1, H_ATT, ATT_DH)
    conv_s = jnp.stack([buf[:, 1], g_s], axis=1)

    return (yp.reshape(1, t, D_MODEL), ys.reshape(db, 1, D_MODEL), kp, vp,
            s_new.reshape(1, 1, H_HG, HG_DK, HG_DV), conv_p.reshape(1, 1, CONV_W - 1, D_FF),
            ks, vs, s_new_s.reshape(1, db, H_HG, HG_DK, HG_DV), conv_s.reshape(1, db, CONV_W - 1, D_FF))
```

```python
import functools
import math

import numpy as np
import jax
import jax.numpy as jnp
from jax import lax
from jax.experimental import pallas as pl
from jax.experimental.pallas import tpu as pltpu

F32 = jnp.float32
BF16 = jnp.bfloat16

D_MODEL = 2048
H_HG = 8
HG_DK = 128
HG_DV = 128
HG_W = H_HG * HG_DK
HG_VW = H_HG * HG_DV
H_ATT = 8
ATT_DH = 128
ATT_W = H_ATT * ATT_DH
MOBA_BLOCK = 256
MOBA_TOPK = 3
N_BUCKETS = 32
MAX_EXACT = N_BUCKETS // 2
REL_MAX_DIST = 1024
D_FF = 5632
CONV_W = 3
EPS = 1e-6
N_IN = 2 * HG_W + 2 * HG_VW + 3 * ATT_W + 2 * D_MODEL
ATT_SCALE = ATT_DH ** -0.5
LOG2E = math.log2(math.e)

COL_HQ, COL_HF, COL_HI, COL_OG = 0, 8, 16, 24
COL_AQ = 32
COLK_AK, COLK_AV = 5, 6
COLK_GHG, COLK_GATT = 5, 7

HG_CHUNK = 128
HG_SUB = 8
HG_HEADS_PER_STEP = 8
MOBA_GROUP = 8
VMEM_LIMIT = 56 * 1024 * 1024
NEG_INF = float("-inf")


def _bucket_thresholds():
    n = np.arange(MAX_EXACT, 4 * REL_MAX_DIST, dtype=np.float64)
    large = MAX_EXACT + (np.log(n / MAX_EXACT) / math.log(REL_MAX_DIST / MAX_EXACT)
                         * (N_BUCKETS - MAX_EXACT)).astype(np.int64)
    large = np.minimum(large, N_BUCKETS - 1)
    return [int(n[np.argmax(large >= b)]) for b in range(MAX_EXACT + 1, N_BUCKETS)]


BUCKET_THRESHOLDS = _bucket_thresholds()


def _cparams(*sem):
    return pltpu.CompilerParams(dimension_semantics=sem, vmem_limit_bytes=VMEM_LIMIT)


def _silu(x):
    return x * jax.nn.sigmoid(x)


def _dot(a, b):
    return jnp.dot(a, b, preferred_element_type=F32)


def _dot_nt(a, b, precision=None):
    return lax.dot_general(a, b, (((1,), (1,)), ((), ())), precision=precision,
                           preferred_element_type=F32)


def _norm_matmul_kernel(x_ref, g_ref, w_ref, o_ref, k_ref, v_ref, h_ref):
    j = pl.program_id(1)

    @pl.when(j == 0)
    def _():
        x = x_ref[...]
        ms = jnp.mean(x * x, axis=-1, keepdims=True)
        h_ref[...] = (x * lax.rsqrt(ms + EPS) * g_ref[...]).astype(BF16)

    @pl.when(j == COLK_AK)
    def _():
        k_ref[...] = _dot(h_ref[...], w_ref[...])

    @pl.when(j == COLK_AV)
    def _():
        v_ref[...] = _dot(h_ref[...], w_ref[...])

    @pl.when(jnp.logical_and(j != COLK_AK, j != COLK_AV))
    def _():
        o_ref[...] = _dot(h_ref[...], w_ref[...])


def _norm_matmul(x, gain, w_bf16, tm):
    m, k = x.shape
    n = w_bf16.shape[1]
    tn = ATT_W
    assert COLK_AV == COLK_AK + 1
    return pl.pallas_call(
        _norm_matmul_kernel,
        grid=(m // tm, n // tn),
        in_specs=[pl.BlockSpec((tm, k), lambda i, j: (i, 0)),
                  pl.BlockSpec((1, k), lambda i, j: (0, 0)),
                  pl.BlockSpec((k, tn), lambda i, j: (0, j))],
        out_specs=[pl.BlockSpec((tm, tn), lambda i, j: (i, jnp.where(j < COLK_AK, j, jnp.maximum(j - 2, COLK_AK - 1)))),
                   pl.BlockSpec((tm, tn), lambda i, j: (i, 0)),
                   pl.BlockSpec((tm, tn), lambda i, j: (i, 0))],
        out_shape=[jax.ShapeDtypeStruct((m, n - 2 * tn), F32), jax.ShapeDtypeStruct((m, tn), F32),
                   jax.ShapeDtypeStruct((m, tn), F32)],
        scratch_shapes=[pltpu.VMEM((tm, k), BF16)],
        compiler_params=_cparams("parallel", "arbitrary"),
        name="norm_in_proj",
    )(x, gain.reshape(1, k), w_bf16)


def _forget_lower_bound(lbraw):
    e = jnp.exp(lbraw - jnp.max(lbraw, axis=0, keepdims=True))
    return e[0:1] / jnp.sum(e, axis=0, keepdims=True)


def _hgrn_seq_kernel(hq_ref, hf_ref, hi_ref, og_ref, lbraw_ref, gn_ref, o_ref, s_out_ref,
                     st_ref, q_s, k_s, b_s, v_s, acc_s, *, chunk, heads):
    c = pl.program_id(1)

    @pl.when(c == 0)
    def _():
        st_ref[...] = jnp.zeros_like(st_ref)

    row = lax.broadcasted_iota(jnp.int32, (chunk, HG_DK), 0)
    rr = lax.broadcasted_iota(jnp.int32, (chunk, chunk), 0)
    cc = lax.broadcasted_iota(jnp.int32, (chunk, chunk), 1)
    sub = lax.broadcasted_iota(jnp.int32, (HG_SUB, HG_DK), 0)

    for hh in range(heads):
        cols = slice(hh * 128, (hh + 1) * 128)
        lb = _forget_lower_bound(lbraw_ref[:, cols])
        q = _silu(hq_ref[:, cols])
        f = lb + (1.0 - lb) * jax.nn.sigmoid(hf_ref[:, cols])
        k = 1.0 - f
        v = hi_ref[:, cols]

        b = jnp.log2(f)
        shift = 1
        while shift < chunk:
            b = b + jnp.where(row >= shift, pltpu.roll(b, shift, axis=0), 0.0)
            shift *= 2

        q_s[hh] = q
        k_s[hh] = k
        b_s[hh] = b
        v_s[hh] = v

        st = st_ref[hh]
        o = _dot_nt((q * jnp.exp2(b)).astype(BF16), st.astype(BF16))

        a = jnp.zeros((chunk, chunk), F32)
        hs = HG_SUB
        while hs < chunk:
            blk = 2 * hs
            ref_rows = jnp.concatenate(
                [jnp.broadcast_to(b_s[hh, m0 + hs - 1:m0 + hs, :], (blk, HG_DK)) for m0 in range(0, chunk, blk)],
                axis=0)
            second = (row & (blk - 1)) >= hs
            qd = jnp.where(second, q * jnp.exp2(b - ref_rows), 0.0)
            kd = jnp.where(second, 0.0, k * jnp.exp2(ref_rows - b))
            a_l = _dot_nt(qd.astype(BF16), kd.astype(BF16))
            sh = int(math.log2(blk))
            a = a + jnp.where((rr >> sh) == (cc >> sh), a_l, 0.0)
            hs = blk
        o = o + _dot(a.astype(BF16), v.astype(BF16))

        for r0 in range(0, chunk, HG_SUB):
            qi = q_s[hh, r0:r0 + HG_SUB, :]
            bi = b_s[hh, r0:r0 + HG_SUB, :]
            acc = jnp.zeros((HG_SUB, HG_DV), F32)
            for s in range(HG_SUB):
                ks = k_s[hh, r0 + s:r0 + s + 1, :]
                bs = b_s[hh, r0 + s:r0 + s + 1, :]
                vs = v_s[hh, r0 + s:r0 + s + 1, :]
                e = jnp.where(sub >= s, jnp.exp2(bi - bs), 0.0)
                w = jnp.sum(qi * ks * e, axis=-1, keepdims=True)
                acc = acc + w * vs
            acc_s[hh, r0:r0 + HG_SUB, :] = acc
        o = o + acc_s[hh]

        b_last = b_s[hh, chunk - 1:chunk, :]
        kd = k * jnp.exp2(b_last - b)
        st_new = st * jnp.exp2(b_last) + _dot(v.T.astype(BF16), kd.astype(BF16))
        st_ref[hh] = st_new

        ms = jnp.mean(o * o, axis=-1, keepdims=True)
        o_ref[:, cols] = (o * lax.rsqrt(ms + EPS) * gn_ref[...] * _silu(og_ref[:, cols])).astype(o_ref.dtype)

    @pl.when(c == pl.num_programs(1) - 1)
    def _():
        for hh in range(heads):
            s_out_ref[hh] = st_ref[hh].T


def _hgrn_seq(proj, hg_lb, gn, chunk=HG_CHUNK, heads=HG_HEADS_PER_STEP):
    t = proj.shape[0]
    w = 128 * heads
    blk = lambda off: pl.BlockSpec((chunk, w), lambda h, c, off=off: (c, off // heads + h))
    return pl.pallas_call(
        functools.partial(_hgrn_seq_kernel, chunk=chunk, heads=heads),
        grid=(H_HG // heads, t // chunk),
        in_specs=[blk(COL_HQ), blk(COL_HF), blk(COL_HI), blk(COL_OG),
                  pl.BlockSpec((hg_lb.shape[0], w), lambda h, c: (0, h)),
                  pl.BlockSpec((1, HG_DV), lambda h, c: (0, 0))],
        out_specs=[pl.BlockSpec((chunk, w), lambda h, c: (c, h)),
                   pl.BlockSpec((heads, HG_DK, HG_DV), lambda h, c: (h, 0, 0))],
        out_shape=[jax.ShapeDtypeStruct((t, HG_VW), BF16),
                   jax.ShapeDtypeStruct((H_HG, HG_DK, HG_DV), F32)],
        scratch_shapes=[pltpu.VMEM((heads, HG_DV, HG_DK), F32)] + [pltpu.VMEM((heads, chunk, 128), F32)] * 5,
        compiler_params=_cparams("parallel", "arbitrary"),
        name="hgrn_seq",
    )(proj, proj, proj, proj, hg_lb, gn.reshape(1, HG_DV))


def _hgrn_step_kernel(hqc_ref, hfc_ref, hi_ref, og_ref, lbc_ref, gn_ref, s_ref, o_ref, s_out_ref):
    lbraw = lbc_ref[...]
    e = jnp.exp(lbraw - jnp.max(lbraw, axis=0, keepdims=True))
    lb = e[0] / jnp.sum(e, axis=0)
    q = _silu(hqc_ref[...])
    f = lb + (1.0 - lb) * jax.nn.sigmoid(hfc_ref[...])
    k = 1.0 - f
    v = hi_ref[...]
    s_new = f * s_ref[...] + k * v
    s_out_ref[...] = s_new
    o = jnp.sum(s_new * q, axis=-2, keepdims=True)
    ms = jnp.mean(o * o, axis=-1, keepdims=True)
    o_ref[...] = (o * lax.rsqrt(ms + EPS) * gn_ref[...] * _silu(og_ref[...])).astype(o_ref.dtype)


def _hgrn_step(proj, state, hg_lb, gn):
    nb = proj.shape[0]
    col = lambda a: a.reshape(nb, H_HG, HG_DK, 1)
    rowv = lambda a: a.reshape(nb, H_HG, 1, HG_DV)
    hq = col(proj[:, 0:HG_W])
    hf = col(proj[:, HG_W:2 * HG_W])
    hi = rowv(proj[:, 2 * HG_W:2 * HG_W + HG_VW])
    og = rowv(proj[:, 2 * HG_W + HG_VW:2 * HG_W + 2 * HG_VW])
    nl = hg_lb.shape[0]
    per = 4 if nb % 4 == 0 else 1
    cspec = pl.BlockSpec((per, H_HG, HG_DK, 1), lambda b: (b, 0, 0, 0))
    rspec = pl.BlockSpec((per, H_HG, 1, HG_DV), lambda b: (b, 0, 0, 0))
    sspec = pl.BlockSpec((per, H_HG, HG_DK, HG_DV), lambda b: (b, 0, 0, 0))
    return pl.pallas_call(
        _hgrn_step_kernel,
        grid=(nb // per,),
        in_specs=[cspec, cspec, rspec, rspec,
                  pl.BlockSpec((nl, H_HG, HG_DK, 1), lambda b: (0, 0, 0, 0)),
                  pl.BlockSpec((1, 1, HG_DV), lambda b: (0, 0, 0)),
                  sspec],
        out_specs=[rspec, sspec],
        out_shape=[jax.ShapeDtypeStruct((nb, H_HG, 1, HG_DV), BF16),
                   jax.ShapeDtypeStruct((nb, H_HG, HG_DK, HG_DV), F32)],
        compiler_params=_cparams("parallel"),
        name="hgrn_step",
    )(hq, hf, hi, og, hg_lb.reshape(nl, H_HG, HG_DK, 1), gn.reshape(1, 1, HG_DV), state)


BUCKET_STARTS = list(range(1, MAX_EXACT + 1)) + BUCKET_THRESHOLDS


def _bias_from_rel(rel, lo, hi, tab_ref, h):
    first = sum(1 for start in BUCKET_STARTS if start <= lo)
    out = jnp.full(rel.shape, tab_ref[first, h], F32)
    for bkt in range(first + 1, N_BUCKETS):
        if BUCKET_STARTS[bkt - 1] <= hi:
            out = jnp.where(rel >= BUCKET_STARTS[bkt - 1], tab_ref[bkt, h], out)
    return jnp.where(rel >= 0, out, NEG_INF) if lo < 0 else out


def _bias_tiles_kernel(tab_ref, o_ref, *, base, tile_step, row_step, col_step, scale):
    h = pl.program_id(0)
    n_tiles, rows, cols = o_ref.shape
    r = lax.broadcasted_iota(jnp.int32, (rows, cols), 0)
    c = lax.broadcasted_iota(jnp.int32, (rows, cols), 1)
    in_tile = r * row_step + c * col_step
    spans = [row_step * (rows - 1), col_step * (cols - 1)]
    for t in range(n_tiles):
        off = base + t * tile_step
        lo = off + sum(min(sp, 0) for sp in spans)
        hi = off + sum(max(sp, 0) for sp in spans)
        bias = _bias_from_rel(off + in_tile, lo, hi, tab_ref, h)
        o_ref[t] = bias if scale == 1.0 else bias * scale


def _bias_tiles(rel_bias, n_tiles, rows, cols, base, tile_step, row_step, col_step, scale=1.0):
    return pl.pallas_call(
        functools.partial(_bias_tiles_kernel, base=base, tile_step=tile_step, row_step=row_step,
                          col_step=col_step, scale=scale),
        grid=(H_ATT,),
        in_specs=[pl.BlockSpec(memory_space=pltpu.SMEM)],
        out_specs=pl.BlockSpec((None, n_tiles, rows, cols), lambda h: (h, 0, 0, 0)),
        out_shape=jax.ShapeDtypeStruct((H_ATT, n_tiles, rows, cols), F32),
        compiler_params=_cparams("parallel"),
        name="rel_bias_tiles",
    )(rel_bias)


def _block_mean_kernel(k_ref, o_ref):
    for blk in range(o_ref.shape[0]):
        o_ref[blk] = jnp.mean(k_ref[blk * MOBA_BLOCK:(blk + 1) * MOBA_BLOCK, :], axis=0, keepdims=True)


def _block_mean(keys, blocks_per_step=8):
    t = keys.shape[0]
    nb = t // MOBA_BLOCK
    assert nb % blocks_per_step == 0
    out = pl.pallas_call(
        _block_mean_kernel,
        grid=(nb // blocks_per_step,),
        in_specs=[pl.BlockSpec((blocks_per_step * MOBA_BLOCK, ATT_W), lambda n: (n, 0))],
        out_specs=pl.BlockSpec((blocks_per_step, 1, ATT_W), lambda n: (n, 0, 0)),
        out_shape=jax.ShapeDtypeStruct((nb, 1, ATT_W), F32),
        compiler_params=_cparams("parallel"),
        name="moba_block_mean",
    )(keys)
    return out.reshape(nb, ATT_W)


def _top_blocks(s, n_valid, axis):
    n = float(s.shape[axis])
    idx = lax.broadcasted_iota(jnp.int32, s.shape, axis).astype(F32)
    s = jnp.where(idx < n_valid, s, NEG_INF)
    sel = jnp.zeros(s.shape, F32)
    for _ in range(MOBA_TOPK):
        m = jnp.max(s, axis=axis, keepdims=True)
        first = jnp.min(jnp.where(s == m, idx, n), axis=axis, keepdims=True)
        pick = jnp.logical_and(idx == first, m > NEG_INF)
        sel = jnp.where(pick, 1.0, sel)
        s = jnp.where(pick, NEG_INF, s)
    return sel


def _moba_pen_kernel(tab_ref, q_ref, km_ref, pen_ref, penfar_ref):
    h = pl.program_id(0)
    nblk = km_ref.shape[0]
    tq = q_ref.shape[0]
    scores = _dot_nt(km_ref[...].astype(BF16), (q_ref[...] * ATT_SCALE).astype(BF16))
    pos = pl.program_id(1) * tq + lax.broadcasted_iota(jnp.int32, (1, tq), 1)
    own = pos // MOBA_BLOCK
    blk_id = lax.broadcasted_iota(jnp.int32, scores.shape, 0)
    chosen = _top_blocks(scores, own, axis=0) > 0.5
    pen_ref[0:nblk, :] = jnp.where(jnp.logical_or(chosen, blk_id == own), 0.0, NEG_INF)
    pen_ref[nblk:, :] = jnp.full((pen_ref.shape[0] - nblk, tq), NEG_INF, F32)
    far_bias = tab_ref[N_BUCKETS - 1, h] * LOG2E
    penfar_ref[...] = jnp.where(jnp.logical_and(chosen, blk_id <= own - MOBA_GROUP), far_bias, NEG_INF)


def _moba_pen(proj, kmean, rel_bias, tq=2048):
    t = proj.shape[0]
    nblk = t // MOBA_BLOCK
    return pl.pallas_call(
        _moba_pen_kernel,
        grid=(H_ATT, t // tq),
        in_specs=[pl.BlockSpec(memory_space=pltpu.SMEM),
                  pl.BlockSpec((tq, ATT_DH), lambda h, n: (n, COL_AQ + h)),
                  pl.BlockSpec((nblk, ATT_DH), lambda h, n: (0, h))],
        out_specs=[pl.BlockSpec((None, nblk + 8, tq), lambda h, n: (h, 0, n)),
                   pl.BlockSpec((None, nblk, tq), lambda h, n: (h, 0, n))],
        out_shape=[jax.ShapeDtypeStruct((H_ATT, nblk + 8, t), F32),
                   jax.ShapeDtypeStruct((H_ATT, nblk, t), F32)],
        compiler_params=_cparams("parallel", "parallel"),
        name="moba_block_masks",
    )(rel_bias, proj, kmean)


def _stream_page_sums(pt_ref, pages_hbm, sums_ref, pbuf, psem, step, n_steps, pages_per_block):
    n_pg = pbuf.shape[1]
    total = pt_ref.shape[0]
    slot = step % 2

    def copies(st, sl):
        return [pltpu.make_async_copy(pages_hbm.at[pt_ref[jnp.minimum(st * n_pg + p, total - 1)]],
                                      pbuf.at[sl, p], psem.at[sl, p]) for p in range(n_pg)]

    @pl.when(step == 0)
    def _():
        for cp in copies(step, slot):
            cp.start()

    @pl.when(step < n_steps)
    def _():
        for cp in copies(step, slot):
            cp.wait()

    @pl.when(step + 1 < n_steps)
    def _():
        for cp in copies(step + 1, 1 - slot):
            cp.start()

    def reduce_pages():
        for blk in range(n_pg // pages_per_block):
            tot = pbuf[slot, blk * pages_per_block].sum(axis=0)
            for pg in range(1, pages_per_block):
                tot = tot + pbuf[slot, blk * pages_per_block + pg].sum(axis=0)
            sums_ref[blk] = tot

    return reduce_pages


def _moba_seq_kernel(*refs, n_far, stream):
    if stream:
        n_stream_steps, pages_per_block = stream
        (pt_ref, q_ref, k_ref, v_ref, bias_ref, pen_ref, penfar_ref, pages_hbm, o_ref, sums_ref,
         kb_s, vt_s, acc_s, s_s, pbuf, psem) = refs
        reduce_pages = _stream_page_sums(
            pt_ref, pages_hbm, sums_ref, pbuf, psem,
            pl.program_id(0) * pl.num_programs(1) + pl.program_id(1), n_stream_steps, pages_per_block)
    else:
        q_ref, k_ref, v_ref, bias_ref, pen_ref, penfar_ref, o_ref, kb_s, vt_s, acc_s, s_s = refs
        reduce_pages = None
    i = pl.program_id(1)
    nblk = vt_s.shape[0]

    @pl.when(i == 0)
    def _():
        kb_s[...] = k_ref[...].astype(BF16)

        def xpose(j, carry):
            r0 = pl.multiple_of(j * MOBA_BLOCK, MOBA_BLOCK)
            vt_s[j] = v_ref[pl.ds(r0, MOBA_BLOCK), :].T.astype(BF16)
            return carry

        lax.fori_loop(0, nblk, xpose, 0)

    if reduce_pages is not None:
        reduce_pages()

    tq = q_ref.shape[0]
    grp = MOBA_GROUP
    qb = (q_ref[...] * (ATT_SCALE * LOG2E)).astype(BF16)
    acc_s[...] = jnp.zeros_like(acc_s)

    def logits(j):
        c0 = pl.multiple_of(j * MOBA_BLOCK, MOBA_BLOCK)
        return _dot_nt(kb_s[pl.ds(c0, MOBA_BLOCK), :], qb)

    def keep_max(smax, s):
        cur = jnp.max(s, axis=0, keepdims=True)
        return cur if smax is None else jnp.maximum(smax, cur)

    smax = None
    for u in range(grp):
        j = jnp.maximum(i - u, 0)
        pen_row = jnp.where(u <= i, j, nblk)
        s = logits(j) + bias_ref[min(u, n_far)] + pen_ref[pl.ds(pen_row, 1), :]
        s_s[0, u] = s
        smax = keep_max(smax, s)

    def softmax_update(slot, m_old, l_old, smax, block_of):
        mg = jnp.maximum(m_old, smax)
        alpha = jnp.exp2(m_old - mg)
        l_new = alpha * l_old
        pv = None
        for u in range(grp):
            p = jnp.exp2(s_s[slot, u] - mg)
            l_new = l_new + jnp.sum(p, axis=0, keepdims=True)
            d = _dot(vt_s[block_of(u)], p.astype(BF16))
            pv = d if pv is None else pv + d
        acc_s[...] = alpha * acc_s[...] + pv
        return mg, l_new

    def trip_with_slots(t, carry, slot_prev):
        m_old, l_old, smax_prev = carry
        smax_new = None
        for u in range(grp):
            s = logits(t * grp + u) + penfar_ref[pl.ds(t * grp + u, 1), :]
            s_s[1 - slot_prev, u] = s
            smax_new = keep_max(smax_new, s)
        prev = lambda u: jnp.where(t == 0, jnp.maximum(i - u, 0), (t - 1) * grp + u)
        m_new, l_new = softmax_update(slot_prev, m_old, l_old, smax_prev, prev)
        return m_new, l_new, smax_new

    def trip(t, carry):
        return lax.cond(t % 2 == 0, lambda c: trip_with_slots(t, c, 0), lambda c: trip_with_slots(t, c, 1), carry)

    n_far_groups = i // grp
    init = (jnp.full((1, tq), NEG_INF, F32), jnp.zeros((1, tq), F32), smax)
    m, l, smax = lax.fori_loop(0, n_far_groups, trip, init)
    last = lambda u: jnp.where(n_far_groups == 0, jnp.maximum(i - u, 0), (n_far_groups - 1) * grp + u)
    m, l = softmax_update(n_far_groups % 2, m, l, smax, last)
    o_ref[...] = (acc_s[...] / l).T.astype(o_ref.dtype)


def _moba_seq(proj, keys, values, rel_bias, side_pool=None, side_page_table=None):
    t = proj.shape[0]
    nq = t // MOBA_BLOCK
    kmean = _block_mean(keys)
    n_far = -(-(BUCKET_THRESHOLDS[-1] + MOBA_BLOCK) // MOBA_BLOCK)
    assert MOBA_GROUP >= n_far and nq % MOBA_GROUP == 0
    bias = _bias_tiles(rel_bias, n_far + 1, MOBA_BLOCK, MOBA_BLOCK, 0, MOBA_BLOCK, -1, 1, scale=LOG2E)
    pen, penfar = _moba_pen(proj, kmean, rel_bias)
    n_steps = H_ATT * nq
    in_specs = [pl.BlockSpec((MOBA_BLOCK, ATT_DH), lambda h, i, *_: (i, COL_AQ + h)),
                pl.BlockSpec((t, ATT_DH), lambda h, i, *_: (0, h)),
                pl.BlockSpec((t, ATT_DH), lambda h, i, *_: (0, h)),
                pl.BlockSpec((None, n_far + 1, MOBA_BLOCK, MOBA_BLOCK), lambda h, i, *_: (h, 0, 0, 0)),
                pl.BlockSpec((None, nq + 8, MOBA_BLOCK), lambda h, i, *_: (h, 0, i)),
                pl.BlockSpec((None, nq, MOBA_BLOCK), lambda h, i, *_: (h, 0, i))]
    out_specs = [pl.BlockSpec((MOBA_BLOCK, ATT_DH), lambda h, i, *_: (i, h))]
    out_shape = [jax.ShapeDtypeStruct((t, ATT_W), BF16)]
    scratch = [pltpu.VMEM((t, ATT_DH), BF16), pltpu.VMEM((nq, ATT_DH, MOBA_BLOCK), BF16),
               pltpu.VMEM((ATT_DH, MOBA_BLOCK), F32),
               pltpu.VMEM((2, MOBA_GROUP, MOBA_BLOCK, MOBA_BLOCK), F32)]
    operands = [proj, keys, values, bias, pen, penfar]
    stream = None
    prefetch = []
    if side_pool is not None:
        pt = side_page_table.reshape(-1)
        page_rows = side_pool.shape[1]
        ppb = MOBA_BLOCK // page_rows
        assert ppb * page_rows == MOBA_BLOCK and pt.shape[0] % ppb == 0
        n_pg = ppb * -(-pt.shape[0] // (ppb * n_steps))
        stream = (-(-pt.shape[0] // n_pg), ppb)
        in_specs.append(pl.BlockSpec(memory_space=pl.ANY))
        out_specs.append(pl.BlockSpec((None, n_pg // ppb, H_ATT, ATT_DH), lambda h, i, *_: (h * nq + i, 0, 0, 0)))
        out_shape.append(jax.ShapeDtypeStruct((n_steps, n_pg // ppb, H_ATT, ATT_DH), F32))
        scratch += [pltpu.VMEM((2, n_pg) + side_pool.shape[1:], F32), pltpu.SemaphoreType.DMA((2, n_pg))]
        operands.append(side_pool)
        prefetch = [pt]
    outs = pl.pallas_call(
        functools.partial(_moba_seq_kernel, n_far=n_far, stream=stream),
        grid_spec=pltpu.PrefetchScalarGridSpec(
            num_scalar_prefetch=len(prefetch), grid=(H_ATT, nq), in_specs=in_specs, out_specs=out_specs,
            scratch_shapes=scratch),
        out_shape=out_shape,
        compiler_params=_cparams("arbitrary", "arbitrary"),
        name="moba_seq",
    )(*prefetch, *operands)
    if side_pool is None:
        return outs[0], None
    return outs[0], outs[1].reshape(-1, H_ATT, ATT_DH)[:pt.shape[0] // ppb]


FFN_CHUNK = 256
MOBA_STEP_SLOTS = 4


def _moba_select_kernel(ksum_ref, q_ref, sel_ref):
    kmean = ksum_ref[...] / MOBA_BLOCK
    prod = kmean.astype(BF16).astype(F32) * (q_ref[...] * ATT_SCALE).astype(BF16).astype(F32)[None]
    s = jnp.sum(prod, axis=-1, keepdims=True)
    nb = s.shape[0]
    mask = _top_blocks(s, nb, axis=0)
    idx = lax.broadcasted_iota(jnp.int32, s.shape, 0).astype(F32)
    for r in range(MOBA_TOPK):
        first = jnp.min(jnp.where(mask > 0.5, idx, float(nb)), axis=0)
        sel_ref[r] = first.astype(jnp.int32)
        mask = jnp.where(idx == first[None], 0.0, mask)


def _moba_select(block_sums, q_heads):
    nb_seq, n_blocks = block_sums.shape[:2]
    assert n_blocks >= MOBA_TOPK
    return pl.pallas_call(
        _moba_select_kernel,
        grid=(nb_seq,),
        in_specs=[pl.BlockSpec((None, n_blocks, H_ATT, ATT_DH), lambda b: (b, 0, 0, 0)),
                  pl.BlockSpec((None, H_ATT, ATT_DH), lambda b: (b, 0, 0))],
        out_specs=pl.BlockSpec((None, MOBA_TOPK, H_ATT, 1), lambda b: (b, 0, 0, 0)),
        out_shape=jax.ShapeDtypeStruct((nb_seq, MOBA_TOPK, H_ATT, 1), jnp.int32),
        compiler_params=_cparams("parallel"),
        name="moba_select",
    )(block_sums, q_heads)


def _moba_step_kernel(pt_ref, sel_ref, q_ref, kn_ref, vn_ref, b0_ref, b1_ref, b2_ref, bown_ref, ck_hbm, cv_hbm,
                      o_ref, kbuf, vbuf, sem, *, n_pages):
    n_pg = MOBA_TOPK * 2
    b_refs = (b0_ref, b1_ref, b2_ref)
    n_slots = kbuf.shape[0]
    page_rows = kbuf.shape[2]
    b = pl.program_id(0)
    h = pl.program_id(1)
    n_heads = pl.num_programs(1)
    n_steps = pl.num_programs(0) * n_heads
    step = b * n_heads + h
    slot = step % n_slots

    def page_copies(bb, hh, sl):
        cps = []
        for r in range(MOBA_TOPK):
            blk = sel_ref[(bb * MOBA_TOPK + r) * n_heads + hh]
            for half in range(2):
                page = pt_ref[bb * n_pages + 2 * blk + half]
                idx = 2 * r + half
                cps.append(pltpu.make_async_copy(ck_hbm.at[page, :, hh, :], kbuf.at[sl, idx], sem.at[sl, idx]))
                cps.append(pltpu.make_async_copy(cv_hbm.at[page, :, hh, :], vbuf.at[sl, idx], sem.at[sl, n_pg + idx]))
        return cps

    def start_step(st):
        for n, cp in enumerate(page_copies(st // n_heads, st % n_heads, st % n_slots)):
            cp.start(priority=n % 2)

    @pl.when(step == 0)
    def _():
        for ahead in range(n_slots - 1):
            @pl.when(ahead < n_steps)
            def _():
                start_step(jnp.int32(ahead))

    @pl.when(step + n_slots - 1 < n_steps)
    def _():
        start_step(step + n_slots - 1)

    for cp in page_copies(b, h, slot):
        cp.wait()

    qs = q_ref[...] * ATT_SCALE
    eye = (lax.broadcasted_iota(jnp.int32, (page_rows, page_rows), 0)
           == lax.broadcasted_iota(jnp.int32, (page_rows, page_rows), 1))
    logits = []
    for r in range(MOBA_TOPK):
        bias = b_refs[r][...]
        for half in range(2):
            lg = jnp.sum(kbuf[slot, 2 * r + half] * qs, axis=-1, keepdims=True)
            brow = bias[:, half * page_rows:(half + 1) * page_rows]
            bcol = jnp.sum(jnp.where(eye, brow, 0.0), axis=-1, keepdims=True)
            logits.append(lg + bcol)
    l_own = jnp.sum(qs * kn_ref[...], axis=-1, keepdims=True) + bown_ref[:, 0:1]
    m = l_own
    for lg in logits:
        m = jnp.maximum(m, jnp.max(lg, axis=0, keepdims=True))
    p_own = jnp.exp(l_own - m)
    den = p_own
    acc = p_own * vn_ref[...]
    for idx, lg in enumerate(logits):
        p = jnp.exp(lg - m)
        den = den + jnp.sum(p, axis=0, keepdims=True)
        acc = acc + jnp.sum(p * vbuf[slot, idx], axis=0, keepdims=True)
    o_ref[...] = (acc / den).astype(o_ref.dtype)


def _moba_step(proj, ak, av, block_sums, cache_k4, cache_v4, page_table, rel_bias):
    nb_seq, n_pages = page_table.shape
    page_rows = cache_k4.shape[1]
    ppb = MOBA_BLOCK // page_rows
    assert ppb == 2
    n_blocks = n_pages // ppb
    n_past = n_pages * page_rows
    head_rows = lambda a: a.reshape(nb_seq, H_ATT, 1, ATT_DH)
    aq = proj[:, COL_AQ * 128:COL_AQ * 128 + ATT_W]
    sel = _moba_select(block_sums.reshape(nb_seq, n_blocks, H_ATT, ATT_DH), aq.reshape(nb_seq, H_ATT, ATT_DH))
    bias = _bias_tiles(rel_bias, n_blocks + 1, 1, MOBA_BLOCK, n_past, -MOBA_BLOCK, 0, -1)

    def bias_spec(r):
        return pl.BlockSpec((None, None, 1, MOBA_BLOCK),
                            lambda b, h, pt, sl: (h, sl[(b * MOBA_TOPK + r) * H_ATT + h], 0, 0))

    row_spec = pl.BlockSpec((None, None, 1, ATT_DH), lambda b, h, pt, sl: (b, h, 0, 0))
    hbm = pl.BlockSpec(memory_space=pl.ANY)
    n_pg = 2 * MOBA_TOPK
    out = pl.pallas_call(
        functools.partial(_moba_step_kernel, n_pages=n_pages),
        grid_spec=pltpu.PrefetchScalarGridSpec(
            num_scalar_prefetch=2,
            grid=(nb_seq, H_ATT),
            in_specs=[row_spec, row_spec, row_spec] + [bias_spec(r) for r in range(MOBA_TOPK)]
                     + [pl.BlockSpec((None, None, 1, MOBA_BLOCK), lambda b, h, pt, sl: (h, n_blocks, 0, 0)),
                        hbm, hbm],
            out_specs=row_spec,
            scratch_shapes=[pltpu.VMEM((MOBA_STEP_SLOTS, n_pg, page_rows, ATT_DH), F32),
                            pltpu.VMEM((MOBA_STEP_SLOTS, n_pg, page_rows, ATT_DH), F32),
                            pltpu.SemaphoreType.DMA((MOBA_STEP_SLOTS, 2 * n_pg))]),
        out_shape=jax.ShapeDtypeStruct((nb_seq, H_ATT, 1, ATT_DH), BF16),
        compiler_params=_cparams("arbitrary", "arbitrary"),
        name="moba_step",
    )(page_table.reshape(-1), sel.reshape(-1), head_rows(aq), head_rows(ak), head_rows(av),
      bias, bias, bias, bias, cache_k4, cache_v4)
    return out.reshape(nb_seq, ATT_W)


def _merge_out_kernel(oh_ref, oa_ref, gh0_ref, gh1_ref, ga0_ref, ga1_ref, wh_ref, wa_ref, wo_ref, x_ref, g_ref,
                      gnext_ref, o_ref, h_ref):
    half = D_MODEL // 2
    br_h = _dot(oh_ref[...], wh_ref[...])
    br_a = _dot(oa_ref[...], wa_ref[...])
    merged = jnp.concatenate(
        [(jax.nn.sigmoid(gh_ref[...]) * br_h[:, c * half:(c + 1) * half]
          + jax.nn.sigmoid(ga_ref[...]) * br_a[:, c * half:(c + 1) * half]).astype(BF16)
         for c, (gh_ref, ga_ref) in enumerate(((gh0_ref, ga0_ref), (gh1_ref, ga1_ref)))], axis=1)
    z = _dot(merged, wo_ref[...])
    ms = jnp.mean(z * z, axis=-1, keepdims=True)
    x1 = x_ref[...] + z * lax.rsqrt(ms + EPS) * g_ref[...]
    o_ref[...] = x1
    ms1 = jnp.mean(x1 * x1, axis=-1, keepdims=True)
    h_ref[...] = (x1 * lax.rsqrt(ms1 + EPS) * gnext_ref[...]).astype(h_ref.dtype)


def _merge_out(o_hg, o_att, proj, w_bh, w_bm, w_out, x, gain, gain_next, tm):
    m = x.shape[0]
    half = D_MODEL // 2
    assert half == 1024
    resident = pl.Buffered(1)
    vec = pl.BlockSpec((1, D_MODEL), lambda i: (0, 0), pipeline_mode=resident)
    row = pl.BlockSpec((tm, D_MODEL), lambda i: (i, 0))
    gate = lambda blk: pl.BlockSpec((tm, half), lambda i, blk=blk: (i, blk))
    return pl.pallas_call(
        _merge_out_kernel,
        grid=(m // tm,),
        in_specs=[pl.BlockSpec((tm, HG_VW), lambda i: (i, 0)), pl.BlockSpec((tm, ATT_W), lambda i: (i, 0)),
                  gate(COLK_GHG), gate(COLK_GHG + 1), gate(COLK_GATT), gate(COLK_GATT + 1),
                  pl.BlockSpec((HG_VW, D_MODEL), lambda i: (0, 0), pipeline_mode=resident),
                  pl.BlockSpec((ATT_W, D_MODEL), lambda i: (0, 0), pipeline_mode=resident),
                  pl.BlockSpec((D_MODEL, D_MODEL), lambda i: (0, 0), pipeline_mode=resident),
                  row, vec, vec],
        out_specs=[row, row],
        out_shape=[jax.ShapeDtypeStruct((m, D_MODEL), F32), jax.ShapeDtypeStruct((m, D_MODEL), BF16)],
        compiler_params=_cparams("parallel"),
        name="merge_out_proj_residual",
    )(o_hg, o_att, proj, proj, proj, proj, w_bh, w_bm, w_out, x, gain.reshape(1, D_MODEL),
      gain_next.reshape(1, D_MODEL))


def _ffn_up_kernel(h_ref, wg_ref, wu_ref, cw_ref, cb_ref, p0_ref, p1_ref, a_ref, g_out_ref, carry_s, *, seq):
    i = pl.program_id(0)
    j = pl.program_id(1)
    tm, tf = a_ref.shape
    if seq:
        @pl.when(i == 0)
        def _():
            carry_s[j, 0:1, :] = p0_ref[...]
            carry_s[j, 1:2, :] = p1_ref[...]

    top = lax.broadcasted_iota(jnp.int32, (8, FFN_CHUNK), 0)
    for c in range(tf // FFN_CHUNK):
        cols = slice(c * FFN_CHUNK, (c + 1) * FFN_CHUNK)
        g = _dot(h_ref[...], wg_ref[:, cols].astype(BF16))
        u = _dot(h_ref[...], wu_ref[:, cols].astype(BF16))
        if seq:
            c0 = carry_s[j, 0:1, cols]
            c1 = carry_s[j, 1:2, cols]
            r1 = pltpu.roll(g, 1, axis=0)
            r2 = pltpu.roll(g, 2, axis=0)
            prev1 = jnp.concatenate([jnp.where(top == 0, c1, r1[0:8]), r1[8:]], axis=0)
            prev2 = jnp.concatenate([jnp.where(top == 0, c0, jnp.where(top == 1, c1, r2[0:8])), r2[8:]], axis=0)
            carry_s[j, 0:2, cols] = g[tm - 2:tm, :]
            g_out_ref[:, cols] = g[tm - 2:tm, :]
        else:
            prev2 = p0_ref[:, cols]
            prev1 = p1_ref[:, cols]
            g_out_ref[:, cols] = g
        gc = cb_ref[:, cols] + cw_ref[0:1, cols] * prev2 + cw_ref[1:2, cols] * prev1 + cw_ref[2:3, cols] * g
        a_ref[:, cols] = (jax.nn.gelu(gc, approximate=True) * u).astype(a_ref.dtype)


def _ffn_down_kernel(a_ref, wd_ref, x_ref, g_ref, o_ref):
    y = _dot(a_ref[...], wd_ref[...])
    ms = jnp.mean(y * y, axis=-1, keepdims=True)
    o_ref[...] = x_ref[...] + y * lax.rsqrt(ms + EPS) * g_ref[...]


def _ffn(x, h, w_up, conv_w, conv_b, prev0, prev1, w_down, gpost, *, seq, tm_up, tm_down, tf=512):
    m = x.shape[0]
    nf = D_FF // tf
    assert CONV_W == 3 and nf * tf == D_FF
    prow = 1 if seq else tm_up
    pspec = pl.BlockSpec((prow, tf), (lambda i, j: (0, j)) if seq else (lambda i, j: (i, j)))
    if seq:
        gspec = pl.BlockSpec((None, CONV_W - 1, tf), lambda i, j: (i, 0, j))
        gshape = (m // tm_up, CONV_W - 1, D_FF)
    else:
        gspec = pl.BlockSpec((tm_up, tf), lambda i, j: (i, j))
        gshape = (m, D_FF)
    act, g_out = pl.pallas_call(
        functools.partial(_ffn_up_kernel, seq=seq),
        grid=(m // tm_up, nf),
        in_specs=[pl.BlockSpec((tm_up, D_MODEL), lambda i, j: (i, 0)),
                  pl.BlockSpec((D_MODEL, tf), lambda i, j: (0, j)),
                  pl.BlockSpec((D_MODEL, tf), lambda i, j: (0, nf + j)),
                  pl.BlockSpec((CONV_W, tf), lambda i, j: (0, j)),
                  pl.BlockSpec((1, tf), lambda i, j: (0, j)),
                  pspec, pspec],
        out_specs=[pl.BlockSpec((tm_up, tf), lambda i, j: (i, j)), gspec],
        out_shape=[jax.ShapeDtypeStruct((m, D_FF), BF16), jax.ShapeDtypeStruct(gshape, F32)],
        scratch_shapes=[pltpu.VMEM((nf, 8, tf), F32)],
        compiler_params=_cparams("arbitrary", "arbitrary"),
        name="conv_ffn_up",
    )(h, w_up, w_up, conv_w, conv_b.reshape(1, D_FF), prev0, prev1)
    resident = pl.Buffered(1)
    y = pl.pallas_call(
        _ffn_down_kernel,
        grid=(m // tm_down,),
        in_specs=[pl.BlockSpec((tm_down, D_FF), lambda i: (i, 0)),
                  pl.BlockSpec((D_FF, D_MODEL), lambda i: (0, 0), pipeline_mode=resident),
                  pl.BlockSpec((tm_down, D_MODEL), lambda i: (i, 0)),
                  pl.BlockSpec((1, D_MODEL), lambda i: (0, 0), pipeline_mode=resident)],
        out_specs=pl.BlockSpec((tm_down, D_MODEL), lambda i: (i, 0)),
        out_shape=jax.ShapeDtypeStruct((m, D_MODEL), F32),
        compiler_params=_cparams("parallel"),
        name="ffn_down_residual",
    )(act, w_down, x, gpost.reshape(1, D_MODEL))
    return y, (g_out[-1] if seq else g_out)


def kernel(x_prompt, x_sample, cache_k, cache_v, state_hgrn, state_ffn_conv, page_table, rel_bias, hg_lb,
           norm_mix_pre, norm_mix_post, norm_ffn_pre, norm_ffn_post, w_in, hg_out_norm, w_branch_hgrn,
           w_branch_moba, w_out, w_ffn_up, ffn_conv_w, ffn_conv_b, w_ffn_down):
    nb, t, _ = x_prompt.shape
    db = x_sample.shape[0]
    depth = w_in.shape[0]
    assert nb == 1 and depth == 1 and x_sample.shape[1] == 1
    l = 0
    w_in_b = w_in[l].astype(BF16)
    w_bh = w_branch_hgrn[l].astype(BF16)
    w_bm = w_branch_moba[l].astype(BF16)
    w_o = w_out[l].astype(BF16)
    w_up = w_ffn_up[l]
    w_dn = w_ffn_down[l].astype(BF16)

    n_pool, page_rows = cache_k.shape[1], cache_k.shape[2]
    ck = cache_k.reshape(depth * n_pool, page_rows, H_ATT, ATT_DH)
    cv = cache_v.reshape(depth * n_pool, page_rows, H_ATT, ATT_DH)

    xp = x_prompt.reshape(t, D_MODEL)
    proj, k_new, v_new = _norm_matmul(xp, norm_mix_pre[l], w_in_b, tm=1024)
    o_hg, s_new = _hgrn_seq(proj, hg_lb, hg_out_norm[l])
    o_att, cache_block_sums = _moba_seq(proj, k_new, v_new, rel_bias, side_pool=ck, side_page_table=page_table)
    x1, h2 = _merge_out(o_hg, o_att, proj, w_bh, w_bm, w_o, xp, norm_mix_post[l], norm_ffn_pre[l], tm=256)
    zero_row = jnp.zeros((1, D_FF), F32)
    yp, conv_p = _ffn(x1, h2, w_up, ffn_conv_w[l], ffn_conv_b[l], zero_row, zero_row, w_dn,
                      norm_ffn_post[l], seq=True, tm_up=1024, tm_down=256)
    kp = k_new.reshape(1, 1, t, H_ATT, ATT_DH)
    vp = v_new.reshape(1, 1, t, H_ATT, ATT_DH)

    xs = x_sample.reshape(db, D_MODEL)
    projs, k_new_s, v_new_s = _norm_matmul(xs, norm_mix_pre[l], w_in_b, tm=db)
    o_hg_s, s_new_s = _hgrn_step(projs, state_hgrn[l], hg_lb, hg_out_norm[l])
    o_att_s = _moba_step(projs, k_new_s, v_new_s, cache_block_sums, ck, cv, page_table, rel_bias)
    x1s, h2s = _merge_out(o_hg_s.reshape(db, HG_VW), o_att_s, projs, w_bh, w_bm, w_o, xs, norm_mix_post[l],
                          norm_ffn_pre[l], tm=db)
    buf = state_ffn_conv[l]
    ys, g_s = _ffn(x1s, h2s, w_up, ffn_conv_w[l], ffn_conv_b[l], buf[:, 0], buf[:, 1], w_dn,
                   norm_ffn_post[l], seq=False, tm_up=db, tm_down=db)
    ks = k_new_s.reshape(1, db, 1, H_ATT, ATT_DH)
    vs = v_new_s.reshape(1, db, 1, H_ATT, ATT_DH)
    conv_s = jnp.stack([buf[:, 1], g_s], axis=1)

    return (yp.reshape(1, t, D_MODEL), ys.reshape(db, 1, D_MODEL), kp, vp,
            s_new.reshape(1, 1, H_HG, HG_DK, HG_DV), conv_p.reshape(1, 1, CONV_W - 1, D_FF),
            ks, vs, s_new_s.reshape(1, db, H_HG, HG_DK, HG_DV), conv_s.reshape(1, db, CONV_W - 1, D_FF))
```
